```python
import jax, jax.numpy as jnp
from jax import lax
import numpy as np

D_MODEL = 2048
BATCH = 1
SEQ = 16384
DEPTH = 1

HEAD_DIM = 128
HEADS_PER_GROUP = 4
DILATED_GROUPS = ((128, 1), (512, 4), (2048, 16))
N_ATTN_HEADS = HEADS_PER_GROUP * len(DILATED_GROUPS)
ATTN_WIDTH = N_ATTN_HEADS * HEAD_DIM
ATTN_OUT_WIDTH = HEADS_PER_GROUP * HEAD_DIM
ATTN_BLOCK = 128
ROPE_THETA = 10000.0
CONV_CHANNELS = D_MODEL // 2
CONV_WIDTH = 31
N_BRANCHES = 2
IN_SPLITS = [ATTN_WIDTH, 2 * ATTN_WIDTH, 3 * ATTN_WIDTH, 3 * ATTN_WIDTH + 2 * CONV_CHANNELS]
IN_COLS = 3 * ATTN_WIDTH + 2 * CONV_CHANNELS + N_BRANCHES * D_MODEL
PEER_HEADS = 8
PEER_N_KEYS = 128
PEER_N_EXPERTS = PEER_N_KEYS * PEER_N_KEYS
PEER_QUERY_DIM = 256
PEER_HALF = PEER_QUERY_DIM // 2
PEER_TOPK = 16
PEER_CHUNK = 128
NORM_EPS = 1e-6
MASK_VALUE = -1e30

kernel_name = 'hybrid_dilated_conformer_peer_block'


def rms_norm(x, gain):
    xf = x.astype(jnp.float32)
    y = xf * lax.rsqrt(jnp.mean(xf * xf, axis=-1, keepdims=True) + NORM_EPS)
    return (y * gain.astype(jnp.float32)).astype(x.dtype)


def layer_norm(x, gain, bias):
    xf = x.astype(jnp.float32)
    mu = jnp.mean(xf, axis=-1, keepdims=True)
    var = jnp.mean(jnp.square(xf - mu), axis=-1, keepdims=True)
    y = (xf - mu) * lax.rsqrt(var + NORM_EPS)
    return (y * gain.astype(jnp.float32) + bias.astype(jnp.float32)).astype(x.dtype)


def rotary(t, positions):
    half = HEAD_DIM // 2
    inv_freq = ROPE_THETA ** (-jnp.arange(0, HEAD_DIM, 2, dtype=jnp.float32) / HEAD_DIM)
    ang = positions.astype(jnp.float32)[..., None] * inv_freq
    cos = jnp.cos(ang)[:, :, None, :]
    sin = jnp.sin(ang)[:, :, None, :]
    tf = t.astype(jnp.float32)
    t1, t2 = tf[..., :half], tf[..., half:]
    return jnp.concatenate([t1 * cos - t2 * sin, t2 * cos + t1 * sin], axis=-1).astype(t.dtype)


def dilated_window_attention(q, k, v, window, dilation):
    B, S, H, Dh = q.shape
    L = S // dilation
    span = window // dilation
    nb = -(-L // ATTN_BLOCK)
    Lp = nb * ATTN_BLOCK

    def to_blocks(t):
        t = t.reshape(B, L, dilation, H, Dh).transpose(0, 2, 1, 3, 4)
        t = jnp.pad(t, ((0, 0), (0, 0), (0, Lp - L), (0, 0), (0, 0)))
        return t.reshape(B, dilation, nb, ATTN_BLOCK, H, Dh)

    def band(t):
        prev = jnp.pad(t, ((0, 0), (0, 0), (1, 0), (0, 0), (0, 0), (0, 0)))[:, :, :-1]
        return jnp.concatenate([prev, t], axis=3)

    qb = to_blocks(q)
    kb = band(to_blocks(k))
    vb = band(to_blocks(v))
    scores = jnp.einsum('brnqhc,brnkhc->brnhqk', qb, kb).astype(jnp.float32) * (Dh ** -0.5)
    qi = jnp.arange(ATTN_BLOCK)[:, None]
    kj = jnp.arange(2 * ATTN_BLOCK)[None, :]
    dist = ATTN_BLOCK + qi - kj
    key_idx = (jnp.arange(nb)[:, None, None] - 1) * ATTN_BLOCK + kj
    valid = (dist >= 0) & (dist <= span) & (key_idx >= 0)
    scores = jnp.where(valid[None, None, :, None], scores, MASK_VALUE)
    lse = jax.nn.logsumexp(scores, axis=-1)
    probs = jnp.exp(scores - lse[..., None]).astype(v.dtype)
    out = jnp.einsum('brnhqk,brnkhc->brnqhc', probs, vb)
    out = out.reshape(B, dilation, Lp, H, Dh)[:, :, :L].transpose(0, 2, 1, 3, 4).reshape(B, S, H, Dh)
    lse = lse.transpose(0, 1, 2, 4, 3).reshape(B, dilation, Lp, H)[:, :, :L]
    lse = lse.transpose(0, 2, 1, 3).reshape(B, S, H)
    return out, lse


def hybrid_mixer(h, positions, w_in, w_attn_o, conv_w, conv_b, conv_ln_g, conv_ln_b, w_conv_o, w_out):
    B, S, _ = h.shape
    proj = h @ w_in
    q, k, v, glu, gates = jnp.split(proj, IN_SPLITS, axis=-1)
    q = rotary(q.reshape(B, S, N_ATTN_HEADS, HEAD_DIM), positions)
    k = rotary(k.reshape(B, S, N_ATTN_HEADS, HEAD_DIM), positions)
    v = v.reshape(B, S, N_ATTN_HEADS, HEAD_DIM)

    outs, lses = [], []
    for g, (window, dilation) in enumerate(DILATED_GROUPS):
        hs = slice(g * HEADS_PER_GROUP, (g + 1) * HEADS_PER_GROUP)
        o, l = dilated_window_attention(q[:, :, hs], k[:, :, hs], v[:, :, hs], window, dilation)
        outs.append(o)
        lses.append(l)
    mix = jax.nn.softmax(jnp.stack(lses, axis=0), axis=0)
    attn = jnp.sum(mix[..., None] * jnp.stack(outs, axis=0).astype(jnp.float32), axis=0)
    attn = attn.astype(h.dtype).reshape(B, S, ATTN_OUT_WIDTH) @ w_attn_o

    a, b = jnp.split(glu, 2, axis=-1)
    u = a * jax.nn.sigmoid(b)
    u = lax.conv_general_dilated(u, conv_w, window_strides=(1,), padding=[(CONV_WIDTH - 1, 0)],
                                 dimension_numbers=('NWC', 'WIO', 'NWC'),
                                 feature_group_count=CONV_CHANNELS) + conv_b
    u = jax.nn.silu(layer_norm(u, conv_ln_g, conv_ln_b))
    conv = u @ w_conv_o

    g_attn, g_conv = jnp.split(jax.nn.sigmoid(gates), N_BRANCHES, axis=-1)
    return (g_attn * attn + g_conv * conv) @ w_out


def peer_ffn(h, w_q, sub_keys, expert_down, expert_up):
    B, S, D = h.shape
    q = (h @ w_q).reshape(B, S, PEER_HEADS, 2, PEER_HALF)
    s1 = jnp.einsum('bshc,hnc->bshn', q[..., 0, :], sub_keys[:, 0]).astype(jnp.float32)
    s2 = jnp.einsum('bshc,hnc->bshn', q[..., 1, :], sub_keys[:, 1]).astype(jnp.float32)
    v1, i1 = lax.top_k(s1, PEER_TOPK)
    v2, i2 = lax.top_k(s2, PEER_TOPK)
    cand = (v1[..., :, None] + v2[..., None, :]).reshape(B, S, PEER_HEADS, PEER_TOPK * PEER_TOPK)
    top, ci = lax.top_k(cand, PEER_TOPK)
    ea = jnp.take_along_axis(i1, ci // PEER_TOPK, axis=-1)
    eb = jnp.take_along_axis(i2, ci % PEER_TOPK, axis=-1)
    expert = ea * PEER_N_KEYS + eb
    gate = jax.nn.softmax(top, axis=-1)

    n_chunks = (B * S) // PEER_CHUNK
    xs = h.reshape(n_chunks, PEER_CHUNK, D)
    idx = expert.reshape(n_chunks, PEER_CHUNK, PEER_HEADS * PEER_TOPK)
    gs = gate.reshape(n_chunks, PEER_CHUNK, PEER_HEADS * PEER_TOPK)

    def chunk(args):
        xc, ic, gc = args
        pre = jnp.einsum('td,tkd->tk', xc, expert_down[ic]).astype(jnp.float32)
        act = jax.nn.gelu(pre, approximate=False) * gc
        return jnp.einsum('tk,tkd->td', act.astype(xc.dtype), expert_up[ic])

    out = lax.map(chunk, (xs, idx, gs))
    return out.reshape(B, S, D)


def setup_inputs(seed: int = 0) -> dict:
    key = jax.random.key(seed)
    ks = jax.random.split(key, 20)
    f32 = jnp.float32
    D = D_MODEL
    nrm = lambda k, shape, scale: jax.random.normal(k, shape, f32) * scale
    return {
        'x': nrm(ks[0], (BATCH, SEQ, D), 1.0),
        'c': nrm(ks[1], (BATCH, D), 1.0),
        'positions': (jnp.arange(SEQ, dtype=jnp.int32)[None, :]
                      + jax.random.randint(ks[2], (BATCH, 1), 0, 4096, dtype=jnp.int32)),
        'ada_w': nrm(ks[3], (DEPTH, D, 6 * D), 0.5 * D ** -0.5),
        'ada_b': nrm(ks[4], (DEPTH, 6 * D), 0.01),
        'norm_gains': 1.0 + nrm(ks[5], (DEPTH, 4, D), 0.05),
        'w_in': nrm(ks[6], (DEPTH, D, IN_COLS), D ** -0.5),
        'w_attn_o': nrm(ks[7], (DEPTH, ATTN_OUT_WIDTH, D), ATTN_OUT_WIDTH ** -0.5),
        'conv_w': nrm(ks[8], (DEPTH, CONV_WIDTH, 1, CONV_CHANNELS), CONV_WIDTH ** -0.5),
        'conv_b': nrm(ks[9], (DEPTH, CONV_CHANNELS), 0.01),
        'conv_ln_g': 1.0 + nrm(ks[10], (DEPTH, CONV_CHANNELS), 0.05),
        'conv_ln_b': nrm(ks[11], (DEPTH, CONV_CHANNELS), 0.01),
        'w_conv_o': nrm(ks[12], (DEPTH, CONV_CHANNELS, D), CONV_CHANNELS ** -0.5),
        'w_out': nrm(ks[13], (DEPTH, D, D), D ** -0.5),
        'peer_w_q': nrm(ks[14], (DEPTH, D, PEER_HEADS * PEER_QUERY_DIM), D ** -0.5),
        'peer_sub_keys': nrm(ks[15], (DEPTH, PEER_HEADS, 2, PEER_N_KEYS, PEER_HALF), PEER_HALF ** -0.5),
        'peer_down': nrm(ks[16], (DEPTH, PEER_N_EXPERTS, D), D ** -0.5),
        'peer_up': nrm(ks[17], (DEPTH, PEER_N_EXPERTS, D), (PEER_HEADS * PEER_TOPK) ** -0.5),
    }


def reference(x, c, positions, ada_w, ada_b, norm_gains, w_in, w_attn_o, conv_w, conv_b,
              conv_ln_g, conv_ln_b, w_conv_o, w_out, peer_w_q, peer_sub_keys, peer_down, peer_up):
    for l in range(DEPTH):
        mod = jax.nn.silu(c) @ ada_w[l] + ada_b[l]
        shift1, scale1, gate1, shift2, scale2, gate2 = jnp.split(mod[:, None, :], 6, axis=-1)
        g = norm_gains[l]
        h = rms_norm(x, g[0]) * (1.0 + scale1) + shift1
        y = hybrid_mixer(h, positions, w_in[l], w_attn_o[l], conv_w[l], conv_b[l],
                         conv_ln_g[l], conv_ln_b[l], w_conv_o[l], w_out[l])
        x = x + gate1 * rms_norm(y, g[1])
        h = rms_norm(x, g[2]) * (1.0 + scale2) + shift2
        y = peer_ffn(h, peer_w_q[l], peer_sub_keys[l], peer_down[l], peer_up[l])
        x = x + gate2 * rms_norm(y, g[3])
    return x
```

```python
import functools
import math

import jax
import jax.numpy as jnp
from jax import lax
from jax.experimental import pallas as pl
from jax.experimental.pallas import tpu as pltpu

F32 = jnp.float32
BF16 = jnp.bfloat16

HEAD_DIM = 128
HEADS_PER_GROUP = 4
DILATED_GROUPS = ((128, 1), (512, 4), (2048, 16))
N_GROUPS = len(DILATED_GROUPS)
GROUP_WIDTH = HEADS_PER_GROUP * HEAD_DIM
ATTN_WIDTH = N_GROUPS * GROUP_WIDTH
ATTN_BLOCK = 128
ROPE_THETA = 10000.0
CONV_WIDTH = 31
CONV_HALO = 32
PEER_HEADS = 8
PEER_N_KEYS = 128
PEER_HALF = 128
PEER_TOPK = 16
NORM_EPS = 1e-6
MASK_VALUE = -1e30
NEG_INF = float("-inf")
INV_SQRT2 = 1.0 / math.sqrt(2.0)

COL = 512
VMEM_LIMIT = 56 * 1024 * 1024

_CAND_ROWS = [PEER_TOPK // (a + 1) for a in range(PEER_TOPK)]
_N_CAND = sum(_CAND_ROWS)
_CAND_PAD = -(-_N_CAND // 8) * 8


def _params(sem):
    return pltpu.CompilerParams(dimension_semantics=sem, vmem_limit_bytes=VMEM_LIMIT)


def _rms(x):
    return x * lax.rsqrt(jnp.mean(x * x, axis=-1, keepdims=True) + NORM_EPS)


def _mod_kernel(c_ref, w_ref, b_ref, o_ref):
    rows = 256

    def body(k, acc):
        r = pl.multiple_of(k * rows, rows)
        c = c_ref[pl.ds(r, rows), :]
        sc = c * jax.nn.sigmoid(c)
        return acc + jnp.sum(w_ref[pl.ds(r, rows), :] * sc, axis=0, keepdims=True)

    acc = lax.fori_loop(0, w_ref.shape[0] // rows, body, jnp.zeros(o_ref.shape, F32))
    o_ref[...] = acc + b_ref[...]


def _modulation(c_col, w, b):
    d, n = w.shape
    tn = 512
    return pl.pallas_call(
        _mod_kernel,
        out_shape=jax.ShapeDtypeStruct((1, n), F32),
        grid=(n // tn,),
        in_specs=[pl.BlockSpec((d, 1), lambda j: (0, 0)),
                  pl.BlockSpec((d, tn), lambda j: (0, j)),
                  pl.BlockSpec((1, tn), lambda j: (0, j))],
        out_specs=pl.BlockSpec((1, tn), lambda j: (0, j)),
        compiler_params=_params(("parallel",)),
        name="mod",
    )(c_col, w, b)


def _rope_kernel(pos_ref, invf_ref, cos_ref, sin_ref):
    ang = pos_ref[...] * invf_ref[...]
    cos_ref[...] = jnp.cos(ang)
    s = jnp.sin(ang)
    lane = lax.broadcasted_iota(jnp.int32, s.shape, 1)
    sin_ref[...] = jnp.where(lane < HEAD_DIM // 2, -s, s)


def _rope_tables(pos_col, invf_row):
    s = pos_col.shape[0]
    ts = 2048
    return pl.pallas_call(
        _rope_kernel,
        out_shape=(jax.ShapeDtypeStruct((s, HEAD_DIM), F32),) * 2,
        grid=(s // ts,),
        in_specs=[pl.BlockSpec((ts, 1), lambda i: (i, 0)),
                  pl.BlockSpec((1, HEAD_DIM), lambda i: (0, 0))],
        out_specs=(pl.BlockSpec((ts, HEAD_DIM), lambda i: (i, 0)),) * 2,
        compiler_params=_params(("parallel",)),
        name="rope",
    )(pos_col, invf_row)


def _norm_mod_kernel(x_ref, g_ref, scale_ref, shift_ref, o_ref):
    y = _rms(x_ref[...]) * g_ref[...]
    o_ref[...] = (y * (1.0 + scale_ref[...]) + shift_ref[...]).astype(o_ref.dtype)


def _norm_mod(x, g, scale, shift):
    s, d = x.shape
    tm = 512
    vec = pl.BlockSpec((1, d), lambda i: (0, 0))
    return pl.pallas_call(
        _norm_mod_kernel,
        out_shape=jax.ShapeDtypeStruct((s, d), BF16),
        grid=(s // tm,),
        in_specs=[pl.BlockSpec((tm, d), lambda i: (i, 0)), vec, vec, vec],
        out_specs=pl.BlockSpec((tm, d), lambda i: (i, 0)),
        compiler_params=_params(("parallel",)),
        name="norm_mod",
    )(x, g, scale, shift)


def _qkv_kernel(h_ref, w_ref, cos_ref, sin_ref, o_ref):
    acc = jnp.dot(h_ref[...], w_ref[...], preferred_element_type=F32)
    j = pl.program_id(1)

    @pl.when(j < 2 * N_GROUPS)
    def _():
        cos = cos_ref[...]
        sin = sin_ref[...]
        for hh in range(HEADS_PER_GROUP):
            cols = slice(hh * HEAD_DIM, (hh + 1) * HEAD_DIM)
            t = acc[:, cols]
            o_ref[:, cols] = (t * cos + pltpu.roll(t, HEAD_DIM // 2, axis=1) * sin).astype(o_ref.dtype)

    @pl.when(j >= 2 * N_GROUPS)
    def _():
        o_ref[...] = acc.astype(o_ref.dtype)


def _qkv_proj(h, w_in, cos, sin):
    s, d = h.shape
    tm = 1024
    n_col = 3 * ATTN_WIDTH // COL
    return pl.pallas_call(
        _qkv_kernel,
        out_shape=jax.ShapeDtypeStruct((s, 3 * ATTN_WIDTH), BF16),
        grid=(s // tm, n_col),
        in_specs=[pl.BlockSpec((tm, d), lambda i, j: (i, 0)),
                  pl.BlockSpec((d, COL), lambda i, j: (0, j)),
                  pl.BlockSpec((tm, HEAD_DIM), lambda i, j: (i, 0)),
                  pl.BlockSpec((tm, HEAD_DIM), lambda i, j: (i, 0))],
        out_specs=pl.BlockSpec((tm, COL), lambda i, j: (i, j)),
        compiler_params=_params(("parallel", "arbitrary")),
        name="qkv",
    )(h, w_in, cos, sin)


def _glu_kernel(h_ref, wa_ref, wb_ref, o_ref):
    h = h_ref[...]
    a = jnp.dot(h, wa_ref[...], preferred_element_type=F32)
    b = jnp.dot(h, wb_ref[...], preferred_element_type=F32)
    o_ref[...] = a * jax.nn.sigmoid(b)


def _glu_proj(h, w_in, conv_channels):
    s, d = h.shape
    tm = 1024
    a0 = 3 * ATTN_WIDTH // COL
    nb = conv_channels // COL
    return pl.pallas_call(
        _glu_kernel,
        out_shape=jax.ShapeDtypeStruct((s, conv_channels), F32),
        grid=(s // tm, nb),
        in_specs=[pl.BlockSpec((tm, d), lambda i, j: (i, 0)),
                  pl.BlockSpec((d, COL), lambda i, j: (0, a0 + j)),
                  pl.BlockSpec((d, COL), lambda i, j: (0, a0 + nb + j))],
        out_specs=pl.BlockSpec((tm, COL), lambda i, j: (i, j)),
        compiler_params=_params(("parallel", "arbitrary")),
        name="glu",
    )(h, w_in, w_in)


def _gates_kernel(h_ref, w_ref, o_ref):
    o_ref[...] = jax.nn.sigmoid(jnp.dot(h_ref[...], w_ref[...], preferred_element_type=F32))


def _gates_proj(h, w_in, col0, width):
    s, d = h.shape
    tm = 1024
    c0 = col0 // COL
    return pl.pallas_call(
        _gates_kernel,
        out_shape=jax.ShapeDtypeStruct((s, width), F32),
        grid=(s // tm, width // COL),
        in_specs=[pl.BlockSpec((tm, d), lambda i, j: (i, 0)),
                  pl.BlockSpec((d, COL), lambda i, j: (0, c0 + j))],
        out_specs=pl.BlockSpec((tm, COL), lambda i, j: (i, j)),
        compiler_params=_params(("parallel", "arbitrary")),
        name="gates",
    )(h, w_in)


ATTN_Q_BLOCKS = 4


def _attn_kernel(q_ref, kc_ref, kp_ref, vc_ref, vp_ref, o_ref, l_ref):
    n = pl.program_id(1)
    qi = lax.broadcasted_iota(jnp.int32, (ATTN_BLOCK, 2 * ATTN_BLOCK), 0)
    kj = lax.broadcasted_iota(jnp.int32, (ATTN_BLOCK, 2 * ATTN_BLOCK), 1)
    band = jnp.abs(kj - qi - ATTN_BLOCK // 2) <= ATTN_BLOCK // 2
    first_lo = jnp.where(n == 0, ATTN_BLOCK, 0)
    band_first = band & (kj >= first_lo)
    scale = HEAD_DIM ** -0.5
    for hh in range(HEADS_PER_GROUP):
        cols = slice(hh * HEAD_DIM, (hh + 1) * HEAD_DIM)
        for b in range(ATTN_Q_BLOCKS):
            rows = slice(b * ATTN_BLOCK, (b + 1) * ATTN_BLOCK)
            q = q_ref[rows, cols]
            if b == 0:
                k = jnp.concatenate([kp_ref[:, cols], kc_ref[0:ATTN_BLOCK, cols]], axis=0)
                v = jnp.concatenate([vp_ref[:, cols], vc_ref[0:ATTN_BLOCK, cols]], axis=0)
                valid = band_first
            else:
                band_rows = slice((b - 1) * ATTN_BLOCK, (b + 1) * ATTN_BLOCK)
                k = kc_ref[band_rows, cols]
                v = vc_ref[band_rows, cols]
                valid = band
            s = lax.dot_general(q, k, (((1,), (1,)), ((), ())), preferred_element_type=F32) * scale
            s = jnp.where(valid, s, MASK_VALUE)
            m = jnp.max(s, axis=-1, keepdims=True)
            p = jnp.exp(s - m)
            den = jnp.sum(p, axis=-1, keepdims=True)
            o = jnp.dot(p.astype(BF16), v, preferred_element_type=F32)
            o_ref[rows, cols] = o * (1.0 / den)
            l_ref[rows, cols] = jnp.broadcast_to(m + jnp.log(den), (ATTN_BLOCK, HEAD_DIM))


def _dilated_attention(qkv, group, dilation):
    s = qkv.shape[0]
    sub_len = s // dilation
    qt = ATTN_Q_BLOCKS * ATTN_BLOCK
    per_tok = 3 * ATTN_WIDTH // COL
    view = qkv.reshape(sub_len, dilation * 3 * ATTN_WIDTH)
    qc, kc, vc = group, N_GROUPS + group, 2 * N_GROUPS + group

    def cur(c):
        return pl.BlockSpec((qt, COL), lambda r, n: (n, r * per_tok + c))

    def prev(c):
        return pl.BlockSpec((ATTN_BLOCK, COL),
                            lambda r, n: (jnp.maximum(n * ATTN_Q_BLOCKS - 1, 0), r * per_tok + c))

    out_sds = jax.ShapeDtypeStruct((sub_len, dilation * GROUP_WIDTH), F32)
    out_spec = pl.BlockSpec((qt, GROUP_WIDTH), lambda r, n: (n, r))
    o, lse = pl.pallas_call(
        _attn_kernel,
        out_shape=(out_sds, out_sds),
        grid=(dilation, sub_len // qt),
        in_specs=[cur(qc), cur(kc), prev(kc), cur(vc), prev(vc)],
        out_specs=(out_spec, out_spec),
        compiler_params=_params(("parallel", "arbitrary")),
        name=f"attn_d{dilation}",
    )(view, view, view, view, view)
    return o.reshape(s, GROUP_WIDTH), lse.reshape(s, GROUP_WIDTH)


def _mix_kernel(o0_ref, o1_ref, o2_ref, l0_ref, l1_ref, l2_ref, u_ref, uh_ref, ga_ref, gc_ref,
                wao_ref, cw_ref, cb_ref, lg_ref, lb_ref, wco_ref, out_ref, ucat_ref):
    i = pl.program_id(0)
    tm = u_ref.shape[0]
    l0, l1, l2 = l0_ref[...], l1_ref[...], l2_ref[...]
    m = jnp.maximum(jnp.maximum(l0, l1), l2)
    e0, e1, e2 = jnp.exp(l0 - m), jnp.exp(l1 - m), jnp.exp(l2 - m)
    attn = (e0 * o0_ref[...] + e1 * o1_ref[...] + e2 * o2_ref[...]) / (e0 + e1 + e2)
    attn_o = jnp.dot(attn.astype(BF16), wao_ref[...], preferred_element_type=F32)
    ucat_ref[0:CONV_HALO, :] = jnp.where(i > 0, uh_ref[...], 0.0)
    ucat_ref[CONV_HALO:, :] = u_ref[...]
    off = CONV_HALO - (CONV_WIDTH - 1)
    conv = jnp.zeros(u_ref.shape, F32) + cb_ref[...]
    for w in range(CONV_WIDTH):
        conv = conv + ucat_ref[off + w:off + w + tm, :] * cw_ref[w:w + 1, :]
    mu = jnp.mean(conv, axis=-1, keepdims=True)
    cen = conv - mu
    var = jnp.mean(cen * cen, axis=-1, keepdims=True)
    y = cen * lax.rsqrt(var + NORM_EPS) * lg_ref[...] + lb_ref[...]
    y = y * jax.nn.sigmoid(y)
    conv_o = jnp.dot(y.astype(BF16), wco_ref[...], preferred_element_type=F32)
    out_ref[...] = (ga_ref[...] * attn_o + gc_ref[...] * conv_o).astype(out_ref.dtype)


def _mix(outs, lses, u, gates, w_attn_o, conv_w, conv_b, ln_g, ln_b, w_conv_o):
    s, cc = u.shape
    d = w_attn_o.shape[1]
    tm = 256
    row = lambda i: (i, 0)
    const = lambda i: (0, 0)
    grp = pl.BlockSpec((tm, GROUP_WIDTH), row)
    cvec = pl.BlockSpec((1, cc), const)
    return pl.pallas_call(
        _mix_kernel,
        out_shape=jax.ShapeDtypeStruct((s, d), BF16),
        grid=(s // tm,),
        in_specs=[grp] * 6 + [
            pl.BlockSpec((tm, cc), row),
            pl.BlockSpec((CONV_HALO, cc), lambda i: (jnp.maximum(i * (tm // CONV_HALO) - 1, 0), 0)),
            pl.BlockSpec((tm, d), lambda i: (i, 0)),
            pl.BlockSpec((tm, d), lambda i: (i, 1)),
            pl.BlockSpec((GROUP_WIDTH, d), const),
            pl.BlockSpec((CONV_WIDTH, cc), const),
            cvec, cvec, cvec,
            pl.BlockSpec((cc, d), const)],
        out_specs=pl.BlockSpec((tm, d), row),
        scratch_shapes=[pltpu.VMEM((tm + CONV_HALO, cc), F32)],
        compiler_params=_params(("parallel",)),
        name="mix",
    )(*outs, *lses, u, u, gates, gates, w_attn_o, conv_w, conv_b, ln_g, ln_b, w_conv_o)


def _out_proj_kernel(m_ref, w_ref, x_ref, gate_ref, g1_ref, g2_ref, scale_ref, shift_ref,
                     x1_ref, h2t_ref):
    y = jnp.dot(m_ref[...], w_ref[...], preferred_element_type=F32)
    x1 = x_ref[...] + gate_ref[...] * (_rms(y) * g1_ref[...])
    x1_ref[...] = x1
    h2 = (_rms(x1) * g2_ref[...]) * (1.0 + scale_ref[...]) + shift_ref[...]
    h2t_ref[...] = h2.T.astype(h2t_ref.dtype)


def _out_proj(merged, w_out, x, gate1, g1, g2, scale2, shift2):
    s, d = x.shape
    tm = 256
    row = lambda i: (i, 0)
    vec = pl.BlockSpec((1, d), lambda i: (0, 0))
    return pl.pallas_call(
        _out_proj_kernel,
        out_shape=(jax.ShapeDtypeStruct((s, d), F32), jax.ShapeDtypeStruct((d, s), BF16)),
        grid=(s // tm,),
        in_specs=[pl.BlockSpec((tm, d), row), pl.BlockSpec((d, d), lambda i: (0, 0)),
                  pl.BlockSpec((tm, d), row), vec, vec, vec, vec, vec],
        out_specs=(pl.BlockSpec((tm, d), row), pl.BlockSpec((d, tm), lambda i: (0, i))),
        compiler_params=_params(("parallel",)),
        name="out_proj",
    )(merged, w_out, x, gate1, g1, g2, scale2, shift2)


def _top_values(work, count):
    row = lax.broadcasted_iota(jnp.int32, work.shape, 0)
    vals = []
    for _ in range(count):
        m = jnp.max(work, axis=0, keepdims=True)
        vals.append(m)
        first = jnp.min(jnp.where(work == m, row, work.shape[0]), axis=0, keepdims=True)
        work = jnp.where(row == first, NEG_INF, work)
    return vals


def _peer_query_kernel(h2t_ref, wqt_ref, keys_ref, s1_ref, e1_ref, s2_ref, e2_ref, tau_ref,
                       qt_ref, top_ref, cand_ref):
    qt_ref[...] = jnp.dot(wqt_ref[...], h2t_ref[...], preferred_element_type=F32)
    for h in range(PEER_HEADS):
        shifted = []
        for side in range(2):
            r0 = (2 * h + side) * PEER_HALF
            qc = qt_ref[r0:r0 + PEER_HALF, :].astype(BF16)
            sc = jnp.dot(keys_ref[h, side].astype(BF16), qc, preferred_element_type=F32)
            vals = _top_values(sc, PEER_TOPK)
            for k in range(PEER_TOPK):
                top_ref[side, k:k + 1, :] = vals[k] - vals[0]
            shifted.append(sc - vals[0])
        cand_ref[...] = jnp.full(cand_ref.shape, NEG_INF, F32)
        off = 0
        for a, nb in enumerate(_CAND_ROWS):
            cand_ref[off:off + nb, :] = top_ref[0, a:a + 1, :] + top_ref[1, 0:nb, :]
            off += nb
        cand = cand_ref[...]
        tau = _top_values(cand, PEER_TOPK)[-1]
        z = jnp.sum(jnp.where(cand >= tau, jnp.exp(cand), 0.0), axis=0, keepdims=True)
        inv_z = 1.0 / z
        s1 = shifted[0]
        e1 = jnp.exp(s1) * inv_z
        for blk in range(PEER_N_KEYS // 8):
            s1_ref[h, blk] = s1[blk * 8:(blk + 1) * 8, :]
            e1_ref[h, blk] = e1[blk * 8:(blk + 1) * 8, :]
        s2_ref[h] = shifted[1]
        e2_ref[h] = jnp.exp(shifted[1])
        tau_ref[h:h + 1, :] = tau


def _peer_query(h2t, wq_t, sub_keys):
    d, s = h2t.shape
    tt = 512
    nq = wq_t.shape[0]
    side_blocked = jax.ShapeDtypeStruct((PEER_HEADS, PEER_N_KEYS // 8, 8, s), F32)
    side_flat = jax.ShapeDtypeStruct((PEER_HEADS, PEER_N_KEYS, s), F32)
    blocked_spec = pl.BlockSpec((PEER_HEADS, PEER_N_KEYS // 8, 8, tt), lambda t: (0, 0, 0, t))
    flat_spec = pl.BlockSpec((PEER_HEADS, PEER_N_KEYS, tt), lambda t: (0, 0, t))
    return pl.pallas_call(
        _peer_query_kernel,
        out_shape=(side_blocked, side_blocked, side_flat, side_flat,
                   jax.ShapeDtypeStruct((PEER_HEADS, s), F32)),
        grid=(s // tt,),
        in_specs=[pl.BlockSpec((d, tt), lambda t: (0, t)),
                  pl.BlockSpec((nq, d), lambda t: (0, 0)),
                  pl.BlockSpec(sub_keys.shape, lambda t: (0, 0, 0, 0))],
        out_specs=(blocked_spec, blocked_spec, flat_spec, flat_spec,
                   pl.BlockSpec((PEER_HEADS, tt), lambda t: (0, t))),
        scratch_shapes=[pltpu.VMEM((nq, tt), F32),
                        pltpu.VMEM((2, PEER_TOPK, tt), F32),
                        pltpu.VMEM((_CAND_PAD, tt), F32)],
        compiler_params=_params(("parallel",)),
        name="peer_query",
    )(h2t, wq_t, sub_keys)


PEER_LANE_CHUNK = 128


def _peer_dense_kernel(h2t_ref, down_ref, up_ref, s1_ref, e1_ref, s2_ref, e2_ref, tau_ref,
                       x1_ref, gate_ref, g_ref, out_ref, pre_ref, act_ref):
    e = pl.program_id(1)
    tt = h2t_ref.shape[1]

    @pl.when(e == 0)
    def _():
        out_ref[...] = jnp.zeros(out_ref.shape, F32)

    pre_ref[...] = jnp.dot(down_ref[...], h2t_ref[...], preferred_element_type=F32)
    for ii in range(8):
        rows = slice(ii * PEER_N_KEYS, (ii + 1) * PEER_N_KEYS)
        for c in range(tt // PEER_LANE_CHUNK):
            lanes = slice(c * PEER_LANE_CHUNK, (c + 1) * PEER_LANE_CHUNK)
            w = jnp.zeros((PEER_N_KEYS, PEER_LANE_CHUNK), F32)
            for h in range(PEER_HEADS):
                z = s1_ref[h, 0, ii:ii + 1, lanes] + s2_ref[h, :, lanes]
                p = e1_ref[h, 0, ii:ii + 1, lanes] * e2_ref[h, :, lanes]
                w = w + jnp.where(z >= tau_ref[h:h + 1, lanes], p, 0.0)
            pre = pre_ref[rows, lanes]
            act = 0.5 * pre * (1.0 + lax.erf(pre * INV_SQRT2)) * w
            act_ref[rows, lanes] = act.astype(act_ref.dtype)
    out_ref[...] += lax.dot_general(act_ref[...], up_ref[...], (((0,), (0,)), ((), ())),
                                    preferred_element_type=F32)

    @pl.when(e == pl.num_programs(1) - 1)
    def _():
        y = out_ref[...]
        out_ref[...] = x1_ref[...] + gate_ref[...] * (_rms(y) * g_ref[...])


def _peer_dense(h2t, down, up, s1, e1, s2, e2, tau, x1, gate2, g3):
    d, s = h2t.shape
    n_exp = down.shape[0]
    tt = 512
    et = 8 * PEER_N_KEYS
    vec = pl.BlockSpec((1, d), lambda t, e: (0, 0))
    blocked_spec = pl.BlockSpec((PEER_HEADS, 1, 8, tt), lambda t, e: (0, e, 0, t))
    flat_spec = pl.BlockSpec((PEER_HEADS, PEER_N_KEYS, tt), lambda t, e: (0, 0, t))
    return pl.pallas_call(
        _peer_dense_kernel,
        out_shape=jax.ShapeDtypeStruct((s, d), F32),
        grid=(s // tt, n_exp // et),
        in_specs=[pl.BlockSpec((d, tt), lambda t, e: (0, t)),
                  pl.BlockSpec((et, d), lambda t, e: (e, 0)),
                  pl.BlockSpec((et, d), lambda t, e: (e, 0)),
                  blocked_spec, blocked_spec, flat_spec, flat_spec,
                  pl.BlockSpec((PEER_HEADS, tt), lambda t, e: (0, t)),
                  pl.BlockSpec((tt, d), lambda t, e: (t, 0)),
                  vec, vec],
        out_specs=pl.BlockSpec((tt, d), lambda t, e: (t, 0)),
        scratch_shapes=[pltpu.VMEM((et, tt), F32), pltpu.VMEM((et, tt), BF16)],
        compiler_params=_params(("parallel", "arbitrary")),
        name="peer_dense",
    )(h2t, down, up, s1, e1, s2, e2, tau, x1, gate2, g3)


def kernel(x, c, positions, ada_w, ada_b, norm_gains, w_in, w_attn_o, conv_w, conv_b, conv_ln_g,
           conv_ln_b, w_conv_o, w_out, peer_w_q, peer_sub_keys, peer_down, peer_up):
    batch, seq, d = x.shape
    depth = ada_w.shape[0]
    assert batch == 1, "kernels are written for a single sequence"
    cc = conv_w.shape[-1]
    xs = x[0]
    inv_freq = ROPE_THETA ** (-jnp.arange(0, HEAD_DIM, 2, dtype=F32) / HEAD_DIM)
    invf_row = jnp.concatenate([inv_freq, inv_freq])[None, :]
    cos, sin = _rope_tables(positions[0].astype(F32)[:, None], invf_row)
    row = lambda v: v[None, :]
    for l in range(depth):
        mod = _modulation(c[0][:, None], ada_w[l], ada_b[l][None, :])
        shift1, scale1, gate1, shift2, scale2, gate2 = [mod[:, k * d:(k + 1) * d] for k in range(6)]
        g = norm_gains[l]
        w_in_b = w_in[l].astype(BF16)

        h = _norm_mod(xs, row(g[0]), scale1, shift1)
        qkv = _qkv_proj(h, w_in_b, cos, sin)
        u = _glu_proj(h, w_in_b, cc)
        gates = _gates_proj(h, w_in_b, 3 * ATTN_WIDTH + 2 * cc, 2 * d)
        outs, lses = [], []
        for gi, (_, dilation) in enumerate(DILATED_GROUPS):
            o, lse = _dilated_attention(qkv, gi, dilation)
            outs.append(o)
            lses.append(lse)
        merged = _mix(outs, lses, u, gates, w_attn_o[l].astype(BF16), conv_w[l][:, 0, :],
                      row(conv_b[l]), row(conv_ln_g[l]), row(conv_ln_b[l]), w_conv_o[l].astype(BF16))
        x1, h2t = _out_proj(merged, w_out[l].astype(BF16), xs, gate1, row(g[1]), row(g[2]),
                            scale2, shift2)

        s1, e1, s2, e2, tau = _peer_query(h2t, peer_w_q[l].T.astype(BF16), peer_sub_keys[l])
        xs = _peer_dense(h2t, peer_down[l].astype(BF16), peer_up[l].astype(BF16),
                         s1, e1, s2, e2, tau, x1, gate2, row(g[3]))
    return xs[None]
```

```python
import functools
import math

import jax
import jax.numpy as jnp
from jax import lax
from jax.experimental import pallas as pl
from jax.experimental.pallas import tpu as pltpu

F32 = jnp.float32
BF16 = jnp.bfloat16

HEAD_DIM = 128
HEADS_PER_GROUP = 4
DILATED_GROUPS = ((128, 1), (512, 4), (2048, 16))
N_GROUPS = len(DILATED_GROUPS)
GROUP_WIDTH = HEADS_PER_GROUP * HEAD_DIM
ATTN_WIDTH = N_GROUPS * GROUP_WIDTH
ATTN_BLOCK = 128
ROPE_THETA = 10000.0
CONV_WIDTH = 31
CONV_HALO = 32
PEER_HEADS = 8
PEER_N_KEYS = 128
PEER_HALF = 128
PEER_TOPK = 16
NORM_EPS = 1e-6
MASK_VALUE = -1e30
NEG_INF = float("-inf")
INV_SQRT2 = 1.0 / math.sqrt(2.0)

COL = 512
VMEM_LIMIT = 56 * 1024 * 1024

_CAND_ROWS = [PEER_TOPK // (a + 1) for a in range(PEER_TOPK)]
_N_CAND = sum(_CAND_ROWS)
_CAND_PAD = -(-_N_CAND // 8) * 8


def _params(sem):
    return pltpu.CompilerParams(dimension_semantics=sem, vmem_limit_bytes=VMEM_LIMIT)


def _rms(x):
    return x * lax.rsqrt(jnp.mean(x * x, axis=-1, keepdims=True) + NORM_EPS)


def _mod_kernel(c_ref, w_ref, b_ref, o_ref):
    rows = 256

    def body(k, acc):
        r = pl.multiple_of(k * rows, rows)
        c = c_ref[pl.ds(r, rows), :]
        sc = c * jax.nn.sigmoid(c)
        return acc + jnp.sum(w_ref[pl.ds(r, rows), :] * sc, axis=0, keepdims=True)

    acc = lax.fori_loop(0, w_ref.shape[0] // rows, body, jnp.zeros(o_ref.shape, F32))
    o_ref[...] = acc + b_ref[...]


def _modulation(c_col, w, b):
    d, n = w.shape
    tn = 512
    return pl.pallas_call(
        _mod_kernel,
        out_shape=jax.ShapeDtypeStruct((1, n), F32),
        grid=(n // tn,),
        in_specs=[pl.BlockSpec((d, 1), lambda j: (0, 0)),
                  pl.BlockSpec((d, tn), lambda j: (0, j)),
                  pl.BlockSpec((1, tn), lambda j: (0, j))],
        out_specs=pl.BlockSpec((1, tn), lambda j: (0, j)),
        compiler_params=_params(("parallel",)),
        name="mod",
    )(c_col, w, b)


def _rope_kernel(pos_ref, invf_ref, cos_ref, sin_ref):
    ang = pos_ref[...] * invf_ref[...]
    cos_ref[...] = jnp.cos(ang)
    s = jnp.sin(ang)
    lane = lax.broadcasted_iota(jnp.int32, s.shape, 1)
    sin_ref[...] = jnp.where(lane < HEAD_DIM // 2, -s, s)


def _rope_tables(pos_col, invf_row):
    s = pos_col.shape[0]
    ts = 2048
    return pl.pallas_call(
        _rope_kernel,
        out_shape=(jax.ShapeDtypeStruct((s, HEAD_DIM), F32),) * 2,
        grid=(s // ts,),
        in_specs=[pl.BlockSpec((ts, 1), lambda i: (i, 0)),
                  pl.BlockSpec((1, HEAD_DIM), lambda i: (0, 0))],
        out_specs=(pl.BlockSpec((ts, HEAD_DIM), lambda i: (i, 0)),) * 2,
        compiler_params=_params(("parallel",)),
        name="rope",
    )(pos_col, invf_row)


def _norm_mod_kernel(x_ref, g_ref, scale_ref, shift_ref, o_ref):
    y = _rms(x_ref[...]) * g_ref[...]
    o_ref[...] = (y * (1.0 + scale_ref[...]) + shift_ref[...]).astype(o_ref.dtype)


def _norm_mod(x, g, scale, shift):
    s, d = x.shape
    tm = 512
    vec = pl.BlockSpec((1, d), lambda i: (0, 0))
    return pl.pallas_call(
        _norm_mod_kernel,
        out_shape=jax.ShapeDtypeStruct((s, d), BF16),
        grid=(s // tm,),
        in_specs=[pl.BlockSpec((tm, d), lambda i: (i, 0)), vec, vec, vec],
        out_specs=pl.BlockSpec((tm, d), lambda i: (i, 0)),
        compiler_params=_params(("parallel",)),
        name="norm_mod",
    )(x, g, scale, shift)


def _qkv_kernel(h_ref, w_ref, cos_ref, sin_ref, o_ref):
    acc = jnp.dot(h_ref[...], w_ref[...], preferred_element_type=F32)
    j = pl.program_id(1)

    @pl.when(j < 2 * N_GROUPS)
    def _():
        cos = cos_ref[...]
        sin = sin_ref[...]
        for hh in range(HEADS_PER_GROUP):
            cols = slice(hh * HEAD_DIM, (hh + 1) * HEAD_DIM)
            t = acc[:, cols]
            o_ref[:, cols] = (t * cos + pltpu.roll(t, HEAD_DIM // 2, axis=1) * sin).astype(o_ref.dtype)

    @pl.when(j >= 2 * N_GROUPS)
    def _():
        o_ref[...] = acc.astype(o_ref.dtype)


def _qkv_proj(h, w_in, cos, sin):
    s, d = h.shape
    tm = 1024
    n_col = 3 * ATTN_WIDTH // COL
    return pl.pallas_call(
        _qkv_kernel,
        out_shape=jax.ShapeDtypeStruct((s, 3 * ATTN_WIDTH), BF16),
        grid=(s // tm, n_col),
        in_specs=[pl.BlockSpec((tm, d), lambda i, j: (i, 0)),
                  pl.BlockSpec((d, COL), lambda i, j: (0, j)),
                  pl.BlockSpec((tm, HEAD_DIM), lambda i, j: (i, 0)),
                  pl.BlockSpec((tm, HEAD_DIM), lambda i, j: (i, 0))],
        out_specs=pl.BlockSpec((tm, COL), lambda i, j: (i, j)),
        compiler_params=_params(("parallel", "arbitrary")),
        name="qkv",
    )(h, w_in, cos, sin)


def _glu_kernel(h_ref, wa_ref, wb_ref, o_ref):
    h = h_ref[...]
    a = jnp.dot(h, wa_ref[...], preferred_element_type=F32)
    b = jnp.dot(h, wb_ref[...], preferred_element_type=F32)
    o_ref[...] = a * jax.nn.sigmoid(b)


def _glu_proj(h, w_in, conv_channels):
    s, d = h.shape
    tm = 1024
    a0 = 3 * ATTN_WIDTH // COL
    nb = conv_channels // COL
    return pl.pallas_call(
        _glu_kernel,
        out_shape=jax.ShapeDtypeStruct((s, conv_channels), F32),
        grid=(s // tm, nb),
        in_specs=[pl.BlockSpec((tm, d), lambda i, j: (i, 0)),
                  pl.BlockSpec((d, COL), lambda i, j: (0, a0 + j)),
                  pl.BlockSpec((d, COL), lambda i, j: (0, a0 + nb + j))],
        out_specs=pl.BlockSpec((tm, COL), lambda i, j: (i, j)),
        compiler_params=_params(("parallel", "arbitrary")),
        name="glu",
    )(h, w_in, w_in)


def _gates_kernel(h_ref, w_ref, o_ref):
    o_ref[...] = jax.nn.sigmoid(jnp.dot(h_ref[...], w_ref[...], preferred_element_type=F32))


def _gates_proj(h, w_in, col0, width):
    s, d = h.shape
    tm = 1024
    c0 = col0 // COL
    return pl.pallas_call(
        _gates_kernel,
        out_shape=jax.ShapeDtypeStruct((s, width), F32),
        grid=(s // tm, width // COL),
        in_specs=[pl.BlockSpec((tm, d), lambda i, j: (i, 0)),
                  pl.BlockSpec((d, COL), lambda i, j: (0, c0 + j))],
        out_specs=pl.BlockSpec((tm, COL), lambda i, j: (i, j)),
        compiler_params=_params(("parallel", "arbitrary")),
        name="gates",
    )(h, w_in)


ATTN_Q_BLOCKS = 4


def _attn_kernel(q_ref, kc_ref, kp_ref, vc_ref, vp_ref, o_ref, l_ref):
    n = pl.program_id(1)
    qi = lax.broadcasted_iota(jnp.int32, (ATTN_BLOCK, 2 * ATTN_BLOCK), 0)
    kj = lax.broadcasted_iota(jnp.int32, (ATTN_BLOCK, 2 * ATTN_BLOCK), 1)
    band = jnp.abs(kj - qi - ATTN_BLOCK // 2) <= ATTN_BLOCK // 2
    first_lo = jnp.where(n == 0, ATTN_BLOCK, 0)
    band_first = band & (kj >= first_lo)
    scale = HEAD_DIM ** -0.5
    for hh in range(HEADS_PER_GROUP):
        cols = slice(hh * HEAD_DIM, (hh + 1) * HEAD_DIM)
        for b in range(ATTN_Q_BLOCKS):
            rows = slice(b * ATTN_BLOCK, (b + 1) * ATTN_BLOCK)
            q = q_ref[rows, cols]
            if b == 0:
                k = jnp.concatenate([kp_ref[:, cols], kc_ref[0:ATTN_BLOCK, cols]], axis=0)
                v = jnp.concatenate([vp_ref[:, cols], vc_ref[0:ATTN_BLOCK, cols]], axis=0)
                valid = band_first
            else:
                band_rows = slice((b - 1) * ATTN_BLOCK, (b + 1) * ATTN_BLOCK)
                k = kc_ref[band_rows, cols]
                v = vc_ref[band_rows, cols]
                valid = band
            s = lax.dot_general(q, k, (((1,), (1,)), ((), ())), preferred_element_type=F32) * scale
            s = jnp.where(valid, s, MASK_VALUE)
            m = jnp.max(s, axis=-1, keepdims=True)
            p = jnp.exp(s - m)
            den = jnp.sum(p, axis=-1, keepdims=True)
            o = jnp.dot(p.astype(BF16), v, preferred_element_type=F32)
            o_ref[rows, cols] = o * (1.0 / den)
            l_ref[rows, cols] = jnp.broadcast_to(m + jnp.log(den), (ATTN_BLOCK, HEAD_DIM))


def _dilated_attention(qkv, group, dilation):
    s = qkv.shape[0]
    sub_len = s // dilation
    qt = ATTN_Q_BLOCKS * ATTN_BLOCK
    per_tok = 3 * ATTN_WIDTH // COL
    view = qkv.reshape(sub_len, dilation * 3 * ATTN_WIDTH)
    qc, kc, vc = group, N_GROUPS + group, 2 * N_GROUPS + group

    def cur(c):
        return pl.BlockSpec((qt, COL), lambda r, n: (n, r * per_tok + c))

    def prev(c):
        return pl.BlockSpec((ATTN_BLOCK, COL),
                            lambda r, n: (jnp.maximum(n * ATTN_Q_BLOCKS - 1, 0), r * per_tok + c))

    out_sds = jax.ShapeDtypeStruct((sub_len, dilation * GROUP_WIDTH), F32)
    out_spec = pl.BlockSpec((qt, GROUP_WIDTH), lambda r, n: (n, r))
    o, lse = pl.pallas_call(
        _attn_kernel,
        out_shape=(out_sds, out_sds),
        grid=(dilation, sub_len // qt),
        in_specs=[cur(qc), cur(kc), prev(kc), cur(vc), prev(vc)],
        out_specs=(out_spec, out_spec),
        compiler_params=_params(("parallel", "arbitrary")),
        name=f"attn_d{dilation}",
    )(view, view, view, view, view)
    return o.reshape(s, GROUP_WIDTH), lse.reshape(s, GROUP_WIDTH)


def _mix_kernel(o0_ref, o1_ref, o2_ref, l0_ref, l1_ref, l2_ref, u_ref, uh_ref, ga_ref, gc_ref,
                wao_ref, cw_ref, cb_ref, lg_ref, lb_ref, wco_ref, out_ref, ucat_ref):
    i = pl.program_id(0)
    tm = u_ref.shape[0]
    l0, l1, l2 = l0_ref[...], l1_ref[...], l2_ref[...]
    m = jnp.maximum(jnp.maximum(l0, l1), l2)
    e0, e1, e2 = jnp.exp(l0 - m), jnp.exp(l1 - m), jnp.exp(l2 - m)
    attn = (e0 * o0_ref[...] + e1 * o1_ref[...] + e2 * o2_ref[...]) / (e0 + e1 + e2)
    attn_o = jnp.dot(attn.astype(BF16), wao_ref[...], preferred_element_type=F32)
    ucat_ref[0:CONV_HALO, :] = jnp.where(i > 0, uh_ref[...], 0.0)
    ucat_ref[CONV_HALO:, :] = u_ref[...]
    off = CONV_HALO - (CONV_WIDTH - 1)
    conv = jnp.zeros(u_ref.shape, F32) + cb_ref[...]
    for w in range(CONV_WIDTH):
        conv = conv + ucat_ref[off + w:off + w + tm, :] * cw_ref[w:w + 1, :]
    mu = jnp.mean(conv, axis=-1, keepdims=True)
    cen = conv - mu
    var = jnp.mean(cen * cen, axis=-1, keepdims=True)
    y = cen * lax.rsqrt(var + NORM_EPS) * lg_ref[...] + lb_ref[...]
    y = y * jax.nn.sigmoid(y)
    conv_o = jnp.dot(y.astype(BF16), wco_ref[...], preferred_element_type=F32)
    out_ref[...] = (ga_ref[...] * attn_o + gc_ref[...] * conv_o).astype(out_ref.dtype)


def _mix(outs, lses, u, gates, w_attn_o, conv_w, conv_b, ln_g, ln_b, w_conv_o):
    s, cc = u.shape
    d = w_attn_o.shape[1]
    tm = 256
    row = lambda i: (i, 0)
    const = lambda i: (0, 0)
    grp = pl.BlockSpec((tm, GROUP_WIDTH), row)
    cvec = pl.BlockSpec((1, cc), const)
    return pl.pallas_call(
        _mix_kernel,
        out_shape=jax.ShapeDtypeStruct((s, d), BF16),
        grid=(s // tm,),
        in_specs=[grp] * 6 + [
            pl.BlockSpec((tm, cc), row),
            pl.BlockSpec((CONV_HALO, cc), lambda i: (jnp.maximum(i * (tm // CONV_HALO) - 1, 0), 0)),
            pl.BlockSpec((tm, d), lambda i: (i, 0)),
            pl.BlockSpec((tm, d), lambda i: (i, 1)),
            pl.BlockSpec((GROUP_WIDTH, d), const),
            pl.BlockSpec((CONV_WIDTH, cc), const),
            cvec, cvec, cvec,
            pl.BlockSpec((cc, d), const)],
        out_specs=pl.BlockSpec((tm, d), row),
        scratch_shapes=[pltpu.VMEM((tm + CONV_HALO, cc), F32)],
        compiler_params=_params(("parallel",)),
        name="mix",
    )(*outs, *lses, u, u, gates, gates, w_attn_o, conv_w, conv_b, ln_g, ln_b, w_conv_o)


def _out_proj_kernel(m_ref, w_ref, x_ref, gate_ref, g1_ref, g2_ref, scale_ref, shift_ref,
                     x1_ref, h2t_ref):
    y = jnp.dot(m_ref[...], w_ref[...], preferred_element_type=F32)
    x1 = x_ref[...] + gate_ref[...] * (_rms(y) * g1_ref[...])
    x1_ref[...] = x1
    h2 = (_rms(x1) * g2_ref[...]) * (1.0 + scale_ref[...]) + shift_ref[...]
    h2t_ref[...] = h2.T.astype(h2t_ref.dtype)


def _out_proj(merged, w_out, x, gate1, g1, g2, scale2, shift2):
    s, d = x.shape
    tm = 256
    row = lambda i: (i, 0)
    vec = pl.BlockSpec((1, d), lambda i: (0, 0))
    return pl.pallas_call(
        _out_proj_kernel,
        out_shape=(jax.ShapeDtypeStruct((s, d), F32), jax.ShapeDtypeStruct((d, s), BF16)),
        grid=(s // tm,),
        in_specs=[pl.BlockSpec((tm, d), row), pl.BlockSpec((d, d), lambda i: (0, 0)),
                  pl.BlockSpec((tm, d), row), vec, vec, vec, vec, vec],
        out_specs=(pl.BlockSpec((tm, d), row), pl.BlockSpec((d, tm), lambda i: (0, i))),
        compiler_params=_params(("parallel",)),
        name="out_proj",
    )(merged, w_out, x, gate1, g1, g2, scale2, shift2)


def _top_values(work, count):
    row = lax.broadcasted_iota(jnp.int32, work.shape, 0)
    vals = []
    for _ in range(count):
        m = jnp.max(work, axis=0, keepdims=True)
        vals.append(m)
        first = jnp.min(jnp.where(work == m, row, work.shape[0]), axis=0, keepdims=True)
        work = jnp.where(row == first, NEG_INF, work)
    return vals


def _peer_query_kernel(h2t_ref, wqt_ref, keys_ref, e1_ref, e2_ref, theta_ref,
                       qt_ref, top_ref, cand_ref):
    qt_ref[...] = jnp.dot(wqt_ref[...], h2t_ref[...], preferred_element_type=F32)
    for h in range(PEER_HEADS):
        u = []
        for side in range(2):
            r0 = (2 * h + side) * PEER_HALF
            qc = qt_ref[r0:r0 + PEER_HALF, :].astype(BF16)
            sc = jnp.dot(keys_ref[h, side].astype(BF16), qc, preferred_element_type=F32)
            us = jnp.exp(sc - jnp.max(sc, axis=0, keepdims=True))
            vals = _top_values(us, PEER_TOPK)
            for k in range(PEER_TOPK):
                top_ref[side, k:k + 1, :] = vals[k]
            u.append(us)
        cand_ref[...] = jnp.full(cand_ref.shape, -1.0, F32)
        off = 0
        for a, nb in enumerate(_CAND_ROWS):
            cand_ref[off:off + nb, :] = top_ref[0, a:a + 1, :] * top_ref[1, 0:nb, :]
            off += nb
        cand = cand_ref[...]
        selected = cand >= _top_values(cand, PEER_TOPK)[-1]
        inv_z = 1.0 / jnp.sum(jnp.where(selected, cand, 0.0), axis=0, keepdims=True)
        off = 0
        for a, nb in enumerate(_CAND_ROWS):
            cand_ref[off:off + nb, :] = (top_ref[0, a:a + 1, :] * inv_z) * top_ref[1, 0:nb, :]
            off += nb
        theta = jnp.min(jnp.where(selected, cand_ref[...], jnp.inf), axis=0, keepdims=True)
        e1 = u[0] * inv_z
        for blk in range(PEER_N_KEYS // 8):
            e1_ref[h, blk] = e1[blk * 8:(blk + 1) * 8, :]
        e2_ref[h] = u[1]
        theta_ref[h:h + 1, :] = theta


def _peer_query(h2t, wq_t, sub_keys):
    d, s = h2t.shape
    tt = 512
    nq = wq_t.shape[0]
    side_blocked = jax.ShapeDtypeStruct((PEER_HEADS, PEER_N_KEYS // 8, 8, s), F32)
    side_flat = jax.ShapeDtypeStruct((PEER_HEADS, PEER_N_KEYS, s), F32)
    blocked_spec = pl.BlockSpec((PEER_HEADS, PEER_N_KEYS // 8, 8, tt), lambda t: (0, 0, 0, t))
    flat_spec = pl.BlockSpec((PEER_HEADS, PEER_N_KEYS, tt), lambda t: (0, 0, t))
    return pl.pallas_call(
        _peer_query_kernel,
        out_shape=(side_blocked, side_flat, jax.ShapeDtypeStruct((PEER_HEADS, s), F32)),
        grid=(s // tt,),
        in_specs=[pl.BlockSpec((d, tt), lambda t: (0, t)),
                  pl.BlockSpec((nq, d), lambda t: (0, 0)),
                  pl.BlockSpec(sub_keys.shape, lambda t: (0, 0, 0, 0))],
        out_specs=(blocked_spec, flat_spec, pl.BlockSpec((PEER_HEADS, tt), lambda t: (0, t))),
        scratch_shapes=[pltpu.VMEM((nq, tt), F32),
                        pltpu.VMEM((2, PEER_TOPK, tt), F32),
                        pltpu.VMEM((_CAND_PAD, tt), F32)],
        compiler_params=_params(("parallel",)),
        name="peer_query",
    )(h2t, wq_t, sub_keys)


PEER_LANE_CHUNK = 128


PEER_HALF_KEYS = 4
PEER_HALF_TILE = PEER_HALF_KEYS * PEER_N_KEYS


def _gate_act(pre_ref, act_ref, e1_ref, e2_ref, theta_ref, half):
    tt = pre_ref.shape[2]
    for ii in range(PEER_HALF_KEYS):
        rows = slice(ii * PEER_N_KEYS, (ii + 1) * PEER_N_KEYS)
        key = half * PEER_HALF_KEYS + ii
        for c in range(tt // PEER_LANE_CHUNK):
            lanes = slice(c * PEER_LANE_CHUNK, (c + 1) * PEER_LANE_CHUNK)
            w = jnp.zeros((PEER_N_KEYS, PEER_LANE_CHUNK), F32)
            for h in range(PEER_HEADS):
                p = e1_ref[h, 0, key:key + 1, lanes] * e2_ref[h, :, lanes]
                w = w + jnp.where(p >= theta_ref[h:h + 1, lanes], p, 0.0)
            pre = pre_ref[half, rows, lanes]
            act = (0.5 * pre) * (1.0 + lax.erf(pre * INV_SQRT2)) * w
            act_ref[half, rows, lanes] = act.astype(act_ref.dtype)


def _peer_dense_kernel(h2t_ref, down_ref, up_ref, e1_ref, e2_ref, theta_ref,
                       x1_ref, gate_ref, g_ref, out_ref, pre_ref, act_ref):
    e = pl.program_id(1)
    last = pl.num_programs(1) - 1

    @pl.when(e == 0)
    def _():
        out_ref[...] = jnp.zeros(out_ref.shape, F32)
        pre_ref[...] = jnp.zeros(pre_ref.shape, F32)

    for half in range(2):
        rows = slice(half * PEER_HALF_TILE, (half + 1) * PEER_HALF_TILE)
        _gate_act(pre_ref, act_ref, e1_ref, e2_ref, theta_ref, half)
        out_ref[...] += lax.dot_general(act_ref[half], up_ref[rows, :], (((0,), (0,)), ((), ())),
                                        preferred_element_type=F32)
        pre_ref[half] = jnp.dot(down_ref[rows, :], h2t_ref[...], preferred_element_type=F32)

    @pl.when(e == last)
    def _():
        y = out_ref[...]
        out_ref[...] = x1_ref[...] + gate_ref[...] * (_rms(y) * g_ref[...])


def _peer_dense(h2t, down, up, e1, e2, theta, x1, gate2, g3):
    d, s = h2t.shape
    n_exp = down.shape[0]
    tt = 512
    et = 2 * PEER_HALF_TILE
    n_e = n_exp // et
    vec = pl.BlockSpec((1, d), lambda t, e: (0, 0))
    cur = lambda e: jnp.minimum(e, n_e - 1)
    prev = lambda e: jnp.maximum(e - 1, 0)
    return pl.pallas_call(
        _peer_dense_kernel,
        out_shape=jax.ShapeDtypeStruct((s, d), F32),
        grid=(s // tt, n_e + 1),
        in_specs=[pl.BlockSpec((d, tt), lambda t, e: (0, t)),
                  pl.BlockSpec((et, d), lambda t, e: (cur(e), 0)),
                  pl.BlockSpec((et, d), lambda t, e: (prev(e), 0)),
                  pl.BlockSpec((PEER_HEADS, 1, 8, tt), lambda t, e: (0, prev(e), 0, t)),
                  pl.BlockSpec((PEER_HEADS, PEER_N_KEYS, tt), lambda t, e: (0, 0, t)),
                  pl.BlockSpec((PEER_HEADS, tt), lambda t, e: (0, t)),
                  pl.BlockSpec((tt, d), lambda t, e: (t, 0)),
                  vec, vec],
        out_specs=pl.BlockSpec((tt, d), lambda t, e: (t, 0)),
        scratch_shapes=[pltpu.VMEM((2, PEER_HALF_TILE, tt), F32),
                        pltpu.VMEM((2, PEER_HALF_TILE, tt), BF16)],
        compiler_params=_params(("parallel", "arbitrary")),
        name="peer_dense",
    )(h2t, down, up, e1, e2, theta, x1, gate2, g3)


def kernel(x, c, positions, ada_w, ada_b, norm_gains, w_in, w_attn_o, conv_w, conv_b, conv_ln_g,
           conv_ln_b, w_conv_o, w_out, peer_w_q, peer_sub_keys, peer_down, peer_up):
    batch, seq, d = x.shape
    depth = ada_w.shape[0]
    assert batch == 1, "kernels are written for a single sequence"
    cc = conv_w.shape[-1]
    xs = x[0]
    inv_freq = ROPE_THETA ** (-jnp.arange(0, HEAD_DIM, 2, dtype=F32) / HEAD_DIM)
    invf_row = jnp.concatenate([inv_freq, inv_freq])[None, :]
    cos, sin = _rope_tables(positions[0].astype(F32)[:, None], invf_row)
    row = lambda v: v[None, :]
    for l in range(depth):
        mod = _modulation(c[0][:, None], ada_w[l], ada_b[l][None, :])
        shift1, scale1, gate1, shift2, scale2, gate2 = [mod[:, k * d:(k + 1) * d] for k in range(6)]
        g = norm_gains[l]
        w_in_b = w_in[l].astype(BF16)

        h = _norm_mod(xs, row(g[0]), scale1, shift1)
        qkv = _qkv_proj(h, w_in_b, cos, sin)
        u = _glu_proj(h, w_in_b, cc)
        gates = _gates_proj(h, w_in_b, 3 * ATTN_WIDTH + 2 * cc, 2 * d)
        outs, lses = [], []
        for gi, (_, dilation) in enumerate(DILATED_GROUPS):
            o, lse = _dilated_attention(qkv, gi, dilation)
            outs.append(o)
            lses.append(lse)
        merged = _mix(outs, lses, u, gates, w_attn_o[l].astype(BF16), conv_w[l][:, 0, :],
                      row(conv_b[l]), row(conv_ln_g[l]), row(conv_ln_b[l]), w_conv_o[l].astype(BF16))
        x1, h2t = _out_proj(merged, w_out[l].astype(BF16), xs, gate1, row(g[1]), row(g[2]),
                            scale2, shift2)

        e1, e2, theta = _peer_query(h2t, peer_w_q[l].T.astype(BF16), peer_sub_keys[l])
        xs = _peer_dense(h2t, peer_down[l].astype(BF16), peer_up[l].astype(BF16),
                         e1, e2, theta, x1, gate2, row(g[3]))
    return xs[None]
```

```python
import functools
import math

import jax
import jax.numpy as jnp
from jax import lax
from jax.experimental import pallas as pl
from jax.experimental.pallas import tpu as pltpu

F32 = jnp.float32
BF16 = jnp.bfloat16

HEAD_DIM = 128
HEADS_PER_GROUP = 4
DILATED_GROUPS = ((128, 1), (512, 4), (2048, 16))
N_GROUPS = len(DILATED_GROUPS)
GROUP_WIDTH = HEADS_PER_GROUP * HEAD_DIM
ATTN_WIDTH = N_GROUPS * GROUP_WIDTH
ATTN_BLOCK = 128
ROPE_THETA = 10000.0
CONV_WIDTH = 31
CONV_HALO = 32
PEER_HEADS = 8
PEER_N_KEYS = 128
PEER_HALF = 128
PEER_TOPK = 16
NORM_EPS = 1e-6
MASK_VALUE = -1e30
NEG_INF = float("-inf")
INV_SQRT2 = 1.0 / math.sqrt(2.0)

COL = 512
VMEM_LIMIT = 56 * 1024 * 1024

_CAND_ROWS = [PEER_TOPK // (a + 1) for a in range(PEER_TOPK)]
_N_CAND = sum(_CAND_ROWS)
_CAND_PAD = -(-_N_CAND // 8) * 8


def _params(sem):
    return pltpu.CompilerParams(dimension_semantics=sem, vmem_limit_bytes=VMEM_LIMIT)


def _rms(x):
    return x * lax.rsqrt(jnp.mean(x * x, axis=-1, keepdims=True) + NORM_EPS)


def _mod_kernel(c_ref, w_ref, b_ref, o_ref):
    rows = 256

    def body(k, acc):
        r = pl.multiple_of(k * rows, rows)
        c = c_ref[pl.ds(r, rows), :]
        sc = c * jax.nn.sigmoid(c)
        return acc + jnp.sum(w_ref[pl.ds(r, rows), :] * sc, axis=0, keepdims=True)

    acc = lax.fori_loop(0, w_ref.shape[0] // rows, body, jnp.zeros(o_ref.shape, F32))
    o_ref[...] = acc + b_ref[...]


def _modulation(c_col, w, b):
    d, n = w.shape
    tn = 512
    return pl.pallas_call(
        _mod_kernel,
        out_shape=jax.ShapeDtypeStruct((1, n), F32),
        grid=(n // tn,),
        in_specs=[pl.BlockSpec((d, 1), lambda j: (0, 0)),
                  pl.BlockSpec((d, tn), lambda j: (0, j)),
                  pl.BlockSpec((1, tn), lambda j: (0, j))],
        out_specs=pl.BlockSpec((1, tn), lambda j: (0, j)),
        compiler_params=_params(("parallel",)),
        name="mod",
    )(c_col, w, b)


def _rope_kernel(pos_ref, invf_ref, cos_ref, sin_ref):
    ang = pos_ref[...] * invf_ref[...]
    cos_ref[...] = jnp.cos(ang)
    s = jnp.sin(ang)
    lane = lax.broadcasted_iota(jnp.int32, s.shape, 1)
    sin_ref[...] = jnp.where(lane < HEAD_DIM // 2, -s, s)


def _rope_tables(pos_col, invf_row):
    s = pos_col.shape[0]
    ts = 2048
    return pl.pallas_call(
        _rope_kernel,
        out_shape=(jax.ShapeDtypeStruct((s, HEAD_DIM), F32),) * 2,
        grid=(s // ts,),
        in_specs=[pl.BlockSpec((ts, 1), lambda i: (i, 0)),
                  pl.BlockSpec((1, HEAD_DIM), lambda i: (0, 0))],
        out_specs=(pl.BlockSpec((ts, HEAD_DIM), lambda i: (i, 0)),) * 2,
        compiler_params=_params(("parallel",)),
        name="rope",
    )(pos_col, invf_row)


def _norm_mod_kernel(x_ref, g_ref, scale_ref, shift_ref, o_ref):
    y = _rms(x_ref[...]) * g_ref[...]
    o_ref[...] = (y * (1.0 + scale_ref[...]) + shift_ref[...]).astype(o_ref.dtype)


def _norm_mod(x, g, scale, shift):
    s, d = x.shape
    tm = 512
    vec = pl.BlockSpec((1, d), lambda i: (0, 0))
    return pl.pallas_call(
        _norm_mod_kernel,
        out_shape=jax.ShapeDtypeStruct((s, d), BF16),
        grid=(s // tm,),
        in_specs=[pl.BlockSpec((tm, d), lambda i: (i, 0)), vec, vec, vec],
        out_specs=pl.BlockSpec((tm, d), lambda i: (i, 0)),
        compiler_params=_params(("parallel",)),
        name="norm_mod",
    )(x, g, scale, shift)


def _qkv_kernel(h_ref, w_ref, cos_ref, sin_ref, o_ref, stage_ref):
    acc = jnp.dot(h_ref[...], w_ref[...], preferred_element_type=F32)
    part = pl.program_id(1)

    heads = [slice(hh * HEAD_DIM, (hh + 1) * HEAD_DIM) for hh in range(HEADS_PER_GROUP)]

    @pl.when(part < 2)
    def _():
        cos = cos_ref[...]
        sin = sin_ref[...]
        for hh, cols in enumerate(heads):
            t = acc[:, cols]
            stage_ref[hh] = t * cos + pltpu.roll(t, HEAD_DIM // 2, axis=1) * sin

    @pl.when(part == 2)
    def _():
        for hh, cols in enumerate(heads):
            stage_ref[hh] = acc[:, cols]

    dilation, rows, _ = o_ref.shape
    for r in range(dilation):
        for hh, cols in enumerate(heads):
            o_ref[r, :, cols] = stage_ref[hh, pl.ds(r, rows, stride=dilation), :].astype(o_ref.dtype)


def _qkv_proj(h, w_in, cos, sin, group, dilation):
    s, d = h.shape
    tm = 1024
    return pl.pallas_call(
        _qkv_kernel,
        out_shape=jax.ShapeDtypeStruct((dilation, s // dilation, 3 * GROUP_WIDTH), BF16),
        grid=(s // tm, 3),
        in_specs=[pl.BlockSpec((tm, d), lambda i, j: (i, 0)),
                  pl.BlockSpec((d, COL), lambda i, j: (0, j * N_GROUPS + group)),
                  pl.BlockSpec((tm, HEAD_DIM), lambda i, j: (i, 0)),
                  pl.BlockSpec((tm, HEAD_DIM), lambda i, j: (i, 0))],
        out_specs=pl.BlockSpec((dilation, tm // dilation, COL), lambda i, j: (0, i, j)),
        scratch_shapes=[pltpu.VMEM((HEADS_PER_GROUP, tm, HEAD_DIM), F32)],
        compiler_params=_params(("parallel", "arbitrary")),
        name=f"qkv_d{dilation}",
    )(h, w_in, cos, sin)


def _glu_kernel(h_ref, wa_ref, wb_ref, o_ref):
    h = h_ref[...]
    a = jnp.dot(h, wa_ref[...], preferred_element_type=F32)
    b = jnp.dot(h, wb_ref[...], preferred_element_type=F32)
    o_ref[...] = a * jax.nn.sigmoid(b)


def _glu_proj(h, w_in, conv_channels):
    s, d = h.shape
    tm = 1024
    a0 = 3 * ATTN_WIDTH // COL
    nb = conv_channels // COL
    return pl.pallas_call(
        _glu_kernel,
        out_shape=jax.ShapeDtypeStruct((s, conv_channels), F32),
        grid=(s // tm, nb),
        in_specs=[pl.BlockSpec((tm, d), lambda i, j: (i, 0)),
                  pl.BlockSpec((d, COL), lambda i, j: (0, a0 + j)),
                  pl.BlockSpec((d, COL), lambda i, j: (0, a0 + nb + j))],
        out_specs=pl.BlockSpec((tm, COL), lambda i, j: (i, j)),
        compiler_params=_params(("parallel", "arbitrary")),
        name="glu",
    )(h, w_in, w_in)


def _gates_kernel(h_ref, w_ref, o_ref):
    o_ref[...] = jax.nn.sigmoid(jnp.dot(h_ref[...], w_ref[...], preferred_element_type=F32))


def _gates_proj(h, w_in, col0, width):
    s, d = h.shape
    tm = 1024
    c0 = col0 // COL
    return pl.pallas_call(
        _gates_kernel,
        out_shape=jax.ShapeDtypeStruct((s, width), F32),
        grid=(s // tm, width // COL),
        in_specs=[pl.BlockSpec((tm, d), lambda i, j: (i, 0)),
                  pl.BlockSpec((d, COL), lambda i, j: (0, c0 + j))],
        out_specs=pl.BlockSpec((tm, COL), lambda i, j: (i, j)),
        compiler_params=_params(("parallel", "arbitrary")),
        name="gates",
    )(h, w_in)


ATTN_Q_BLOCKS = 4


def _attn_kernel(q_ref, kc_ref, kp_ref, vc_ref, vp_ref, o_ref, l_ref):
    n = pl.program_id(1)
    qi = lax.broadcasted_iota(jnp.int32, (ATTN_BLOCK, 2 * ATTN_BLOCK), 0)
    kj = lax.broadcasted_iota(jnp.int32, (ATTN_BLOCK, 2 * ATTN_BLOCK), 1)
    band = jnp.abs(kj - qi - ATTN_BLOCK // 2) <= ATTN_BLOCK // 2
    first_lo = jnp.where(n == 0, ATTN_BLOCK, 0)
    band_first = band & (kj >= first_lo)
    scale = HEAD_DIM ** -0.5
    for hh in range(HEADS_PER_GROUP):
        cols = slice(hh * HEAD_DIM, (hh + 1) * HEAD_DIM)
        for b in range(ATTN_Q_BLOCKS):
            rows = slice(b * ATTN_BLOCK, (b + 1) * ATTN_BLOCK)
            q = q_ref[rows, cols]
            if b == 0:
                k = jnp.concatenate([kp_ref[:, cols], kc_ref[0:ATTN_BLOCK, cols]], axis=0)
                v = jnp.concatenate([vp_ref[:, cols], vc_ref[0:ATTN_BLOCK, cols]], axis=0)
                valid = band_first
            else:
                band_rows = slice((b - 1) * ATTN_BLOCK, (b + 1) * ATTN_BLOCK)
                k = kc_ref[band_rows, cols]
                v = vc_ref[band_rows, cols]
                valid = band
            s = lax.dot_general(q, k, (((1,), (1,)), ((), ())), preferred_element_type=F32) * scale
            s = jnp.where(valid, s, MASK_VALUE)
            m = jnp.max(s, axis=-1, keepdims=True)
            p = jnp.exp(s - m)
            den = jnp.sum(p, axis=-1, keepdims=True)
            o = jnp.dot(p.astype(BF16), v, preferred_element_type=F32)
            o_ref[rows, cols] = o * (1.0 / den)
            l_ref[rows, cols] = jnp.broadcast_to(m + jnp.log(den), (ATTN_BLOCK, HEAD_DIM))


def _dilated_attention(qkv):
    dilation, sub_len, _ = qkv.shape
    qt = ATTN_Q_BLOCKS * ATTN_BLOCK

    def cur(part):
        return pl.BlockSpec((None, qt, COL), lambda r, n: (r, n, part))

    def prev(part):
        return pl.BlockSpec((None, ATTN_BLOCK, COL),
                            lambda r, n: (r, jnp.maximum(n * ATTN_Q_BLOCKS - 1, 0), part))

    out_sds = jax.ShapeDtypeStruct((dilation, sub_len, GROUP_WIDTH), F32)
    out_spec = pl.BlockSpec((None, qt, GROUP_WIDTH), lambda r, n: (r, n, 0))
    return pl.pallas_call(
        _attn_kernel,
        out_shape=(out_sds, out_sds),
        grid=(dilation, sub_len // qt),
        in_specs=[cur(0), cur(1), prev(1), cur(2), prev(2)],
        out_specs=(out_spec, out_spec),
        compiler_params=_params(("parallel", "arbitrary")),
        name=f"attn_d{dilation}",
    )(qkv, qkv, qkv, qkv, qkv)


def _mix_kernel(o0_ref, o1_ref, o2_ref, l0_ref, l1_ref, l2_ref, u_ref, uh_ref, ga_ref, gc_ref,
                wao_ref, cw_ref, cb_ref, lg_ref, lb_ref, wco_ref, out_ref, ucat_ref, nat_ref):
    i = pl.program_id(0)
    tm = u_ref.shape[0]

    def natural(ref, slot):
        dilation, rows, _ = ref.shape
        if dilation == 1:
            return ref[0]
        for hh in range(HEADS_PER_GROUP):
            for r in range(dilation):
                nat_ref[slot, hh, pl.ds(r, rows, stride=dilation), :] = ref[r, :, hh * HEAD_DIM:(hh + 1) * HEAD_DIM]
        return jnp.concatenate([nat_ref[slot, hh] for hh in range(HEADS_PER_GROUP)], axis=1)

    l0, l1, l2 = natural(l0_ref, 0), natural(l1_ref, 1), natural(l2_ref, 2)
    m = jnp.maximum(jnp.maximum(l0, l1), l2)
    e0, e1, e2 = jnp.exp(l0 - m), jnp.exp(l1 - m), jnp.exp(l2 - m)
    attn = (e0 * natural(o0_ref, 3) + e1 * natural(o1_ref, 4) + e2 * natural(o2_ref, 5)) / (e0 + e1 + e2)
    attn_o = jnp.dot(attn.astype(BF16), wao_ref[...], preferred_element_type=F32)
    ucat_ref[0:CONV_HALO, :] = jnp.where(i > 0, uh_ref[...], 0.0)
    ucat_ref[CONV_HALO:, :] = u_ref[...]
    off = CONV_HALO - (CONV_WIDTH - 1)
    conv = jnp.zeros(u_ref.shape, F32) + cb_ref[...]
    for w in range(CONV_WIDTH):
        conv = conv + ucat_ref[off + w:off + w + tm, :] * cw_ref[w:w + 1, :]
    mu = jnp.mean(conv, axis=-1, keepdims=True)
    cen = conv - mu
    var = jnp.mean(cen * cen, axis=-1, keepdims=True)
    y = cen * lax.rsqrt(var + NORM_EPS) * lg_ref[...] + lb_ref[...]
    y = y * jax.nn.sigmoid(y)
    conv_o = jnp.dot(y.astype(BF16), wco_ref[...], preferred_element_type=F32)
    out_ref[...] = (ga_ref[...] * attn_o + gc_ref[...] * conv_o).astype(out_ref.dtype)


def _mix(outs, lses, u, gates, w_attn_o, conv_w, conv_b, ln_g, ln_b, w_conv_o):
    s, cc = u.shape
    d = w_attn_o.shape[1]
    tm = 256
    row = lambda i: (i, 0)
    const = lambda i: (0, 0)
    cvec = pl.BlockSpec((1, cc), const)

    def grp(arr):
        dilation = arr.shape[0]
        return pl.BlockSpec((dilation, tm // dilation, GROUP_WIDTH), lambda i: (0, i, 0))

    return pl.pallas_call(
        _mix_kernel,
        out_shape=jax.ShapeDtypeStruct((s, d), BF16),
        grid=(s // tm,),
        in_specs=[grp(a) for a in (*outs, *lses)] + [
            pl.BlockSpec((tm, cc), row),
            pl.BlockSpec((CONV_HALO, cc), lambda i: (jnp.maximum(i * (tm // CONV_HALO) - 1, 0), 0)),
            pl.BlockSpec((tm, d), lambda i: (i, 0)),
            pl.BlockSpec((tm, d), lambda i: (i, 1)),
            pl.BlockSpec((GROUP_WIDTH, d), const),
            pl.BlockSpec((CONV_WIDTH, cc), const),
            cvec, cvec, cvec,
            pl.BlockSpec((cc, d), const)],
        out_specs=pl.BlockSpec((tm, d), row),
        scratch_shapes=[pltpu.VMEM((tm + CONV_HALO, cc), F32),
                        pltpu.VMEM((2 * N_GROUPS, HEADS_PER_GROUP, tm, HEAD_DIM), F32)],
        compiler_params=_params(("parallel",)),
        name="mix",
    )(*outs, *lses, u, u, gates, gates, w_attn_o, conv_w, conv_b, ln_g, ln_b, w_conv_o)


def _out_proj_kernel(m_ref, w_ref, x_ref, gate_ref, g1_ref, g2_ref, scale_ref, shift_ref,
                     x1_ref, h2t_ref):
    y = jnp.dot(m_ref[...], w_ref[...], preferred_element_type=F32)
    x1 = x_ref[...] + gate_ref[...] * (_rms(y) * g1_ref[...])
    x1_ref[...] = x1
    h2 = (_rms(x1) * g2_ref[...]) * (1.0 + scale_ref[...]) + shift_ref[...]
    h2t_ref[...] = h2.T.astype(h2t_ref.dtype)


def _out_proj(merged, w_out, x, gate1, g1, g2, scale2, shift2):
    s, d = x.shape
    tm = 256
    row = lambda i: (i, 0)
    vec = pl.BlockSpec((1, d), lambda i: (0, 0))
    return pl.pallas_call(
        _out_proj_kernel,
        out_shape=(jax.ShapeDtypeStruct((s, d), F32), jax.ShapeDtypeStruct((d, s), BF16)),
        grid=(s // tm,),
        in_specs=[pl.BlockSpec((tm, d), row), pl.BlockSpec((d, d), lambda i: (0, 0)),
                  pl.BlockSpec((tm, d), row), vec, vec, vec, vec, vec],
        out_specs=(pl.BlockSpec((tm, d), row), pl.BlockSpec((d, tm), lambda i: (0, i))),
        compiler_params=_params(("parallel",)),
        name="out_proj",
    )(merged, w_out, x, gate1, g1, g2, scale2, shift2)


def _top_values(work, count):
    row = lax.broadcasted_iota(jnp.int32, work.shape, 0)
    vals = []
    for _ in range(count):
        m = jnp.max(work, axis=0, keepdims=True)
        vals.append(m)
        first = jnp.min(jnp.where(work == m, row, work.shape[0]), axis=0, keepdims=True)
        work = jnp.where(row == first, NEG_INF, work)
    return vals


def _odd_even_merge_sort(lo, hi):
    def merge(lo, hi, r):
        step = 2 * r
        if step < hi - lo:
            yield from merge(lo, hi, step)
            yield from merge(lo + r, hi, step)
            for i in range(lo + r, hi - r, step):
                yield (i, i + r)
        else:
            yield (lo, lo + r)

    if hi > lo:
        mid = lo + (hi - lo) // 2
        yield from _odd_even_merge_sort(lo, mid)
        yield from _odd_even_merge_sort(mid + 1, hi)
        yield from merge(lo, hi, 1)


def _bitonic_merge(n):
    half = n // 2
    while half >= 1:
        for i in range(n):
            if (i // half) % 2 == 0:
                yield (i, i + half)
        half //= 2


_SORT_PAIRS = tuple(_odd_even_merge_sort(0, PEER_TOPK - 1))
_MERGE_PAIRS = tuple(_bitonic_merge(PEER_TOPK))
SUBLANES = 8


def _compare_exchange(v, pairs):
    for i, j in pairs:
        v[i], v[j] = jnp.maximum(v[i], v[j]), jnp.minimum(v[i], v[j])


def _top16_of_keys(scores):
    assert scores.shape[0] == PEER_TOPK * SUBLANES
    v = [scores[k * SUBLANES:(k + 1) * SUBLANES, :] for k in range(PEER_TOPK)]
    _compare_exchange(v, _SORT_PAIRS)
    shift = SUBLANES // 2
    while shift >= 1:
        partner = [pltpu.roll(x, shift, axis=0) for x in v]
        v = [jnp.maximum(v[k], partner[PEER_TOPK - 1 - k]) for k in range(PEER_TOPK)]
        _compare_exchange(v, _MERGE_PAIRS)
        shift //= 2
    return [x[0:1, :] for x in v]


def _peer_query_kernel(h2t_ref, wqt_ref, keys_ref, e1_ref, e2_ref, theta_ref,
                       qt_ref, top_ref, cand_ref):
    qt_ref[...] = jnp.dot(wqt_ref[...], h2t_ref[...], preferred_element_type=F32)
    for h in range(PEER_HEADS):
        u = []
        for side in range(2):
            r0 = (2 * h + side) * PEER_HALF
            qc = qt_ref[r0:r0 + PEER_HALF, :].astype(BF16)
            sc = jnp.dot(keys_ref[h, side].astype(BF16), qc, preferred_element_type=F32)
            us = jnp.exp(sc - jnp.max(sc, axis=0, keepdims=True))
            vals = _top16_of_keys(us)
            for k in range(PEER_TOPK):
                top_ref[side, k:k + 1, :] = vals[k]
            u.append(us)
        cand_ref[...] = jnp.full(cand_ref.shape, -1.0, F32)
        off = 0
        for a, nb in enumerate(_CAND_ROWS):
            cand_ref[off:off + nb, :] = top_ref[0, a:a + 1, :] * top_ref[1, 0:nb, :]
            off += nb
        cand = cand_ref[...]
        selected = cand >= _top_values(cand, PEER_TOPK)[-1]
        inv_z = 1.0 / jnp.sum(jnp.where(selected, cand, 0.0), axis=0, keepdims=True)
        off = 0
        for a, nb in enumerate(_CAND_ROWS):
            cand_ref[off:off + nb, :] = (top_ref[0, a:a + 1, :] * inv_z) * top_ref[1, 0:nb, :]
            off += nb
        theta = jnp.min(jnp.where(selected, cand_ref[...], jnp.inf), axis=0, keepdims=True)
        e1 = u[0] * inv_z
        for blk in range(PEER_N_KEYS // 8):
            e1_ref[h, blk] = e1[blk * 8:(blk + 1) * 8, :]
        e2_ref[h] = u[1]
        theta_ref[h:h + 1, :] = theta


def _peer_query(h2t, wq_t, sub_keys):
    d, s = h2t.shape
    tt = 512
    nq = wq_t.shape[0]
    side_blocked = jax.ShapeDtypeStruct((PEER_HEADS, PEER_N_KEYS // 8, 8, s), F32)
    side_flat = jax.ShapeDtypeStruct((PEER_HEADS, PEER_N_KEYS, s), F32)
    blocked_spec = pl.BlockSpec((PEER_HEADS, PEER_N_KEYS // 8, 8, tt), lambda t: (0, 0, 0, t))
    flat_spec = pl.BlockSpec((PEER_HEADS, PEER_N_KEYS, tt), lambda t: (0, 0, t))
    return pl.pallas_call(
        _peer_query_kernel,
        out_shape=(side_blocked, side_flat, jax.ShapeDtypeStruct((PEER_HEADS, s), F32)),
        grid=(s // tt,),
        in_specs=[pl.BlockSpec((d, tt), lambda t: (0, t)),
                  pl.BlockSpec((nq, d), lambda t: (0, 0)),
                  pl.BlockSpec(sub_keys.shape, lambda t: (0, 0, 0, 0))],
        out_specs=(blocked_spec, flat_spec, pl.BlockSpec((PEER_HEADS, tt), lambda t: (0, t))),
        scratch_shapes=[pltpu.VMEM((nq, tt), F32),
                        pltpu.VMEM((2, PEER_TOPK, tt), F32),
                        pltpu.VMEM((_CAND_PAD, tt), F32)],
        compiler_params=_params(("parallel",)),
        name="peer_query",
    )(h2t, wq_t, sub_keys)


PEER_LANE_CHUNK = 128


PEER_HALF_KEYS = 4
PEER_HALF_TILE = PEER_HALF_KEYS * PEER_N_KEYS


def _gate_act(pre_ref, act_ref, e1_ref, e2_ref, theta_ref, half):
    tt = pre_ref.shape[2]
    for ii in range(PEER_HALF_KEYS):
        rows = slice(ii * PEER_N_KEYS, (ii + 1) * PEER_N_KEYS)
        key = half * PEER_HALF_KEYS + ii
        for c in range(tt // PEER_LANE_CHUNK):
            lanes = slice(c * PEER_LANE_CHUNK, (c + 1) * PEER_LANE_CHUNK)
            w = jnp.zeros((PEER_N_KEYS, PEER_LANE_CHUNK), F32)
            for h in range(PEER_HEADS):
                p = e1_ref[h, 0, key:key + 1, lanes] * e2_ref[h, :, lanes]
                w = w + jnp.where(p >= theta_ref[h:h + 1, lanes], p, 0.0)
            pre = pre_ref[half, rows, lanes]
            act = (0.5 * pre) * (1.0 + lax.erf(pre * INV_SQRT2)) * w
            act_ref[half, rows, lanes] = act.astype(act_ref.dtype)


def _peer_dense_kernel(h2t_ref, down_ref, up_ref, e1_ref, e2_ref, theta_ref,
                       x1_ref, gate_ref, g_ref, out_ref, pre_ref, act_ref):
    e = pl.program_id(1)
    last = pl.num_programs(1) - 1

    @pl.when(e == 0)
    def _():
        out_ref[...] = jnp.zeros(out_ref.shape, F32)
        pre_ref[...] = jnp.zeros(pre_ref.shape, F32)

    for half in range(2):
        rows = slice(half * PEER_HALF_TILE, (half + 1) * PEER_HALF_TILE)
        _gate_act(pre_ref, act_ref, e1_ref, e2_ref, theta_ref, half)
        out_ref[...] += lax.dot_general(act_ref[half], up_ref[rows, :], (((0,), (0,)), ((), ())),
                                        preferred_element_type=F32)
        pre_ref[half] = jnp.dot(down_ref[rows, :], h2t_ref[...], preferred_element_type=F32)

    @pl.when(e == last)
    def _():
        y = out_ref[...]
        out_ref[...] = x1_ref[...] + gate_ref[...] * (_rms(y) * g_ref[...])


def _peer_dense(h2t, down, up, e1, e2, theta, x1, gate2, g3):
    d, s = h2t.shape
    n_exp = down.shape[0]
    tt = 512
    et = 2 * PEER_HALF_TILE
    n_e = n_exp // et
    vec = pl.BlockSpec((1, d), lambda t, e: (0, 0))
    cur = lambda e: jnp.minimum(e, n_e - 1)
    prev = lambda e: jnp.maximum(e - 1, 0)
    return pl.pallas_call(
        _peer_dense_kernel,
        out_shape=jax.ShapeDtypeStruct((s, d), F32),
        grid=(s // tt, n_e + 1),
        in_specs=[pl.BlockSpec((d, tt), lambda t, e: (0, t)),
                  pl.BlockSpec((et, d), lambda t, e: (cur(e), 0)),
                  pl.BlockSpec((et, d), lambda t, e: (prev(e), 0)),
                  pl.BlockSpec((PEER_HEADS, 1, 8, tt), lambda t, e: (0, prev(e), 0, t)),
                  pl.BlockSpec((PEER_HEADS, PEER_N_KEYS, tt), lambda t, e: (0, 0, t)),
                  pl.BlockSpec((PEER_HEADS, tt), lambda t, e: (0, t)),
                  pl.BlockSpec((tt, d), lambda t, e: (t, 0)),
                  vec, vec],
        out_specs=pl.BlockSpec((tt, d), lambda t, e: (t, 0)),
        scratch_shapes=[pltpu.VMEM((2, PEER_HALF_TILE, tt), F32),
                        pltpu.VMEM((2, PEER_HALF_TILE, tt), BF16)],
        compiler_params=_params(("parallel", "arbitrary")),
        name="peer_dense",
    )(h2t, down, up, e1, e2, theta, x1, gate2, g3)


def kernel(x, c, positions, ada_w, ada_b, norm_gains, w_in, w_attn_o, conv_w, conv_b, conv_ln_g,
           conv_ln_b, w_conv_o, w_out, peer_w_q, peer_sub_keys, peer_down, peer_up):
    batch, seq, d = x.shape
    depth = ada_w.shape[0]
    assert batch == 1, "kernels are written for a single sequence"
    assert all(window // dilation == ATTN_BLOCK for window, dilation in DILATED_GROUPS)
    cc = conv_w.shape[-1]
    xs = x[0]
    inv_freq = ROPE_THETA ** (-jnp.arange(0, HEAD_DIM, 2, dtype=F32) / HEAD_DIM)
    invf_row = jnp.concatenate([inv_freq, inv_freq])[None, :]
    cos, sin = _rope_tables(positions[0].astype(F32)[:, None], invf_row)
    row = lambda v: v[None, :]
    for l in range(depth):
        mod = _modulation(c[0][:, None], ada_w[l], ada_b[l][None, :])
        shift1, scale1, gate1, shift2, scale2, gate2 = [mod[:, k * d:(k + 1) * d] for k in range(6)]
        g = norm_gains[l]
        w_in_b = w_in[l].astype(BF16)

        h = _norm_mod(xs, row(g[0]), scale1, shift1)
        u = _glu_proj(h, w_in_b, cc)
        gates = _gates_proj(h, w_in_b, 3 * ATTN_WIDTH + 2 * cc, 2 * d)
        outs, lses = [], []
        for gi, (_, dilation) in enumerate(DILATED_GROUPS):
            o, lse = _dilated_attention(_qkv_proj(h, w_in_b, cos, sin, gi, dilation))
            outs.append(o)
            lses.append(lse)
        merged = _mix(outs, lses, u, gates, w_attn_o[l].astype(BF16), conv_w[l][:, 0, :],
                      row(conv_b[l]), row(conv_ln_g[l]), row(conv_ln_b[l]), w_conv_o[l].astype(BF16))
        x1, h2t = _out_proj(merged, w_out[l].astype(BF16), xs, gate1, row(g[1]), row(g[2]),
                            scale2, shift2)

        e1, e2, theta = _peer_query(h2t, peer_w_q[l].T.astype(BF16), peer_sub_keys[l])
        xs = _peer_dense(h2t, peer_down[l].astype(BF16), peer_up[l].astype(BF16),
                         e1, e2, theta, x1, gate2, row(g[3]))
    return xs[None]
```

```python
import functools
import math

import jax
import jax.numpy as jnp
from jax import lax
from jax.experimental import pallas as pl
from jax.experimental.pallas import tpu as pltpu

F32 = jnp.float32
BF16 = jnp.bfloat16

HEAD_DIM = 128
HEADS_PER_GROUP = 4
DILATED_GROUPS = ((128, 1), (512, 4), (2048, 16))
N_GROUPS = len(DILATED_GROUPS)
GROUP_WIDTH = HEADS_PER_GROUP * HEAD_DIM
ATTN_WIDTH = N_GROUPS * GROUP_WIDTH
ATTN_BLOCK = 128
ROPE_THETA = 10000.0
CONV_WIDTH = 31
CONV_HALO = 32
PEER_HEADS = 8
PEER_N_KEYS = 128
PEER_HALF = 128
PEER_TOPK = 16
NORM_EPS = 1e-6
MASK_VALUE = -1e30
NEG_INF = float("-inf")
INV_SQRT2 = 1.0 / math.sqrt(2.0)

SUBLANES = 8
COL = 512
VMEM_LIMIT = 56 * 1024 * 1024

_CAND_ROWS = [PEER_TOPK // (a + 1) for a in range(PEER_TOPK)]
_N_CAND = sum(_CAND_ROWS)
_CAND_PAD = -(-_N_CAND // 8) * 8


def _params(sem):
    return pltpu.CompilerParams(dimension_semantics=sem, vmem_limit_bytes=VMEM_LIMIT)


def _rms(x):
    return x * lax.rsqrt(jnp.mean(x * x, axis=-1, keepdims=True) + NORM_EPS)


def _mod_kernel(c_ref, w_ref, b_ref, o_ref):
    rows = 256

    def body(k, acc):
        r = pl.multiple_of(k * rows, rows)
        c = c_ref[pl.ds(r, rows), :]
        sc = c * jax.nn.sigmoid(c)
        return acc + jnp.sum(w_ref[pl.ds(r, rows), :] * sc, axis=0, keepdims=True)

    acc = lax.fori_loop(0, w_ref.shape[0] // rows, body, jnp.zeros(o_ref.shape, F32))
    o_ref[...] = acc + b_ref[...]


def _modulation(c_col, w, b):
    d, n = w.shape
    tn = 512
    return pl.pallas_call(
        _mod_kernel,
        out_shape=jax.ShapeDtypeStruct((1, n), F32),
        grid=(n // tn,),
        in_specs=[pl.BlockSpec((d, 1), lambda j: (0, 0)),
                  pl.BlockSpec((d, tn), lambda j: (0, j)),
                  pl.BlockSpec((1, tn), lambda j: (0, j))],
        out_specs=pl.BlockSpec((1, tn), lambda j: (0, j)),
        compiler_params=_params(("parallel",)),
        name="mod",
    )(c_col, w, b)


def _rope_kernel(pos_ref, invf_ref, cos_ref, sin_ref):
    ang = pos_ref[...] * invf_ref[...]
    cos_ref[...] = jnp.cos(ang)
    s = jnp.sin(ang)
    lane = lax.broadcasted_iota(jnp.int32, s.shape, 1)
    sin_ref[...] = jnp.where(lane < HEAD_DIM // 2, -s, s)


def _rope_tables(pos_col, invf_row):
    s = pos_col.shape[0]
    ts = 2048
    return pl.pallas_call(
        _rope_kernel,
        out_shape=(jax.ShapeDtypeStruct((s, HEAD_DIM), F32),) * 2,
        grid=(s // ts,),
        in_specs=[pl.BlockSpec((ts, 1), lambda i: (i, 0)),
                  pl.BlockSpec((1, HEAD_DIM), lambda i: (0, 0))],
        out_specs=(pl.BlockSpec((ts, HEAD_DIM), lambda i: (i, 0)),) * 2,
        compiler_params=_params(("parallel",)),
        name="rope",
    )(pos_col, invf_row)


def _norm_mod_kernel(x_ref, g_ref, scale_ref, shift_ref, o_ref):
    y = _rms(x_ref[...]) * g_ref[...]
    o_ref[...] = (y * (1.0 + scale_ref[...]) + shift_ref[...]).astype(o_ref.dtype)


def _norm_mod(x, g, scale, shift):
    s, d = x.shape
    tm = 512
    vec = pl.BlockSpec((1, d), lambda i: (0, 0))
    return pl.pallas_call(
        _norm_mod_kernel,
        out_shape=jax.ShapeDtypeStruct((s, d), BF16),
        grid=(s // tm,),
        in_specs=[pl.BlockSpec((tm, d), lambda i: (i, 0)), vec, vec, vec],
        out_specs=pl.BlockSpec((tm, d), lambda i: (i, 0)),
        compiler_params=_params(("parallel",)),
        name="norm_mod",
    )(x, g, scale, shift)


def _qkv_kernel(h_ref, w_ref, cos_ref, sin_ref, o_ref, stage_ref):
    acc = jnp.dot(h_ref[...], w_ref[...], preferred_element_type=F32)
    part = pl.program_id(1)

    heads = [slice(hh * HEAD_DIM, (hh + 1) * HEAD_DIM) for hh in range(HEADS_PER_GROUP)]

    @pl.when(part < 2)
    def _():
        cos = cos_ref[...]
        sin = sin_ref[...]
        for hh, cols in enumerate(heads):
            t = acc[:, cols]
            stage_ref[hh] = t * cos + pltpu.roll(t, HEAD_DIM // 2, axis=1) * sin

    @pl.when(part == 2)
    def _():
        for hh, cols in enumerate(heads):
            stage_ref[hh] = acc[:, cols]

    dilation, rows, _ = o_ref.shape
    for r in range(dilation):
        for hh, cols in enumerate(heads):
            o_ref[r, :, cols] = stage_ref[hh, pl.ds(r, rows, stride=dilation), :].astype(o_ref.dtype)


def _qkv_proj(h, w_in, cos, sin, group, dilation):
    s, d = h.shape
    tm = 1024
    return pl.pallas_call(
        _qkv_kernel,
        out_shape=jax.ShapeDtypeStruct((dilation, s // dilation, 3 * GROUP_WIDTH), BF16),
        grid=(s // tm, 3),
        in_specs=[pl.BlockSpec((tm, d), lambda i, j: (i, 0)),
                  pl.BlockSpec((d, COL), lambda i, j: (0, j * N_GROUPS + group)),
                  pl.BlockSpec((tm, HEAD_DIM), lambda i, j: (i, 0)),
                  pl.BlockSpec((tm, HEAD_DIM), lambda i, j: (i, 0))],
        out_specs=pl.BlockSpec((dilation, tm // dilation, COL), lambda i, j: (0, i, j)),
        scratch_shapes=[pltpu.VMEM((HEADS_PER_GROUP, tm, HEAD_DIM), F32)],
        compiler_params=_params(("parallel", "arbitrary")),
        name=f"qkv_d{dilation}",
    )(h, w_in, cos, sin)


def _glu_kernel(h_ref, wa_ref, wb_ref, o_ref):
    h = h_ref[...]
    a = jnp.dot(h, wa_ref[...], preferred_element_type=F32)
    b = jnp.dot(h, wb_ref[...], preferred_element_type=F32)
    o_ref[...] = a * jax.nn.sigmoid(b)


def _glu_proj(h, w_in, conv_channels):
    s, d = h.shape
    tm = 1024
    a0 = 3 * ATTN_WIDTH // COL
    nb = conv_channels // COL
    return pl.pallas_call(
        _glu_kernel,
        out_shape=jax.ShapeDtypeStruct((s, conv_channels), F32),
        grid=(s // tm, nb),
        in_specs=[pl.BlockSpec((tm, d), lambda i, j: (i, 0)),
                  pl.BlockSpec((d, COL), lambda i, j: (0, a0 + j)),
                  pl.BlockSpec((d, COL), lambda i, j: (0, a0 + nb + j))],
        out_specs=pl.BlockSpec((tm, COL), lambda i, j: (i, j)),
        compiler_params=_params(("parallel", "arbitrary")),
        name="glu",
    )(h, w_in, w_in)


def _gates_kernel(h_ref, w_ref, o_ref):
    o_ref[...] = jax.nn.sigmoid(jnp.dot(h_ref[...], w_ref[...], preferred_element_type=F32))


def _gates_proj(h, w_in, col0, width):
    s, d = h.shape
    tm = 1024
    c0 = col0 // COL
    return pl.pallas_call(
        _gates_kernel,
        out_shape=jax.ShapeDtypeStruct((s, width), F32),
        grid=(s // tm, width // COL),
        in_specs=[pl.BlockSpec((tm, d), lambda i, j: (i, 0)),
                  pl.BlockSpec((d, COL), lambda i, j: (0, c0 + j))],
        out_specs=pl.BlockSpec((tm, COL), lambda i, j: (i, j)),
        compiler_params=_params(("parallel", "arbitrary")),
        name="gates",
    )(h, w_in)


ATTN_Q_BLOCKS = 4


def _attn_kernel(q_ref, kc_ref, kp_ref, vc_ref, vp_ref, o_ref, l_ref):
    n = pl.program_id(1)
    qi = lax.broadcasted_iota(jnp.int32, (ATTN_BLOCK, 2 * ATTN_BLOCK), 0)
    kj = lax.broadcasted_iota(jnp.int32, (ATTN_BLOCK, 2 * ATTN_BLOCK), 1)
    band = jnp.abs(kj - qi - ATTN_BLOCK // 2) <= ATTN_BLOCK // 2
    first_lo = jnp.where(n == 0, ATTN_BLOCK, 0)
    band_first = band & (kj >= first_lo)
    scale = HEAD_DIM ** -0.5
    for hh in range(HEADS_PER_GROUP):
        cols = slice(hh * HEAD_DIM, (hh + 1) * HEAD_DIM)
        for b in range(ATTN_Q_BLOCKS):
            rows = slice(b * ATTN_BLOCK, (b + 1) * ATTN_BLOCK)
            q = q_ref[rows, cols]
            if b == 0:
                k = jnp.concatenate([kp_ref[:, cols], kc_ref[0:ATTN_BLOCK, cols]], axis=0)
                v = jnp.concatenate([vp_ref[:, cols], vc_ref[0:ATTN_BLOCK, cols]], axis=0)
                valid = band_first
            else:
                band_rows = slice((b - 1) * ATTN_BLOCK, (b + 1) * ATTN_BLOCK)
                k = kc_ref[band_rows, cols]
                v = vc_ref[band_rows, cols]
                valid = band
            s = lax.dot_general(q, k, (((1,), (1,)), ((), ())), preferred_element_type=F32) * scale
            s = jnp.where(valid, s, MASK_VALUE)
            m = jnp.max(s, axis=-1, keepdims=True)
            p = jnp.exp(s - m)
            den = jnp.sum(p, axis=-1, keepdims=True)
            o = jnp.dot(p.astype(BF16), v, preferred_element_type=F32)
            o_ref[rows, cols] = o * (1.0 / den)
            l_ref[rows, cols] = jnp.broadcast_to(m + jnp.log(den), (ATTN_BLOCK, HEAD_DIM))


def _dilated_attention(qkv):
    dilation, sub_len, _ = qkv.shape
    qt = ATTN_Q_BLOCKS * ATTN_BLOCK

    def cur(part):
        return pl.BlockSpec((None, qt, COL), lambda r, n: (r, n, part))

    def prev(part):
        return pl.BlockSpec((None, ATTN_BLOCK, COL),
                            lambda r, n: (r, jnp.maximum(n * ATTN_Q_BLOCKS - 1, 0), part))

    out_sds = jax.ShapeDtypeStruct((dilation, sub_len, GROUP_WIDTH), F32)
    out_spec = pl.BlockSpec((None, qt, GROUP_WIDTH), lambda r, n: (r, n, 0))
    return pl.pallas_call(
        _attn_kernel,
        out_shape=(out_sds, out_sds),
        grid=(dilation, sub_len // qt),
        in_specs=[cur(0), cur(1), prev(1), cur(2), prev(2)],
        out_specs=(out_spec, out_spec),
        compiler_params=_params(("parallel", "arbitrary")),
        name=f"attn_d{dilation}",
    )(qkv, qkv, qkv, qkv, qkv)


def _mix_kernel(o0_ref, o1_ref, o2_ref, l0_ref, l1_ref, l2_ref, u_ref, uh_ref, ga_ref, gc_ref,
                wao_ref, cw_ref, cb_ref, lg_ref, lb_ref, wco_ref, out_ref, ucat_ref, nat_ref, shift_ref):
    i = pl.program_id(0)
    tm = u_ref.shape[0]

    def natural(ref, slot):
        dilation, rows, _ = ref.shape
        if dilation == 1:
            return ref[0]
        for hh in range(HEADS_PER_GROUP):
            for r in range(dilation):
                nat_ref[slot, hh, pl.ds(r, rows, stride=dilation), :] = ref[r, :, hh * HEAD_DIM:(hh + 1) * HEAD_DIM]
        return jnp.concatenate([nat_ref[slot, hh] for hh in range(HEADS_PER_GROUP)], axis=1)

    l0, l1, l2 = natural(l0_ref, 0), natural(l1_ref, 1), natural(l2_ref, 2)
    m = jnp.maximum(jnp.maximum(l0, l1), l2)
    e0, e1, e2 = jnp.exp(l0 - m), jnp.exp(l1 - m), jnp.exp(l2 - m)
    attn = (e0 * natural(o0_ref, 3) + e1 * natural(o1_ref, 4) + e2 * natural(o2_ref, 5)) / (e0 + e1 + e2)
    attn_o = jnp.dot(attn.astype(BF16), wao_ref[...], preferred_element_type=F32)
    ucat_ref[0:CONV_HALO, :] = jnp.where(i > 0, uh_ref[...], 0.0)
    ucat_ref[CONV_HALO:, :] = u_ref[...]
    off = CONV_HALO - (CONV_WIDTH - 1)
    span = tm + CONV_HALO - SUBLANES
    for s in range(1, SUBLANES):
        shift_ref[s - 1] = ucat_ref[s:s + span, :]
    conv = jnp.zeros(u_ref.shape, F32) + cb_ref[...]
    for w in range(CONV_WIDTH):
        q, s = divmod(off + w, SUBLANES)
        src = ucat_ref if s == 0 else shift_ref.at[s - 1]
        conv = conv + src[q * SUBLANES:q * SUBLANES + tm, :] * cw_ref[w:w + 1, :]
    mu = jnp.mean(conv, axis=-1, keepdims=True)
    cen = conv - mu
    var = jnp.mean(cen * cen, axis=-1, keepdims=True)
    y = cen * lax.rsqrt(var + NORM_EPS) * lg_ref[...] + lb_ref[...]
    y = y * jax.nn.sigmoid(y)
    conv_o = jnp.dot(y.astype(BF16), wco_ref[...], preferred_element_type=F32)
    out_ref[...] = (ga_ref[...] * attn_o + gc_ref[...] * conv_o).astype(out_ref.dtype)


def _mix(outs, lses, u, gates, w_attn_o, conv_w, conv_b, ln_g, ln_b, w_conv_o):
    s, cc = u.shape
    d = w_attn_o.shape[1]
    tm = 256
    row = lambda i: (i, 0)
    const = lambda i: (0, 0)
    cvec = pl.BlockSpec((1, cc), const)

    def grp(arr):
        dilation = arr.shape[0]
        return pl.BlockSpec((dilation, tm // dilation, GROUP_WIDTH), lambda i: (0, i, 0))

    return pl.pallas_call(
        _mix_kernel,
        out_shape=jax.ShapeDtypeStruct((s, d), BF16),
        grid=(s // tm,),
        in_specs=[grp(a) for a in (*outs, *lses)] + [
            pl.BlockSpec((tm, cc), row),
            pl.BlockSpec((CONV_HALO, cc), lambda i: (jnp.maximum(i * (tm // CONV_HALO) - 1, 0), 0)),
            pl.BlockSpec((tm, d), lambda i: (i, 0)),
            pl.BlockSpec((tm, d), lambda i: (i, 1)),
            pl.BlockSpec((GROUP_WIDTH, d), const),
            pl.BlockSpec((CONV_WIDTH, cc), const),
            cvec, cvec, cvec,
            pl.BlockSpec((cc, d), const)],
        out_specs=pl.BlockSpec((tm, d), row),
        scratch_shapes=[pltpu.VMEM((tm + CONV_HALO, cc), F32),
                        pltpu.VMEM((2 * N_GROUPS, HEADS_PER_GROUP, tm, HEAD_DIM), F32),
                        pltpu.VMEM((SUBLANES - 1, tm + CONV_HALO - SUBLANES, cc), F32)],
        compiler_params=_params(("parallel",)),
        name="mix",
    )(*outs, *lses, u, u, gates, gates, w_attn_o, conv_w, conv_b, ln_g, ln_b, w_conv_o)


def _out_proj_kernel(m_ref, w_ref, x_ref, gate_ref, g1_ref, g2_ref, scale_ref, shift_ref,
                     x1_ref, h2t_ref):
    y = jnp.dot(m_ref[...], w_ref[...], preferred_element_type=F32)
    x1 = x_ref[...] + gate_ref[...] * (_rms(y) * g1_ref[...])
    x1_ref[...] = x1
    h2 = (_rms(x1) * g2_ref[...]) * (1.0 + scale_ref[...]) + shift_ref[...]
    h2t_ref[...] = h2.T.astype(h2t_ref.dtype)


def _out_proj(merged, w_out, x, gate1, g1, g2, scale2, shift2):
    s, d = x.shape
    tm = 256
    row = lambda i: (i, 0)
    vec = pl.BlockSpec((1, d), lambda i: (0, 0))
    return pl.pallas_call(
        _out_proj_kernel,
        out_shape=(jax.ShapeDtypeStruct((s, d), F32), jax.ShapeDtypeStruct((d, s), BF16)),
        grid=(s // tm,),
        in_specs=[pl.BlockSpec((tm, d), row), pl.BlockSpec((d, d), lambda i: (0, 0)),
                  pl.BlockSpec((tm, d), row), vec, vec, vec, vec, vec],
        out_specs=(pl.BlockSpec((tm, d), row), pl.BlockSpec((d, tm), lambda i: (0, i))),
        compiler_params=_params(("parallel",)),
        name="out_proj",
    )(merged, w_out, x, gate1, g1, g2, scale2, shift2)


def _odd_even_merge_sort(lo, hi):
    def merge(lo, hi, r):
        step = 2 * r
        if step < hi - lo:
            yield from merge(lo, hi, step)
            yield from merge(lo + r, hi, step)
            for i in range(lo + r, hi - r, step):
                yield (i, i + r)
        else:
            yield (lo, lo + r)

    if hi > lo:
        mid = lo + (hi - lo) // 2
        yield from _odd_even_merge_sort(lo, mid)
        yield from _odd_even_merge_sort(mid + 1, hi)
        yield from merge(lo, hi, 1)


def _bitonic_merge(n):
    half = n // 2
    while half >= 1:
        for i in range(n):
            if (i // half) % 2 == 0:
                yield (i, i + half)
        half //= 2


_SORT_PAIRS = {n: tuple(_odd_even_merge_sort(0, n - 1)) for n in (PEER_TOPK // 2, PEER_TOPK)}
_MERGE_PAIRS = tuple(_bitonic_merge(PEER_TOPK))


def _compare_exchange(v, pairs):
    for i, j in pairs:
        v[i], v[j] = jnp.maximum(v[i], v[j]), jnp.minimum(v[i], v[j])


def _top16_rows(x):
    blocks = x.shape[0] // SUBLANES
    assert x.shape[0] == blocks * SUBLANES and blocks <= PEER_TOPK
    n = PEER_TOPK // 2 if blocks <= PEER_TOPK // 2 else PEER_TOPK
    v = [x[k * SUBLANES:(k + 1) * SUBLANES, :] for k in range(blocks)]
    v += [jnp.full_like(v[0], NEG_INF)] * (n - blocks)
    _compare_exchange(v, _SORT_PAIRS[n])
    shift = SUBLANES // 2
    while shift >= 1:
        partner = [pltpu.roll(blk, shift, axis=0) for blk in v]
        if len(v) < PEER_TOPK:
            v = v + partner[::-1]
        else:
            v = [jnp.maximum(v[k], partner[PEER_TOPK - 1 - k]) for k in range(PEER_TOPK)]
        _compare_exchange(v, _MERGE_PAIRS)
        shift //= 2
    return [blk[0:1, :] for blk in v]


def _peer_query_kernel(h2t_ref, wqt_ref, keys_ref, e1_ref, e2_ref, theta_ref,
                       qt_ref, top_ref, cand_ref):
    qt_ref[...] = jnp.dot(wqt_ref[...], h2t_ref[...], preferred_element_type=F32)
    for h in range(PEER_HEADS):
        u = []
        for side in range(2):
            r0 = (2 * h + side) * PEER_HALF
            qc = qt_ref[r0:r0 + PEER_HALF, :].astype(BF16)
            sc = jnp.dot(keys_ref[h, side].astype(BF16), qc, preferred_element_type=F32)
            us = jnp.exp(sc - jnp.max(sc, axis=0, keepdims=True))
            vals = _top16_rows(us)
            for k in range(PEER_TOPK):
                top_ref[side, k:k + 1, :] = vals[k]
            u.append(us)
        cand_ref[...] = jnp.full(cand_ref.shape, -1.0, F32)
        off = 0
        for a, nb in enumerate(_CAND_ROWS):
            cand_ref[off:off + nb, :] = top_ref[0, a:a + 1, :] * top_ref[1, 0:nb, :]
            off += nb
        cand = cand_ref[...]
        selected = cand >= _top16_rows(cand)[-1]
        inv_z = 1.0 / jnp.sum(jnp.where(selected, cand, 0.0), axis=0, keepdims=True)
        off = 0
        for a, nb in enumerate(_CAND_ROWS):
            cand_ref[off:off + nb, :] = (top_ref[0, a:a + 1, :] * inv_z) * top_ref[1, 0:nb, :]
            off += nb
        theta = jnp.min(jnp.where(selected, cand_ref[...], jnp.inf), axis=0, keepdims=True)
        e1 = u[0] * inv_z
        for blk in range(PEER_N_KEYS // 8):
            e1_ref[h, blk] = e1[blk * 8:(blk + 1) * 8, :]
        e2_ref[h] = u[1]
        theta_ref[h:h + 1, :] = theta


def _peer_query(h2t, wq_t, sub_keys):
    d, s = h2t.shape
    tt = 512
    nq = wq_t.shape[0]
    side_blocked = jax.ShapeDtypeStruct((PEER_HEADS, PEER_N_KEYS // 8, 8, s), F32)
    side_flat = jax.ShapeDtypeStruct((PEER_HEADS, PEER_N_KEYS, s), F32)
    blocked_spec = pl.BlockSpec((PEER_HEADS, PEER_N_KEYS // 8, 8, tt), lambda t: (0, 0, 0, t))
    flat_spec = pl.BlockSpec((PEER_HEADS, PEER_N_KEYS, tt), lambda t: (0, 0, t))
    return pl.pallas_call(
        _peer_query_kernel,
        out_shape=(side_blocked, side_flat, jax.ShapeDtypeStruct((PEER_HEADS, s), F32)),
        grid=(s // tt,),
        in_specs=[pl.BlockSpec((d, tt), lambda t: (0, t)),
                  pl.BlockSpec((nq, d), lambda t: (0, 0)),
                  pl.BlockSpec(sub_keys.shape, lambda t: (0, 0, 0, 0))],
        out_specs=(blocked_spec, flat_spec, pl.BlockSpec((PEER_HEADS, tt), lambda t: (0, t))),
        scratch_shapes=[pltpu.VMEM((nq, tt), F32),
                        pltpu.VMEM((2, PEER_TOPK, tt), F32),
                        pltpu.VMEM((_CAND_PAD, tt), F32)],
        compiler_params=_params(("parallel",)),
        name="peer_query",
    )(h2t, wq_t, sub_keys)


PEER_LANE_CHUNK = 128


PEER_TILE_KEYS = 8
PEER_TILE = PEER_TILE_KEYS * PEER_N_KEYS
PEER_PIECE = 256


def _gate_act(pre_ref, act_ref, e1_ref, e2_ref, theta_ref, key, c):
    rows = slice(key * PEER_N_KEYS, (key + 1) * PEER_N_KEYS)
    lanes = slice(c * PEER_LANE_CHUNK, (c + 1) * PEER_LANE_CHUNK)
    w = jnp.zeros((PEER_N_KEYS, PEER_LANE_CHUNK), F32)
    for h in range(PEER_HEADS):
        p = e1_ref[h, 0, key:key + 1, lanes] * e2_ref[h, :, lanes]
        w = w + jnp.where(p >= theta_ref[h:h + 1, lanes], p, 0.0)
    pre = pre_ref[rows, lanes]
    act = (0.5 * pre) * (1.0 + lax.erf(pre * INV_SQRT2)) * w
    act_ref[rows, lanes] = act.astype(act_ref.dtype)


def _peer_dense_kernel(h2t_ref, down_ref, up_ref, e1_ref, e2_ref, theta_ref,
                       x1_ref, gate_ref, g_ref, out_ref, pre_ref, act_ref):
    e = pl.program_id(1)
    last = pl.num_programs(1) - 1
    tt = pre_ref.shape[1]

    @pl.when(e == 0)
    def _():
        out_ref[...] = jnp.zeros(out_ref.shape, F32)
        pre_ref[...] = jnp.zeros(pre_ref.shape, F32)

    contract_rows = (((0,), (0,)), ((), ()))
    keys_per_piece = PEER_PIECE // PEER_N_KEYS
    chunks_per_piece = PEER_PIECE // PEER_LANE_CHUNK
    for kp in range(PEER_TILE // PEER_PIECE):
        experts = slice(kp * PEER_PIECE, (kp + 1) * PEER_PIECE)
        for cp in range(tt // PEER_PIECE):
            tokens = slice(cp * PEER_PIECE, (cp + 1) * PEER_PIECE)
            for key in range(kp * keys_per_piece, (kp + 1) * keys_per_piece):
                for c in range(cp * chunks_per_piece, (cp + 1) * chunks_per_piece):
                    _gate_act(pre_ref, act_ref, e1_ref, e2_ref, theta_ref, key, c)
            out_ref[tokens, :] += lax.dot_general(act_ref[experts, tokens], up_ref[experts, :],
                                                  contract_rows, preferred_element_type=F32)
        pre_ref[experts, :] = jnp.dot(down_ref[experts, :], h2t_ref[...], preferred_element_type=F32)

    @pl.when(e == last)
    def _():
        y = out_ref[...]
        out_ref[...] = x1_ref[...] + gate_ref[...] * (_rms(y) * g_ref[...])


def _peer_dense(h2t, down, up, e1, e2, theta, x1, gate2, g3):
    d, s = h2t.shape
    n_exp = up.shape[0]
    tt = 512
    et = PEER_TILE
    n_e = n_exp // et
    vec = pl.BlockSpec((1, d), lambda t, e: (0, 0))
    cur = lambda e: jnp.minimum(e, n_e - 1)
    prev = lambda e: jnp.maximum(e - 1, 0)
    return pl.pallas_call(
        _peer_dense_kernel,
        out_shape=jax.ShapeDtypeStruct((s, d), F32),
        grid=(s // tt, n_e + 1),
        in_specs=[pl.BlockSpec((d, tt), lambda t, e: (0, t)),
                  pl.BlockSpec((et, d), lambda t, e: (cur(e), 0)),
                  pl.BlockSpec((et, d), lambda t, e: (prev(e), 0)),
                  pl.BlockSpec((PEER_HEADS, 1, 8, tt), lambda t, e: (0, prev(e), 0, t)),
                  pl.BlockSpec((PEER_HEADS, PEER_N_KEYS, tt), lambda t, e: (0, 0, t)),
                  pl.BlockSpec((PEER_HEADS, tt), lambda t, e: (0, t)),
                  pl.BlockSpec((tt, d), lambda t, e: (t, 0)),
                  vec, vec],
        out_specs=pl.BlockSpec((tt, d), lambda t, e: (t, 0)),
        scratch_shapes=[pltpu.VMEM((et, tt), F32), pltpu.VMEM((et, tt), BF16)],
        compiler_params=_params(("parallel", "arbitrary")),
        name="peer_dense",
    )(h2t, down, up, e1, e2, theta, x1, gate2, g3)


def kernel(x, c, positions, ada_w, ada_b, norm_gains, w_in, w_attn_o, conv_w, conv_b, conv_ln_g,
           conv_ln_b, w_conv_o, w_out, peer_w_q, peer_sub_keys, peer_down, peer_up):
    batch, seq, d = x.shape
    depth = ada_w.shape[0]
    assert batch == 1, "kernels are written for a single sequence"
    assert all(window // dilation == ATTN_BLOCK for window, dilation in DILATED_GROUPS)
    cc = conv_w.shape[-1]
    xs = x[0]
    inv_freq = ROPE_THETA ** (-jnp.arange(0, HEAD_DIM, 2, dtype=F32) / HEAD_DIM)
    invf_row = jnp.concatenate([inv_freq, inv_freq])[None, :]
    cos, sin = _rope_tables(positions[0].astype(F32)[:, None], invf_row)
    row = lambda v: v[None, :]
    for l in range(depth):
        mod = _modulation(c[0][:, None], ada_w[l], ada_b[l][None, :])
        shift1, scale1, gate1, shift2, scale2, gate2 = [mod[:, k * d:(k + 1) * d] for k in range(6)]
        g = norm_gains[l]
        w_in_b = w_in[l].astype(BF16)

        h = _norm_mod(xs, row(g[0]), scale1, shift1)
        u = _glu_proj(h, w_in_b, cc)
        gates = _gates_proj(h, w_in_b, 3 * ATTN_WIDTH + 2 * cc, 2 * d)
        outs, lses = [], []
        for gi, (_, dilation) in enumerate(DILATED_GROUPS):
            o, lse = _dilated_attention(_qkv_proj(h, w_in_b, cos, sin, gi, dilation))
            outs.append(o)
            lses.append(lse)
        merged = _mix(outs, lses, u, gates, w_attn_o[l].astype(BF16), conv_w[l][:, 0, :],
                      row(conv_b[l]), row(conv_ln_g[l]), row(conv_ln_b[l]), w_conv_o[l].astype(BF16))
        x1, h2t = _out_proj(merged, w_out[l].astype(BF16), xs, gate1, row(g[1]), row(g[2]),
                            scale2, shift2)

        e1, e2, theta = _peer_query(h2t, peer_w_q[l].T.astype(BF16), peer_sub_keys[l])
        xs = _peer_dense(h2t, peer_down[l].astype(BF16), peer_up[l].astype(BF16),
                         e1, e2, theta, x1, gate2, row(g[3]))
    return xs[None]
```

```python
import functools
import math

import jax
import jax.numpy as jnp
from jax import lax
from jax.experimental import pallas as pl
from jax.experimental.pallas import tpu as pltpu

F32 = jnp.float32
BF16 = jnp.bfloat16

HEAD_DIM = 128
HEADS_PER_GROUP = 4
DILATED_GROUPS = ((128, 1), (512, 4), (2048, 16))
N_GROUPS = len(DILATED_GROUPS)
GROUP_WIDTH = HEADS_PER_GROUP * HEAD_DIM
ATTN_WIDTH = N_GROUPS * GROUP_WIDTH
ATTN_BLOCK = 128
ROPE_THETA = 10000.0
CONV_WIDTH = 31
CONV_HALO = 32
PEER_HEADS = 8
PEER_N_KEYS = 128
PEER_HALF = 128
PEER_TOPK = 16
NORM_EPS = 1e-6
MASK_VALUE = -1e30
NEG_INF = float("-inf")
INV_SQRT2 = 1.0 / math.sqrt(2.0)

SUBLANES = 8
COL = 512
VMEM_LIMIT = 56 * 1024 * 1024

_CAND_ROWS = [PEER_TOPK // (a + 1) for a in range(PEER_TOPK)]
_N_CAND = sum(_CAND_ROWS)
_CAND_PAD = -(-_N_CAND // 8) * 8


def _params(sem):
    return pltpu.CompilerParams(dimension_semantics=sem, vmem_limit_bytes=VMEM_LIMIT)


def _rms(x):
    return x * lax.rsqrt(jnp.mean(x * x, axis=-1, keepdims=True) + NORM_EPS)


def _mod_kernel(c_ref, w_ref, b_ref, o_ref):
    rows = 256

    def body(k, acc):
        r = pl.multiple_of(k * rows, rows)
        c = c_ref[pl.ds(r, rows), :]
        sc = c * jax.nn.sigmoid(c)
        return acc + jnp.sum(w_ref[pl.ds(r, rows), :] * sc, axis=0, keepdims=True)

    acc = lax.fori_loop(0, w_ref.shape[0] // rows, body, jnp.zeros(o_ref.shape, F32))
    o_ref[...] = acc + b_ref[...]


def _modulation(c_col, w, b):
    d, n = w.shape
    tn = 512
    return pl.pallas_call(
        _mod_kernel,
        out_shape=jax.ShapeDtypeStruct((1, n), F32),
        grid=(n // tn,),
        in_specs=[pl.BlockSpec((d, 1), lambda j: (0, 0)),
                  pl.BlockSpec((d, tn), lambda j: (0, j)),
                  pl.BlockSpec((1, tn), lambda j: (0, j))],
        out_specs=pl.BlockSpec((1, tn), lambda j: (0, j)),
        compiler_params=_params(("parallel",)),
        name="mod",
    )(c_col, w, b)


def _rope_kernel(pos_ref, invf_ref, cos_ref, sin_ref):
    ang = pos_ref[...] * invf_ref[...]
    cos_ref[...] = jnp.cos(ang)
    s = jnp.sin(ang)
    lane = lax.broadcasted_iota(jnp.int32, s.shape, 1)
    sin_ref[...] = jnp.where(lane < HEAD_DIM // 2, -s, s)


def _rope_tables(pos_col, invf_row):
    s = pos_col.shape[0]
    ts = 2048
    return pl.pallas_call(
        _rope_kernel,
        out_shape=(jax.ShapeDtypeStruct((s, HEAD_DIM), F32),) * 2,
        grid=(s // ts,),
        in_specs=[pl.BlockSpec((ts, 1), lambda i: (i, 0)),
                  pl.BlockSpec((1, HEAD_DIM), lambda i: (0, 0))],
        out_specs=(pl.BlockSpec((ts, HEAD_DIM), lambda i: (i, 0)),) * 2,
        compiler_params=_params(("parallel",)),
        name="rope",
    )(pos_col, invf_row)


def _norm_mod_kernel(x_ref, g_ref, scale_ref, shift_ref, o_ref):
    y = _rms(x_ref[...]) * g_ref[...]
    o_ref[...] = (y * (1.0 + scale_ref[...]) + shift_ref[...]).astype(o_ref.dtype)


def _norm_mod(x, g, scale, shift):
    s, d = x.shape
    tm = 512
    vec = pl.BlockSpec((1, d), lambda i: (0, 0))
    return pl.pallas_call(
        _norm_mod_kernel,
        out_shape=jax.ShapeDtypeStruct((s, d), BF16),
        grid=(s // tm,),
        in_specs=[pl.BlockSpec((tm, d), lambda i: (i, 0)), vec, vec, vec],
        out_specs=pl.BlockSpec((tm, d), lambda i: (i, 0)),
        compiler_params=_params(("parallel",)),
        name="norm_mod",
    )(x, g, scale, shift)


def _qkv_kernel(h_ref, w_ref, cos_ref, sin_ref, o_ref, stage_ref):
    acc = jnp.dot(h_ref[...], w_ref[...], preferred_element_type=F32)
    part = pl.program_id(1)

    heads = [slice(hh * HEAD_DIM, (hh + 1) * HEAD_DIM) for hh in range(HEADS_PER_GROUP)]

    @pl.when(part < 2)
    def _():
        cos = cos_ref[...]
        sin = sin_ref[...]
        for hh, cols in enumerate(heads):
            t = acc[:, cols]
            stage_ref[hh] = t * cos + pltpu.roll(t, HEAD_DIM // 2, axis=1) * sin

    @pl.when(part == 2)
    def _():
        for hh, cols in enumerate(heads):
            stage_ref[hh] = acc[:, cols]

    dilation, rows, _ = o_ref.shape
    for r in range(dilation):
        for hh, cols in enumerate(heads):
            o_ref[r, :, cols] = stage_ref[hh, pl.ds(r, rows, stride=dilation), :].astype(o_ref.dtype)


def _qkv_proj(h, w_in, cos, sin, group, dilation):
    s, d = h.shape
    tm = 1024
    return pl.pallas_call(
        _qkv_kernel,
        out_shape=jax.ShapeDtypeStruct((dilation, s // dilation, 3 * GROUP_WIDTH), BF16),
        grid=(s // tm, 3),
        in_specs=[pl.BlockSpec((tm, d), lambda i, j: (i, 0)),
                  pl.BlockSpec((d, COL), lambda i, j: (0, j * N_GROUPS + group)),
                  pl.BlockSpec((tm, HEAD_DIM), lambda i, j: (i, 0)),
                  pl.BlockSpec((tm, HEAD_DIM), lambda i, j: (i, 0))],
        out_specs=pl.BlockSpec((dilation, tm // dilation, COL), lambda i, j: (0, i, j)),
        scratch_shapes=[pltpu.VMEM((HEADS_PER_GROUP, tm, HEAD_DIM), F32)],
        compiler_params=_params(("parallel", "arbitrary")),
        name=f"qkv_d{dilation}",
    )(h, w_in, cos, sin)


def _glu_kernel(h_ref, wa_ref, wb_ref, o_ref):
    h = h_ref[...]
    a = jnp.dot(h, wa_ref[...], preferred_element_type=F32)
    b = jnp.dot(h, wb_ref[...], preferred_element_type=F32)
    o_ref[...] = a * jax.nn.sigmoid(b)


def _glu_proj(h, w_in, conv_channels):
    s, d = h.shape
    tm = 1024
    a0 = 3 * ATTN_WIDTH // COL
    nb = conv_channels // COL
    return pl.pallas_call(
        _glu_kernel,
        out_shape=jax.ShapeDtypeStruct((s, conv_channels), F32),
        grid=(s // tm, nb),
        in_specs=[pl.BlockSpec((tm, d), lambda i, j: (i, 0)),
                  pl.BlockSpec((d, COL), lambda i, j: (0, a0 + j)),
                  pl.BlockSpec((d, COL), lambda i, j: (0, a0 + nb + j))],
        out_specs=pl.BlockSpec((tm, COL), lambda i, j: (i, j)),
        compiler_params=_params(("parallel", "arbitrary")),
        name="glu",
    )(h, w_in, w_in)


def _gates_kernel(h_ref, w_ref, o_ref):
    o_ref[...] = jax.nn.sigmoid(jnp.dot(h_ref[...], w_ref[...], preferred_element_type=F32))


def _gates_proj(h, w_in, col0, width):
    s, d = h.shape
    tm = 1024
    c0 = col0 // COL
    return pl.pallas_call(
        _gates_kernel,
        out_shape=jax.ShapeDtypeStruct((s, width), F32),
        grid=(s // tm, width // COL),
        in_specs=[pl.BlockSpec((tm, d), lambda i, j: (i, 0)),
                  pl.BlockSpec((d, COL), lambda i, j: (0, c0 + j))],
        out_specs=pl.BlockSpec((tm, COL), lambda i, j: (i, j)),
        compiler_params=_params(("parallel", "arbitrary")),
        name="gates",
    )(h, w_in)


ATTN_Q_BLOCKS = 4


def _attn_kernel(q_ref, kc_ref, kp_ref, vc_ref, vp_ref, o_ref, l_ref):
    n = pl.program_id(1)
    qi = lax.broadcasted_iota(jnp.int32, (ATTN_BLOCK, 2 * ATTN_BLOCK), 0)
    kj = lax.broadcasted_iota(jnp.int32, (ATTN_BLOCK, 2 * ATTN_BLOCK), 1)
    band = jnp.abs(kj - qi - ATTN_BLOCK // 2) <= ATTN_BLOCK // 2
    first_lo = jnp.where(n == 0, ATTN_BLOCK, 0)
    band_first = band & (kj >= first_lo)
    scale = HEAD_DIM ** -0.5
    for hh in range(HEADS_PER_GROUP):
        cols = slice(hh * HEAD_DIM, (hh + 1) * HEAD_DIM)
        for b in range(ATTN_Q_BLOCKS):
            rows = slice(b * ATTN_BLOCK, (b + 1) * ATTN_BLOCK)
            q = q_ref[rows, cols]
            if b == 0:
                k = jnp.concatenate([kp_ref[:, cols], kc_ref[0:ATTN_BLOCK, cols]], axis=0)
                v = jnp.concatenate([vp_ref[:, cols], vc_ref[0:ATTN_BLOCK, cols]], axis=0)
                valid = band_first
            else:
                band_rows = slice((b - 1) * ATTN_BLOCK, (b + 1) * ATTN_BLOCK)
                k = kc_ref[band_rows, cols]
                v = vc_ref[band_rows, cols]
                valid = band
            s = lax.dot_general(q, k, (((1,), (1,)), ((), ())), preferred_element_type=F32) * scale
            s = jnp.where(valid, s, MASK_VALUE)
            m = jnp.max(s, axis=-1, keepdims=True)
            p = jnp.exp(s - m)
            den = jnp.sum(p, axis=-1, keepdims=True)
            o = jnp.dot(p.astype(BF16), v, preferred_element_type=F32)
            o_ref[rows, cols] = o * (1.0 / den)
            l_ref[rows, cols] = jnp.broadcast_to(m + jnp.log(den), (ATTN_BLOCK, HEAD_DIM))


def _dilated_attention(qkv):
    dilation, sub_len, _ = qkv.shape
    qt = ATTN_Q_BLOCKS * ATTN_BLOCK

    def cur(part):
        return pl.BlockSpec((None, qt, COL), lambda r, n: (r, n, part))

    def prev(part):
        return pl.BlockSpec((None, ATTN_BLOCK, COL),
                            lambda r, n: (r, jnp.maximum(n * ATTN_Q_BLOCKS - 1, 0), part))

    out_sds = jax.ShapeDtypeStruct((dilation, sub_len, GROUP_WIDTH), F32)
    out_spec = pl.BlockSpec((None, qt, GROUP_WIDTH), lambda r, n: (r, n, 0))
    return pl.pallas_call(
        _attn_kernel,
        out_shape=(out_sds, out_sds),
        grid=(dilation, sub_len // qt),
        in_specs=[cur(0), cur(1), prev(1), cur(2), prev(2)],
        out_specs=(out_spec, out_spec),
        compiler_params=_params(("parallel", "arbitrary")),
        name=f"attn_d{dilation}",
    )(qkv, qkv, qkv, qkv, qkv)


def _mix_kernel(o0_ref, o1_ref, o2_ref, l0_ref, l1_ref, l2_ref, u_ref, uh_ref, ga_ref, gc_ref,
                wao_ref, cw_ref, cb_ref, lg_ref, lb_ref, wco_ref, out_ref, ucat_ref, nat_ref, shift_ref):
    i = pl.program_id(0)
    tm = u_ref.shape[0]

    def natural(ref, slot):
        dilation, rows, _ = ref.shape
        if dilation == 1:
            return ref[0]
        for hh in range(HEADS_PER_GROUP):
            for r in range(dilation):
                nat_ref[slot, hh, pl.ds(r, rows, stride=dilation), :] = ref[r, :, hh * HEAD_DIM:(hh + 1) * HEAD_DIM]
        return jnp.concatenate([nat_ref[slot, hh] for hh in range(HEADS_PER_GROUP)], axis=1)

    l0, l1, l2 = natural(l0_ref, 0), natural(l1_ref, 1), natural(l2_ref, 2)
    m = jnp.maximum(jnp.maximum(l0, l1), l2)
    e0, e1, e2 = jnp.exp(l0 - m), jnp.exp(l1 - m), jnp.exp(l2 - m)
    attn = (e0 * natural(o0_ref, 3) + e1 * natural(o1_ref, 4) + e2 * natural(o2_ref, 5)) / (e0 + e1 + e2)
    attn_o = jnp.dot(attn.astype(BF16), wao_ref[...], preferred_element_type=F32)
    ucat_ref[0:CONV_HALO, :] = jnp.where(i > 0, uh_ref[...], 0.0)
    ucat_ref[CONV_HALO:, :] = u_ref[...]
    off = CONV_HALO - (CONV_WIDTH - 1)
    span = tm + CONV_HALO - SUBLANES
    for s in range(1, SUBLANES):
        shift_ref[s - 1] = ucat_ref[s:s + span, :]
    conv = jnp.zeros(u_ref.shape, F32) + cb_ref[...]
    for w in range(CONV_WIDTH):
        q, s = divmod(off + w, SUBLANES)
        src = ucat_ref if s == 0 else shift_ref.at[s - 1]
        conv = conv + src[q * SUBLANES:q * SUBLANES + tm, :] * cw_ref[w:w + 1, :]
    mu = jnp.mean(conv, axis=-1, keepdims=True)
    cen = conv - mu
    var = jnp.mean(cen * cen, axis=-1, keepdims=True)
    y = cen * lax.rsqrt(var + NORM_EPS) * lg_ref[...] + lb_ref[...]
    y = y * jax.nn.sigmoid(y)
    conv_o = jnp.dot(y.astype(BF16), wco_ref[...], preferred_element_type=F32)
    out_ref[...] = (ga_ref[...] * attn_o + gc_ref[...] * conv_o).astype(out_ref.dtype)


def _mix(outs, lses, u, gates, w_attn_o, conv_w, conv_b, ln_g, ln_b, w_conv_o):
    s, cc = u.shape
    d = w_attn_o.shape[1]
    tm = 256
    row = lambda i: (i, 0)
    const = lambda i: (0, 0)
    cvec = pl.BlockSpec((1, cc), const)

    def grp(arr):
        dilation = arr.shape[0]
        return pl.BlockSpec((dilation, tm // dilation, GROUP_WIDTH), lambda i: (0, i, 0))

    return pl.pallas_call(
        _mix_kernel,
        out_shape=jax.ShapeDtypeStruct((s, d), BF16),
        grid=(s // tm,),
        in_specs=[grp(a) for a in (*outs, *lses)] + [
            pl.BlockSpec((tm, cc), row),
            pl.BlockSpec((CONV_HALO, cc), lambda i: (jnp.maximum(i * (tm // CONV_HALO) - 1, 0), 0)),
            pl.BlockSpec((tm, d), lambda i: (i, 0)),
            pl.BlockSpec((tm, d), lambda i: (i, 1)),
            pl.BlockSpec((GROUP_WIDTH, d), const),
            pl.BlockSpec((CONV_WIDTH, cc), const),
            cvec, cvec, cvec,
            pl.BlockSpec((cc, d), const)],
        out_specs=pl.BlockSpec((tm, d), row),
        scratch_shapes=[pltpu.VMEM((tm + CONV_HALO, cc), F32),
                        pltpu.VMEM((2 * N_GROUPS, HEADS_PER_GROUP, tm, HEAD_DIM), F32),
                        pltpu.VMEM((SUBLANES - 1, tm + CONV_HALO - SUBLANES, cc), F32)],
        compiler_params=_params(("parallel",)),
        name="mix",
    )(*outs, *lses, u, u, gates, gates, w_attn_o, conv_w, conv_b, ln_g, ln_b, w_conv_o)


def _out_proj_kernel(m_ref, w_ref, x_ref, gate_ref, g1_ref, g2_ref, scale_ref, shift_ref,
                     x1_ref, h2t_ref):
    y = jnp.dot(m_ref[...], w_ref[...], preferred_element_type=F32)
    x1 = x_ref[...] + gate_ref[...] * (_rms(y) * g1_ref[...])
    x1_ref[...] = x1
    h2 = (_rms(x1) * g2_ref[...]) * (1.0 + scale_ref[...]) + shift_ref[...]
    h2t_ref[...] = h2.T.astype(h2t_ref.dtype)


def _out_proj(merged, w_out, x, gate1, g1, g2, scale2, shift2):
    s, d = x.shape
    tm = 256
    row = lambda i: (i, 0)
    vec = pl.BlockSpec((1, d), lambda i: (0, 0))
    return pl.pallas_call(
        _out_proj_kernel,
        out_shape=(jax.ShapeDtypeStruct((s, d), F32), jax.ShapeDtypeStruct((d, s), BF16)),
        grid=(s // tm,),
        in_specs=[pl.BlockSpec((tm, d), row), pl.BlockSpec((d, d), lambda i: (0, 0)),
                  pl.BlockSpec((tm, d), row), vec, vec, vec, vec, vec],
        out_specs=(pl.BlockSpec((tm, d), row), pl.BlockSpec((d, tm), lambda i: (0, i))),
        compiler_params=_params(("parallel",)),
        name="out_proj",
    )(merged, w_out, x, gate1, g1, g2, scale2, shift2)


def _odd_even_merge_sort(lo, hi):
    def merge(lo, hi, r):
        step = 2 * r
        if step < hi - lo:
            yield from merge(lo, hi, step)
            yield from merge(lo + r, hi, step)
            for i in range(lo + r, hi - r, step):
                yield (i, i + r)
        else:
            yield (lo, lo + r)

    if hi > lo:
        mid = lo + (hi - lo) // 2
        yield from _odd_even_merge_sort(lo, mid)
        yield from _odd_even_merge_sort(mid + 1, hi)
        yield from merge(lo, hi, 1)


def _bitonic_merge(n):
    half = n // 2
    while half >= 1:
        for i in range(n):
            if (i // half) % 2 == 0:
                yield (i, i + half)
        half //= 2


_SORT_PAIRS = {n: tuple(_odd_even_merge_sort(0, n - 1)) for n in (PEER_TOPK // 2, PEER_TOPK)}
_MERGE_PAIRS = tuple(_bitonic_merge(PEER_TOPK))


def _compare_exchange(v, pairs):
    for i, j in pairs:
        v[i], v[j] = jnp.maximum(v[i], v[j]), jnp.minimum(v[i], v[j])


def _top16_rows(x):
    blocks = x.shape[0] // SUBLANES
    assert x.shape[0] == blocks * SUBLANES and blocks <= PEER_TOPK
    n = PEER_TOPK // 2 if blocks <= PEER_TOPK // 2 else PEER_TOPK
    v = [x[k * SUBLANES:(k + 1) * SUBLANES, :] for k in range(blocks)]
    v += [jnp.full_like(v[0], NEG_INF)] * (n - blocks)
    _compare_exchange(v, _SORT_PAIRS[n])
    shift = SUBLANES // 2
    while shift >= 1:
        partner = [pltpu.roll(blk, shift, axis=0) for blk in v]
        if len(v) < PEER_TOPK:
            v = v + partner[::-1]
        else:
            v = [jnp.maximum(v[k], partner[PEER_TOPK - 1 - k]) for k in range(PEER_TOPK)]
        _compare_exchange(v, _MERGE_PAIRS)
        shift //= 2
    return [blk[0:1, :] for blk in v]


def _peer_query_kernel(h2t_ref, wqt_ref, keys_ref, e1_ref, e2_ref, theta_ref,
                       qt_ref, top_ref, cand_ref):
    qt_ref[...] = jnp.dot(wqt_ref[...], h2t_ref[...], preferred_element_type=F32)
    for h in range(PEER_HEADS):
        u = []
        for side in range(2):
            r0 = (2 * h + side) * PEER_HALF
            qc = qt_ref[r0:r0 + PEER_HALF, :].astype(BF16)
            sc = jnp.dot(keys_ref[h, side].astype(BF16), qc, preferred_element_type=F32)
            us = jnp.exp(sc - jnp.max(sc, axis=0, keepdims=True))
            vals = _top16_rows(us)
            for k in range(PEER_TOPK):
                top_ref[side, k:k + 1, :] = vals[k]
            u.append(us)
        cand_ref[...] = jnp.full(cand_ref.shape, -1.0, F32)
        off = 0
        for a, nb in enumerate(_CAND_ROWS):
            cand_ref[off:off + nb, :] = top_ref[0, a:a + 1, :] * top_ref[1, 0:nb, :]
            off += nb
        cand = cand_ref[...]
        selected = cand >= _top16_rows(cand)[-1]
        inv_z = 1.0 / jnp.sum(jnp.where(selected, cand, 0.0), axis=0, keepdims=True)
        off = 0
        for a, nb in enumerate(_CAND_ROWS):
            cand_ref[off:off + nb, :] = (top_ref[0, a:a + 1, :] * inv_z) * top_ref[1, 0:nb, :]
            off += nb
        theta = jnp.min(jnp.where(selected, cand_ref[...], jnp.inf), axis=0, keepdims=True)
        e1 = u[0] * inv_z
        for blk in range(PEER_N_KEYS // 8):
            e1_ref[h, blk] = e1[blk * 8:(blk + 1) * 8, :]
        e2_ref[h] = u[1]
        theta_ref[h:h + 1, :] = theta


def _peer_query(h2t, wq_t, sub_keys):
    d, s = h2t.shape
    tt = 512
    nq = wq_t.shape[0]
    side_blocked = jax.ShapeDtypeStruct((PEER_HEADS, PEER_N_KEYS // 8, 8, s), F32)
    side_flat = jax.ShapeDtypeStruct((PEER_HEADS, PEER_N_KEYS, s), F32)
    blocked_spec = pl.BlockSpec((PEER_HEADS, PEER_N_KEYS // 8, 8, tt), lambda t: (0, 0, 0, t))
    flat_spec = pl.BlockSpec((PEER_HEADS, PEER_N_KEYS, tt), lambda t: (0, 0, t))
    return pl.pallas_call(
        _peer_query_kernel,
        out_shape=(side_blocked, side_flat, jax.ShapeDtypeStruct((PEER_HEADS, s), F32)),
        grid=(s // tt,),
        in_specs=[pl.BlockSpec((d, tt), lambda t: (0, t)),
                  pl.BlockSpec((nq, d), lambda t: (0, 0)),
                  pl.BlockSpec(sub_keys.shape, lambda t: (0, 0, 0, 0))],
        out_specs=(blocked_spec, flat_spec, pl.BlockSpec((PEER_HEADS, tt), lambda t: (0, t))),
        scratch_shapes=[pltpu.VMEM((nq, tt), F32),
                        pltpu.VMEM((2, PEER_TOPK, tt), F32),
                        pltpu.VMEM((_CAND_PAD, tt), F32)],
        compiler_params=_params(("parallel",)),
        name="peer_query",
    )(h2t, wq_t, sub_keys)


PEER_LANE_CHUNK = 128
PEER_HALF_KEYS = 4
PEER_HALF_TILE = PEER_HALF_KEYS * PEER_N_KEYS
PEER_TILE = 2 * PEER_HALF_TILE
PEER_TOKENS = 512


def _gate_act(pre_ref, act_ref, e1_ref, e2_ref, theta_ref, half):
    tt = pre_ref.shape[2]
    for ii in range(PEER_HALF_KEYS):
        rows = slice(ii * PEER_N_KEYS, (ii + 1) * PEER_N_KEYS)
        key = half * PEER_HALF_KEYS + ii
        for c in range(tt // PEER_LANE_CHUNK):
            lanes = slice(c * PEER_LANE_CHUNK, (c + 1) * PEER_LANE_CHUNK)
            w = jnp.zeros((PEER_N_KEYS, PEER_LANE_CHUNK), F32)
            for h in range(PEER_HEADS):
                p = e1_ref[h, 0, key:key + 1, lanes] * e2_ref[h, :, lanes]
                w = w + jnp.where(p >= theta_ref[h:h + 1, lanes], p, 0.0)
            pre = pre_ref[half, rows, lanes]
            act = (0.5 * pre) * (1.0 + lax.erf(pre * INV_SQRT2)) * w
            act_ref[half, rows, lanes] = act.astype(act_ref.dtype)


def _peer_project_kernel(down_ref, h2t_ref, pre_ref):
    for half in range(2):
        rows = slice(half * PEER_HALF_TILE, (half + 1) * PEER_HALF_TILE)
        pre_ref[half] = jnp.dot(down_ref[rows, :], h2t_ref[...], preferred_element_type=F32)


def _peer_project_first(h2t, down):
    d = h2t.shape[0]
    return pl.pallas_call(
        _peer_project_kernel,
        out_shape=jax.ShapeDtypeStruct((2, PEER_HALF_TILE, PEER_TOKENS), F32),
        grid=(1,),
        in_specs=[pl.BlockSpec((PEER_TILE, d), lambda i: (0, 0)),
                  pl.BlockSpec((d, PEER_TOKENS), lambda i: (0, 0))],
        out_specs=pl.BlockSpec((2, PEER_HALF_TILE, PEER_TOKENS), lambda i: (0, 0, 0)),
        compiler_params=_params(("arbitrary",)),
        name="peer_project_first",
    )(down, h2t)


def _peer_dense_kernel(h2t_ref, down_ref, up_ref, e1_ref, e2_ref, theta_ref, pre0_ref,
                       x1_ref, gate_ref, g_ref, out_ref, pre_ref, act_ref):
    t = pl.program_id(0)
    e = pl.program_id(1)

    @pl.when((t == 0) & (e == 0))
    def _():
        pre_ref[...] = pre0_ref[...]

    @pl.when(e == 0)
    def _():
        out_ref[...] = jnp.zeros(out_ref.shape, F32)

    for half in range(2):
        rows = slice(half * PEER_HALF_TILE, (half + 1) * PEER_HALF_TILE)
        _gate_act(pre_ref, act_ref, e1_ref, e2_ref, theta_ref, half)
        out_ref[...] += lax.dot_general(act_ref[half], up_ref[rows, :], (((0,), (0,)), ((), ())),
                                        preferred_element_type=F32)
        pre_ref[half] = jnp.dot(down_ref[rows, :], h2t_ref[...], preferred_element_type=F32)

    @pl.when(e == pl.num_programs(1) - 1)
    def _():
        y = out_ref[...]
        out_ref[...] = x1_ref[...] + gate_ref[...] * (_rms(y) * g_ref[...])


def _peer_dense(h2t, down, up, e1, e2, theta, x1, gate2, g3):
    d, s = h2t.shape
    tt = PEER_TOKENS
    n_e = down.shape[0] // PEER_TILE
    n_t = s // tt
    vec = pl.BlockSpec((1, d), lambda t, e: (0, 0))
    next_e = lambda e: (e + 1) % n_e
    next_t = lambda t, e: jnp.minimum(t + (e + 1) // n_e, n_t - 1)
    return pl.pallas_call(
        _peer_dense_kernel,
        out_shape=jax.ShapeDtypeStruct((s, d), F32),
        grid=(n_t, n_e),
        in_specs=[pl.BlockSpec((d, tt), lambda t, e: (0, next_t(t, e))),
                  pl.BlockSpec((PEER_TILE, d), lambda t, e: (next_e(e), 0)),
                  pl.BlockSpec((PEER_TILE, d), lambda t, e: (e, 0)),
                  pl.BlockSpec((PEER_HEADS, 1, 2 * PEER_HALF_KEYS, tt), lambda t, e: (0, e, 0, t)),
                  pl.BlockSpec((PEER_HEADS, PEER_N_KEYS, tt), lambda t, e: (0, 0, t)),
                  pl.BlockSpec((PEER_HEADS, tt), lambda t, e: (0, t)),
                  pl.BlockSpec((2, PEER_HALF_TILE, tt), lambda t, e: (0, 0, 0)),
                  pl.BlockSpec((tt, d), lambda t, e: (t, 0)),
                  vec, vec],
        out_specs=pl.BlockSpec((tt, d), lambda t, e: (t, 0)),
        scratch_shapes=[pltpu.VMEM((2, PEER_HALF_TILE, tt), F32),
                        pltpu.VMEM((2, PEER_HALF_TILE, tt), BF16)],
        compiler_params=_params(("arbitrary", "arbitrary")),
        name="peer_dense",
    )(h2t, down, up, e1, e2, theta, _peer_project_first(h2t, down), x1, gate2, g3)


def kernel(x, c, positions, ada_w, ada_b, norm_gains, w_in, w_attn_o, conv_w, conv_b, conv_ln_g,
           conv_ln_b, w_conv_o, w_out, peer_w_q, peer_sub_keys, peer_down, peer_up):
    batch, seq, d = x.shape
    depth = ada_w.shape[0]
    assert batch == 1, "kernels are written for a single sequence"
    assert all(window // dilation == ATTN_BLOCK for window, dilation in DILATED_GROUPS)
    cc = conv_w.shape[-1]
    xs = x[0]
    inv_freq = ROPE_THETA ** (-jnp.arange(0, HEAD_DIM, 2, dtype=F32) / HEAD_DIM)
    invf_row = jnp.concatenate([inv_freq, inv_freq])[None, :]
    cos, sin = _rope_tables(positions[0].astype(F32)[:, None], invf_row)
    row = lambda v: v[None, :]
    for l in range(depth):
        mod = _modulation(c[0][:, None], ada_w[l], ada_b[l][None, :])
        shift1, scale1, gate1, shift2, scale2, gate2 = [mod[:, k * d:(k + 1) * d] for k in range(6)]
        g = norm_gains[l]
        w_in_b = w_in[l].astype(BF16)

        h = _norm_mod(xs, row(g[0]), scale1, shift1)
        u = _glu_proj(h, w_in_b, cc)
        gates = _gates_proj(h, w_in_b, 3 * ATTN_WIDTH + 2 * cc, 2 * d)
        outs, lses = [], []
        for gi, (_, dilation) in enumerate(DILATED_GROUPS):
            o, lse = _dilated_attention(_qkv_proj(h, w_in_b, cos, sin, gi, dilation))
            outs.append(o)
            lses.append(lse)
        merged = _mix(outs, lses, u, gates, w_attn_o[l].astype(BF16), conv_w[l][:, 0, :],
                      row(conv_b[l]), row(conv_ln_g[l]), row(conv_ln_b[l]), w_conv_o[l].astype(BF16))
        x1, h2t = _out_proj(merged, w_out[l].astype(BF16), xs, gate1, row(g[1]), row(g[2]),
                            scale2, shift2)

        e1, e2, theta = _peer_query(h2t, peer_w_q[l].T.astype(BF16), peer_sub_keys[l])
        xs = _peer_dense(h2t, peer_down[l].astype(BF16), peer_up[l].astype(BF16),
                         e1, e2, theta, x1, gate2, row(g[3]))
    return xs[None]
```

```python
import functools
import math

import jax
import jax.numpy as jnp
from jax import lax
from jax.experimental import pallas as pl
from jax.experimental.pallas import tpu as pltpu

F32 = jnp.float32
BF16 = jnp.bfloat16

HEAD_DIM = 128
HEADS_PER_GROUP = 4
DILATED_GROUPS = ((128, 1), (512, 4), (2048, 16))
N_GROUPS = len(DILATED_GROUPS)
GROUP_WIDTH = HEADS_PER_GROUP * HEAD_DIM
ATTN_WIDTH = N_GROUPS * GROUP_WIDTH
ATTN_BLOCK = 128
ROPE_THETA = 10000.0
CONV_WIDTH = 31
CONV_HALO = 32
PEER_HEADS = 8
PEER_N_KEYS = 128
PEER_HALF = 128
PEER_TOPK = 16
NORM_EPS = 1e-6
MASK_VALUE = -1e30
NEG_INF = float("-inf")
INV_SQRT2 = 1.0 / math.sqrt(2.0)

SUBLANES = 8
COL = 512
VMEM_LIMIT = 56 * 1024 * 1024

_CAND_ROWS = [PEER_TOPK // (a + 1) for a in range(PEER_TOPK)]
_N_CAND = sum(_CAND_ROWS)
_CAND_PAD = -(-_N_CAND // 8) * 8


def _params(sem):
    return pltpu.CompilerParams(dimension_semantics=sem, vmem_limit_bytes=VMEM_LIMIT)


def _rms(x):
    return x * lax.rsqrt(jnp.mean(x * x, axis=-1, keepdims=True) + NORM_EPS)


def _mod_kernel(c_ref, w_ref, b_ref, o_ref):
    rows = 256

    def body(k, acc):
        r = pl.multiple_of(k * rows, rows)
        c = c_ref[pl.ds(r, rows), :]
        sc = c * jax.nn.sigmoid(c)
        return acc + jnp.sum(w_ref[pl.ds(r, rows), :] * sc, axis=0, keepdims=True)

    acc = lax.fori_loop(0, w_ref.shape[0] // rows, body, jnp.zeros(o_ref.shape, F32))
    o_ref[...] = acc + b_ref[...]


def _modulation(c_col, w, b):
    d, n = w.shape
    tn = 512
    return pl.pallas_call(
        _mod_kernel,
        out_shape=jax.ShapeDtypeStruct((1, n), F32),
        grid=(n // tn,),
        in_specs=[pl.BlockSpec((d, 1), lambda j: (0, 0)),
                  pl.BlockSpec((d, tn), lambda j: (0, j)),
                  pl.BlockSpec((1, tn), lambda j: (0, j))],
        out_specs=pl.BlockSpec((1, tn), lambda j: (0, j)),
        compiler_params=_params(("parallel",)),
        name="mod",
    )(c_col, w, b)


def _rope_kernel(pos_ref, invf_ref, cos_ref, sin_ref):
    ang = pos_ref[...] * invf_ref[...]
    cos_ref[...] = jnp.cos(ang)
    s = jnp.sin(ang)
    lane = lax.broadcasted_iota(jnp.int32, s.shape, 1)
    sin_ref[...] = jnp.where(lane < HEAD_DIM // 2, -s, s)


def _rope_tables(pos_col, invf_row):
    s = pos_col.shape[0]
    ts = 2048
    return pl.pallas_call(
        _rope_kernel,
        out_shape=(jax.ShapeDtypeStruct((s, HEAD_DIM), F32),) * 2,
        grid=(s // ts,),
        in_specs=[pl.BlockSpec((ts, 1), lambda i: (i, 0)),
                  pl.BlockSpec((1, HEAD_DIM), lambda i: (0, 0))],
        out_specs=(pl.BlockSpec((ts, HEAD_DIM), lambda i: (i, 0)),) * 2,
        compiler_params=_params(("parallel",)),
        name="rope",
    )(pos_col, invf_row)


def _norm_mod_kernel(x_ref, g_ref, scale_ref, shift_ref, o_ref):
    y = _rms(x_ref[...]) * g_ref[...]
    o_ref[...] = (y * (1.0 + scale_ref[...]) + shift_ref[...]).astype(o_ref.dtype)


def _norm_mod(x, g, scale, shift):
    s, d = x.shape
    tm = 512
    vec = pl.BlockSpec((1, d), lambda i: (0, 0))
    return pl.pallas_call(
        _norm_mod_kernel,
        out_shape=jax.ShapeDtypeStruct((s, d), BF16),
        grid=(s // tm,),
        in_specs=[pl.BlockSpec((tm, d), lambda i: (i, 0)), vec, vec, vec],
        out_specs=pl.BlockSpec((tm, d), lambda i: (i, 0)),
        compiler_params=_params(("parallel",)),
        name="norm_mod",
    )(x, g, scale, shift)


def _qkv_kernel(h_ref, w_ref, cos_ref, sin_ref, o_ref, stage_ref):
    acc = jnp.dot(h_ref[...], w_ref[...], preferred_element_type=F32)
    part = pl.program_id(1)

    heads = [slice(hh * HEAD_DIM, (hh + 1) * HEAD_DIM) for hh in range(HEADS_PER_GROUP)]

    @pl.when(part < 2)
    def _():
        cos = cos_ref[...]
        sin = sin_ref[...]
        for hh, cols in enumerate(heads):
            t = acc[:, cols]
            stage_ref[hh] = t * cos + pltpu.roll(t, HEAD_DIM // 2, axis=1) * sin

    @pl.when(part == 2)
    def _():
        for hh, cols in enumerate(heads):
            stage_ref[hh] = acc[:, cols]

    dilation, rows, _ = o_ref.shape
    for r in range(dilation):
        for hh, cols in enumerate(heads):
            o_ref[r, :, cols] = stage_ref[hh, pl.ds(r, rows, stride=dilation), :].astype(o_ref.dtype)


def _qkv_proj(h, w_in, cos, sin, group, dilation):
    s, d = h.shape
    tm = 1024
    return pl.pallas_call(
        _qkv_kernel,
        out_shape=jax.ShapeDtypeStruct((dilation, s // dilation, 3 * GROUP_WIDTH), BF16),
        grid=(s // tm, 3),
        in_specs=[pl.BlockSpec((tm, d), lambda i, j: (i, 0)),
                  pl.BlockSpec((d, COL), lambda i, j: (0, j * N_GROUPS + group)),
                  pl.BlockSpec((tm, HEAD_DIM), lambda i, j: (i, 0)),
                  pl.BlockSpec((tm, HEAD_DIM), lambda i, j: (i, 0))],
        out_specs=pl.BlockSpec((dilation, tm // dilation, COL), lambda i, j: (0, i, j)),
        scratch_shapes=[pltpu.VMEM((HEADS_PER_GROUP, tm, HEAD_DIM), F32)],
        compiler_params=_params(("parallel", "arbitrary")),
        name=f"qkv_d{dilation}",
    )(h, w_in, cos, sin)


def _glu_kernel(h_ref, wa_ref, wb_ref, o_ref):
    h = h_ref[...]
    a = jnp.dot(h, wa_ref[...], preferred_element_type=F32)
    b = jnp.dot(h, wb_ref[...], preferred_element_type=F32)
    o_ref[...] = a * jax.nn.sigmoid(b)


def _glu_proj(h, w_in, conv_channels):
    s, d = h.shape
    tm = 1024
    a0 = 3 * ATTN_WIDTH // COL
    nb = conv_channels // COL
    return pl.pallas_call(
        _glu_kernel,
        out_shape=jax.ShapeDtypeStruct((s, conv_channels), F32),
        grid=(s // tm, nb),
        in_specs=[pl.BlockSpec((tm, d), lambda i, j: (i, 0)),
                  pl.BlockSpec((d, COL), lambda i, j: (0, a0 + j)),
                  pl.BlockSpec((d, COL), lambda i, j: (0, a0 + nb + j))],
        out_specs=pl.BlockSpec((tm, COL), lambda i, j: (i, j)),
        compiler_params=_params(("parallel", "arbitrary")),
        name="glu",
    )(h, w_in, w_in)


def _gates_kernel(h_ref, w_ref, o_ref):
    o_ref[...] = jax.nn.sigmoid(jnp.dot(h_ref[...], w_ref[...], preferred_element_type=F32))


def _gates_proj(h, w_in, col0, width):
    s, d = h.shape
    tm = 1024
    c0 = col0 // COL
    return pl.pallas_call(
        _gates_kernel,
        out_shape=jax.ShapeDtypeStruct((s, width), F32),
        grid=(s // tm, width // COL),
        in_specs=[pl.BlockSpec((tm, d), lambda i, j: (i, 0)),
                  pl.BlockSpec((d, COL), lambda i, j: (0, c0 + j))],
        out_specs=pl.BlockSpec((tm, COL), lambda i, j: (i, j)),
        compiler_params=_params(("parallel", "arbitrary")),
        name="gates",
    )(h, w_in)


ATTN_Q_BLOCKS = 4


def _attn_kernel(q_ref, kc_ref, kp_ref, vc_ref, vp_ref, o_ref, l_ref):
    n = pl.program_id(1)
    qi = lax.broadcasted_iota(jnp.int32, (ATTN_BLOCK, 2 * ATTN_BLOCK), 0)
    kj = lax.broadcasted_iota(jnp.int32, (ATTN_BLOCK, 2 * ATTN_BLOCK), 1)
    band = jnp.abs(kj - qi - ATTN_BLOCK // 2) <= ATTN_BLOCK // 2
    first_lo = jnp.where(n == 0, ATTN_BLOCK, 0)
    band_first = band & (kj >= first_lo)
    scale = HEAD_DIM ** -0.5
    for hh in range(HEADS_PER_GROUP):
        cols = slice(hh * HEAD_DIM, (hh + 1) * HEAD_DIM)
        for b in range(ATTN_Q_BLOCKS):
            rows = slice(b * ATTN_BLOCK, (b + 1) * ATTN_BLOCK)
            q = q_ref[rows, cols]
            if b == 0:
                k = jnp.concatenate([kp_ref[:, cols], kc_ref[0:ATTN_BLOCK, cols]], axis=0)
                v = jnp.concatenate([vp_ref[:, cols], vc_ref[0:ATTN_BLOCK, cols]], axis=0)
                valid = band_first
            else:
                band_rows = slice((b - 1) * ATTN_BLOCK, (b + 1) * ATTN_BLOCK)
                k = kc_ref[band_rows, cols]
                v = vc_ref[band_rows, cols]
                valid = band
            s = lax.dot_general(q, k, (((1,), (1,)), ((), ())), preferred_element_type=F32) * scale
            s = jnp.where(valid, s, MASK_VALUE)
            m = jnp.max(s, axis=-1, keepdims=True)
            p = jnp.exp(s - m)
            den = jnp.sum(p, axis=-1, keepdims=True)
            o = jnp.dot(p.astype(BF16), v, preferred_element_type=F32)
            o_ref[rows, cols] = o * (1.0 / den)
            l_ref[rows, cols] = jnp.broadcast_to(m + jnp.log(den), (ATTN_BLOCK, HEAD_DIM))


def _dilated_attention(qkv):
    dilation, sub_len, _ = qkv.shape
    qt = ATTN_Q_BLOCKS * ATTN_BLOCK

    def cur(part):
        return pl.BlockSpec((None, qt, COL), lambda r, n: (r, n, part))

    def prev(part):
        return pl.BlockSpec((None, ATTN_BLOCK, COL),
                            lambda r, n: (r, jnp.maximum(n * ATTN_Q_BLOCKS - 1, 0), part))

    out_sds = jax.ShapeDtypeStruct((dilation, sub_len, GROUP_WIDTH), F32)
    out_spec = pl.BlockSpec((None, qt, GROUP_WIDTH), lambda r, n: (r, n, 0))
    return pl.pallas_call(
        _attn_kernel,
        out_shape=(out_sds, out_sds),
        grid=(dilation, sub_len // qt),
        in_specs=[cur(0), cur(1), prev(1), cur(2), prev(2)],
        out_specs=(out_spec, out_spec),
        compiler_params=_params(("parallel", "arbitrary")),
        name=f"attn_d{dilation}",
    )(qkv, qkv, qkv, qkv, qkv)


def _mix_kernel(o0_ref, o1_ref, o2_ref, l0_ref, l1_ref, l2_ref, u_ref, uh_ref, ga_ref, gc_ref,
                wao_ref, cw_ref, cb_ref, lg_ref, lb_ref, wco_ref, out_ref, ucat_ref, nat_ref, shift_ref):
    i = pl.program_id(0)
    tm = u_ref.shape[0]

    def natural(ref, slot):
        dilation, rows, _ = ref.shape
        if dilation == 1:
            return ref[0]
        for hh in range(HEADS_PER_GROUP):
            for r in range(dilation):
                nat_ref[slot, hh, pl.ds(r, rows, stride=dilation), :] = ref[r, :, hh * HEAD_DIM:(hh + 1) * HEAD_DIM]
        return jnp.concatenate([nat_ref[slot, hh] for hh in range(HEADS_PER_GROUP)], axis=1)

    l0, l1, l2 = natural(l0_ref, 0), natural(l1_ref, 1), natural(l2_ref, 2)
    m = jnp.maximum(jnp.maximum(l0, l1), l2)
    e0, e1, e2 = jnp.exp(l0 - m), jnp.exp(l1 - m), jnp.exp(l2 - m)
    attn = (e0 * natural(o0_ref, 3) + e1 * natural(o1_ref, 4) + e2 * natural(o2_ref, 5)) / (e0 + e1 + e2)
    attn_o = jnp.dot(attn.astype(BF16), wao_ref[...], preferred_element_type=F32)
    ucat_ref[0:CONV_HALO, :] = jnp.where(i > 0, uh_ref[...], 0.0)
    ucat_ref[CONV_HALO:, :] = u_ref[...]
    off = CONV_HALO - (CONV_WIDTH - 1)
    span = tm + CONV_HALO - SUBLANES
    for s in range(1, SUBLANES):
        shift_ref[s - 1] = ucat_ref[s:s + span, :]
    conv = jnp.zeros(u_ref.shape, F32) + cb_ref[...]
    for w in range(CONV_WIDTH):
        q, s = divmod(off + w, SUBLANES)
        src = ucat_ref if s == 0 else shift_ref.at[s - 1]
        conv = conv + src[q * SUBLANES:q * SUBLANES + tm, :] * cw_ref[w:w + 1, :]
    mu = jnp.mean(conv, axis=-1, keepdims=True)
    cen = conv - mu
    var = jnp.mean(cen * cen, axis=-1, keepdims=True)
    y = cen * lax.rsqrt(var + NORM_EPS) * lg_ref[...] + lb_ref[...]
    y = y * jax.nn.sigmoid(y)
    conv_o = jnp.dot(y.astype(BF16), wco_ref[...], preferred_element_type=F32)
    out_ref[...] = (ga_ref[...] * attn_o + gc_ref[...] * conv_o).astype(out_ref.dtype)


def _mix(outs, lses, u, gates, w_attn_o, conv_w, conv_b, ln_g, ln_b, w_conv_o):
    s, cc = u.shape
    d = w_attn_o.shape[1]
    tm = 256
    row = lambda i: (i, 0)
    const = lambda i: (0, 0)
    cvec = pl.BlockSpec((1, cc), const)

    def grp(arr):
        dilation = arr.shape[0]
        return pl.BlockSpec((dilation, tm // dilation, GROUP_WIDTH), lambda i: (0, i, 0))

    return pl.pallas_call(
        _mix_kernel,
        out_shape=jax.ShapeDtypeStruct((s, d), BF16),
        grid=(s // tm,),
        in_specs=[grp(a) for a in (*outs, *lses)] + [
            pl.BlockSpec((tm, cc), row),
            pl.BlockSpec((CONV_HALO, cc), lambda i: (jnp.maximum(i * (tm // CONV_HALO) - 1, 0), 0)),
            pl.BlockSpec((tm, d), lambda i: (i, 0)),
            pl.BlockSpec((tm, d), lambda i: (i, 1)),
            pl.BlockSpec((GROUP_WIDTH, d), const),
            pl.BlockSpec((CONV_WIDTH, cc), const),
            cvec, cvec, cvec,
            pl.BlockSpec((cc, d), const)],
        out_specs=pl.BlockSpec((tm, d), row),
        scratch_shapes=[pltpu.VMEM((tm + CONV_HALO, cc), F32),
                        pltpu.VMEM((2 * N_GROUPS, HEADS_PER_GROUP, tm, HEAD_DIM), F32),
                        pltpu.VMEM((SUBLANES - 1, tm + CONV_HALO - SUBLANES, cc), F32)],
        compiler_params=_params(("parallel",)),
        name="mix",
    )(*outs, *lses, u, u, gates, gates, w_attn_o, conv_w, conv_b, ln_g, ln_b, w_conv_o)


def _out_proj_kernel(m_ref, w_ref, x_ref, gate_ref, g1_ref, g2_ref, scale_ref, shift_ref,
                     x1_ref, h2t_ref):
    y = jnp.dot(m_ref[...], w_ref[...], preferred_element_type=F32)
    x1 = x_ref[...] + gate_ref[...] * (_rms(y) * g1_ref[...])
    x1_ref[...] = x1
    h2 = (_rms(x1) * g2_ref[...]) * (1.0 + scale_ref[...]) + shift_ref[...]
    h2t_ref[...] = h2.T.astype(h2t_ref.dtype)


def _out_proj(merged, w_out, x, gate1, g1, g2, scale2, shift2):
    s, d = x.shape
    tm = 256
    row = lambda i: (i, 0)
    vec = pl.BlockSpec((1, d), lambda i: (0, 0))
    return pl.pallas_call(
        _out_proj_kernel,
        out_shape=(jax.ShapeDtypeStruct((s, d), F32), jax.ShapeDtypeStruct((d, s), BF16)),
        grid=(s // tm,),
        in_specs=[pl.BlockSpec((tm, d), row), pl.BlockSpec((d, d), lambda i: (0, 0)),
                  pl.BlockSpec((tm, d), row), vec, vec, vec, vec, vec],
        out_specs=(pl.BlockSpec((tm, d), row), pl.BlockSpec((d, tm), lambda i: (0, i))),
        compiler_params=_params(("parallel",)),
        name="out_proj",
    )(merged, w_out, x, gate1, g1, g2, scale2, shift2)


def _odd_even_merge_sort(lo, hi):
    def merge(lo, hi, r):
        step = 2 * r
        if step < hi - lo:
            yield from merge(lo, hi, step)
            yield from merge(lo + r, hi, step)
            for i in range(lo + r, hi - r, step):
                yield (i, i + r)
        else:
            yield (lo, lo + r)

    if hi > lo:
        mid = lo + (hi - lo) // 2
        yield from _odd_even_merge_sort(lo, mid)
        yield from _odd_even_merge_sort(mid + 1, hi)
        yield from merge(lo, hi, 1)


def _bitonic_merge(n):
    half = n // 2
    while half >= 1:
        for i in range(n):
            if (i // half) % 2 == 0:
                yield (i, i + half)
        half //= 2


_SORT_PAIRS = {n: tuple(_odd_even_merge_sort(0, n - 1)) for n in (PEER_TOPK // 2, PEER_TOPK)}
_MERGE_PAIRS = tuple(_bitonic_merge(PEER_TOPK))


def _compare_exchange(v, pairs):
    for i, j in pairs:
        v[i], v[j] = jnp.maximum(v[i], v[j]), jnp.minimum(v[i], v[j])


def _top16_rows(x):
    blocks = x.shape[0] // SUBLANES
    assert x.shape[0] == blocks * SUBLANES and blocks <= PEER_TOPK
    n = PEER_TOPK // 2 if blocks <= PEER_TOPK // 2 else PEER_TOPK
    v = [x[k * SUBLANES:(k + 1) * SUBLANES, :] for k in range(blocks)]
    v += [jnp.full_like(v[0], NEG_INF)] * (n - blocks)
    _compare_exchange(v, _SORT_PAIRS[n])
    shift = SUBLANES // 2
    while shift >= 1:
        partner = [pltpu.roll(blk, shift, axis=0) for blk in v]
        if len(v) < PEER_TOPK:
            v = v + partner[::-1]
        else:
            v = [jnp.maximum(v[k], partner[PEER_TOPK - 1 - k]) for k in range(PEER_TOPK)]
        _compare_exchange(v, _MERGE_PAIRS)
        shift //= 2
    return [blk[0:1, :] for blk in v]


def _peer_query_kernel(h2t_ref, wqt_ref, keys_ref, e1_ref, e2_ref, theta_ref,
                       qt_ref, top_ref, cand_ref):
    qt_ref[...] = jnp.dot(wqt_ref[...], h2t_ref[...], preferred_element_type=F32)
    for h in range(PEER_HEADS):
        u = []
        for side in range(2):
            r0 = (2 * h + side) * PEER_HALF
            qc = qt_ref[r0:r0 + PEER_HALF, :].astype(BF16)
            sc = jnp.dot(keys_ref[h, side].astype(BF16), qc, preferred_element_type=F32)
            us = jnp.exp(sc - jnp.max(sc, axis=0, keepdims=True))
            vals = _top16_rows(us)
            for k in range(PEER_TOPK):
                top_ref[side, k:k + 1, :] = vals[k]
            u.append(us)
        cand_ref[...] = jnp.full(cand_ref.shape, -1.0, F32)
        off = 0
        for a, nb in enumerate(_CAND_ROWS):
            cand_ref[off:off + nb, :] = top_ref[0, a:a + 1, :] * top_ref[1, 0:nb, :]
            off += nb
        cand = cand_ref[...]
        selected = cand >= _top16_rows(cand)[-1]
        inv_z = 1.0 / jnp.sum(jnp.where(selected, cand, 0.0), axis=0, keepdims=True)
        off = 0
        for a, nb in enumerate(_CAND_ROWS):
            cand_ref[off:off + nb, :] = (top_ref[0, a:a + 1, :] * inv_z) * top_ref[1, 0:nb, :]
            off += nb
        theta = jnp.min(jnp.where(selected, cand_ref[...], jnp.inf), axis=0, keepdims=True)
        e1 = u[0] * inv_z
        for blk in range(PEER_N_KEYS // 8):
            e1_ref[h, blk] = e1[blk * 8:(blk + 1) * 8, :]
        e2_ref[h] = u[1]
        theta_ref[h:h + 1, :] = theta


def _peer_query(h2t, wq_t, sub_keys):
    d, s = h2t.shape
    tt = 512
    nq = wq_t.shape[0]
    side_blocked = jax.ShapeDtypeStruct((PEER_HEADS, PEER_N_KEYS // 8, 8, s), F32)
    side_flat = jax.ShapeDtypeStruct((PEER_HEADS, PEER_N_KEYS, s), F32)
    blocked_spec = pl.BlockSpec((PEER_HEADS, PEER_N_KEYS // 8, 8, tt), lambda t: (0, 0, 0, t))
    flat_spec = pl.BlockSpec((PEER_HEADS, PEER_N_KEYS, tt), lambda t: (0, 0, t))
    return pl.pallas_call(
        _peer_query_kernel,
        out_shape=(side_blocked, side_flat, jax.ShapeDtypeStruct((PEER_HEADS, s), F32)),
        grid=(s // tt,),
        in_specs=[pl.BlockSpec((d, tt), lambda t: (0, t)),
                  pl.BlockSpec((nq, d), lambda t: (0, 0)),
                  pl.BlockSpec(sub_keys.shape, lambda t: (0, 0, 0, 0))],
        out_specs=(blocked_spec, flat_spec, pl.BlockSpec((PEER_HEADS, tt), lambda t: (0, t))),
        scratch_shapes=[pltpu.VMEM((nq, tt), F32),
                        pltpu.VMEM((2, PEER_TOPK, tt), F32),
                        pltpu.VMEM((_CAND_PAD, tt), F32)],
        compiler_params=_params(("parallel",)),
        name="peer_query",
    )(h2t, wq_t, sub_keys)


PEER_LANE_CHUNK = 128
PEER_HALF_KEYS = 4
PEER_HALF_TILE = PEER_HALF_KEYS * PEER_N_KEYS
PEER_TILE = 2 * PEER_HALF_TILE
PEER_TOKENS = 512


def _gate_act(pre_ref, act_ref, e1_ref, e2_ref, theta_ref, half):
    tt = pre_ref.shape[2]
    for ii in range(PEER_HALF_KEYS):
        rows = slice(ii * PEER_N_KEYS, (ii + 1) * PEER_N_KEYS)
        key = half * PEER_HALF_KEYS + ii
        for c in range(tt // PEER_LANE_CHUNK):
            lanes = slice(c * PEER_LANE_CHUNK, (c + 1) * PEER_LANE_CHUNK)
            w = jnp.zeros((PEER_N_KEYS, PEER_LANE_CHUNK), F32)
            for h in range(PEER_HEADS):
                p = e1_ref[h, 0, key:key + 1, lanes] * e2_ref[h, :, lanes]
                w = w + jnp.where(p >= theta_ref[h:h + 1, lanes], p, 0.0)
            pre = pre_ref[half, rows, lanes]
            act = (0.5 * pre) * (1.0 + lax.erf(pre * INV_SQRT2)) * w
            act_ref[half, rows, lanes] = act.astype(act_ref.dtype)


def _peer_project_kernel(down_ref, h2t_ref, pre_ref):
    for half in range(2):
        rows = slice(half * PEER_HALF_TILE, (half + 1) * PEER_HALF_TILE)
        pre_ref[half] = jnp.dot(down_ref[rows, :], h2t_ref[...], preferred_element_type=F32)


def _peer_project_first(h2t, down):
    d = h2t.shape[0]
    return pl.pallas_call(
        _peer_project_kernel,
        out_shape=jax.ShapeDtypeStruct((2, PEER_HALF_TILE, PEER_TOKENS), F32),
        grid=(1,),
        in_specs=[pl.BlockSpec((PEER_TILE, d), lambda i: (0, 0)),
                  pl.BlockSpec((d, PEER_TOKENS), lambda i: (0, 0))],
        out_specs=pl.BlockSpec((2, PEER_HALF_TILE, PEER_TOKENS), lambda i: (0, 0, 0)),
        compiler_params=_params(("arbitrary",)),
        name="peer_project_first",
    )(down, h2t)


def _peer_dense_kernel(h2t_ref, down_ref, up_ref, e1_ref, e2_ref, theta_ref, pre0_ref,
                       x1_ref, gate_ref, g_ref, out_ref, pre_ref, act_ref):
    t = pl.program_id(0)
    e = pl.program_id(1)

    @pl.when((t == 0) & (e == 0))
    def _():
        pre_ref[...] = pre0_ref[...]

    @pl.when(e == 0)
    def _():
        out_ref[...] = jnp.zeros(out_ref.shape, F32)

    for half in range(2):
        _gate_act(pre_ref, act_ref, e1_ref, e2_ref, theta_ref, half)
    tt = pre_ref.shape[2]
    out_ref[...] += lax.dot_general(act_ref[...].reshape(PEER_TILE, tt), up_ref[...], (((0,), (0,)), ((), ())),
                                    preferred_element_type=F32)
    for half in range(2):
        rows = slice(half * PEER_HALF_TILE, (half + 1) * PEER_HALF_TILE)
        pre_ref[half] = jnp.dot(down_ref[rows, :], h2t_ref[...], preferred_element_type=F32)

    @pl.when(e == pl.num_programs(1) - 1)
    def _():
        y = out_ref[...]
        out_ref[...] = x1_ref[...] + gate_ref[...] * (_rms(y) * g_ref[...])


def _peer_dense(h2t, down, up, e1, e2, theta, x1, gate2, g3):
    d, s = h2t.shape
    tt = PEER_TOKENS
    n_e = down.shape[0] // PEER_TILE
    n_t = s // tt
    vec = pl.BlockSpec((1, d), lambda t, e: (0, 0))
    next_e = lambda e: (e + 1) % n_e
    next_t = lambda t, e: jnp.minimum(t + (e + 1) // n_e, n_t - 1)
    return pl.pallas_call(
        _peer_dense_kernel,
        out_shape=jax.ShapeDtypeStruct((s, d), F32),
        grid=(n_t, n_e),
        in_specs=[pl.BlockSpec((d, tt), lambda t, e: (0, next_t(t, e))),
                  pl.BlockSpec((PEER_TILE, d), lambda t, e: (next_e(e), 0)),
                  pl.BlockSpec((PEER_TILE, d), lambda t, e: (e, 0)),
                  pl.BlockSpec((PEER_HEADS, 1, 2 * PEER_HALF_KEYS, tt), lambda t, e: (0, e, 0, t)),
                  pl.BlockSpec((PEER_HEADS, PEER_N_KEYS, tt), lambda t, e: (0, 0, t)),
                  pl.BlockSpec((PEER_HEADS, tt), lambda t, e: (0, t)),
                  pl.BlockSpec((2, PEER_HALF_TILE, tt), lambda t, e: (0, 0, 0)),
                  pl.BlockSpec((tt, d), lambda t, e: (t, 0)),
                  vec, vec],
        out_specs=pl.BlockSpec((tt, d), lambda t, e: (t, 0)),
        scratch_shapes=[pltpu.VMEM((2, PEER_HALF_TILE, tt), F32),
                        pltpu.VMEM((2, PEER_HALF_TILE, tt), BF16)],
        compiler_params=_params(("arbitrary", "arbitrary")),
        name="peer_dense",
    )(h2t, down, up, e1, e2, theta, _peer_project_first(h2t, down), x1, gate2, g3)


def kernel(x, c, positions, ada_w, ada_b, norm_gains, w_in, w_attn_o, conv_w, conv_b, conv_ln_g,
           conv_ln_b, w_conv_o, w_out, peer_w_q, peer_sub_keys, peer_down, peer_up):
    batch, seq, d = x.shape
    depth = ada_w.shape[0]
    assert batch == 1, "kernels are written for a single sequence"
    assert all(window // dilation == ATTN_BLOCK for window, dilation in DILATED_GROUPS)
    cc = conv_w.shape[-1]
    xs = x[0]
    inv_freq = ROPE_THETA ** (-jnp.arange(0, HEAD_DIM, 2, dtype=F32) / HEAD_DIM)
    invf_row = jnp.concatenate([inv_freq, inv_freq])[None, :]
    cos, sin = _rope_tables(positions[0].astype(F32)[:, None], invf_row)
    row = lambda v: v[None, :]
    for l in range(depth):
        mod = _modulation(c[0][:, None], ada_w[l], ada_b[l][None, :])
        shift1, scale1, gate1, shift2, scale2, gate2 = [mod[:, k * d:(k + 1) * d] for k in range(6)]
        g = norm_gains[l]
        w_in_b = w_in[l].astype(BF16)

        h = _norm_mod(xs, row(g[0]), scale1, shift1)
        u = _glu_proj(h, w_in_b, cc)
        gates = _gates_proj(h, w_in_b, 3 * ATTN_WIDTH + 2 * cc, 2 * d)
        outs, lses = [], []
        for gi, (_, dilation) in enumerate(DILATED_GROUPS):
            o, lse = _dilated_attention(_qkv_proj(h, w_in_b, cos, sin, gi, dilation))
            outs.append(o)
            lses.append(lse)
        merged = _mix(outs, lses, u, gates, w_attn_o[l].astype(BF16), conv_w[l][:, 0, :],
                      row(conv_b[l]), row(conv_ln_g[l]), row(conv_ln_b[l]), w_conv_o[l].astype(BF16))
        x1, h2t = _out_proj(merged, w_out[l].astype(BF16), xs, gate1, row(g[1]), row(g[2]),
                            scale2, shift2)

        e1, e2, theta = _peer_query(h2t, peer_w_q[l].T.astype(BF16), peer_sub_keys[l])
        xs = _peer_dense(h2t, peer_down[l].astype(BF16), peer_up[l].astype(BF16),
                         e1, e2, theta, x1, gate2, row(g[3]))
    return xs[None]
```

```python
import functools
import math

import jax
import jax.numpy as jnp
from jax import lax
from jax.experimental import pallas as pl
from jax.experimental.pallas import tpu as pltpu

F32 = jnp.float32
BF16 = jnp.bfloat16

HEAD_DIM = 128
HEADS_PER_GROUP = 4
DILATED_GROUPS = ((128, 1), (512, 4), (2048, 16))
N_GROUPS = len(DILATED_GROUPS)
GROUP_WIDTH = HEADS_PER_GROUP * HEAD_DIM
ATTN_WIDTH = N_GROUPS * GROUP_WIDTH
ATTN_BLOCK = 128
ROPE_THETA = 10000.0
CONV_WIDTH = 31
CONV_HALO = 32
PEER_HEADS = 8
PEER_N_KEYS = 128
PEER_HALF = 128
PEER_TOPK = 16
NORM_EPS = 1e-6
MASK_VALUE = -1e30
NEG_INF = float("-inf")
INV_SQRT2 = 1.0 / math.sqrt(2.0)

SUBLANES = 8
COL = 512
VMEM_LIMIT = 56 * 1024 * 1024

_CAND_ROWS = [PEER_TOPK // (a + 1) for a in range(PEER_TOPK)]
_N_CAND = sum(_CAND_ROWS)
_CAND_PAD = -(-_N_CAND // 8) * 8


def _params(sem):
    return pltpu.CompilerParams(dimension_semantics=sem, vmem_limit_bytes=VMEM_LIMIT)


def _rms(x):
    return x * lax.rsqrt(jnp.mean(x * x, axis=-1, keepdims=True) + NORM_EPS)


def _mod_kernel(c_ref, w_ref, b_ref, o_ref):
    rows = 256

    def body(k, acc):
        r = pl.multiple_of(k * rows, rows)
        c = c_ref[pl.ds(r, rows), :]
        sc = c * jax.nn.sigmoid(c)
        return acc + jnp.sum(w_ref[pl.ds(r, rows), :] * sc, axis=0, keepdims=True)

    acc = lax.fori_loop(0, w_ref.shape[0] // rows, body, jnp.zeros(o_ref.shape, F32))
    o_ref[...] = acc + b_ref[...]


def _modulation(c_col, w, b):
    d, n = w.shape
    tn = 512
    return pl.pallas_call(
        _mod_kernel,
        out_shape=jax.ShapeDtypeStruct((1, n), F32),
        grid=(n // tn,),
        in_specs=[pl.BlockSpec((d, 1), lambda j: (0, 0)),
                  pl.BlockSpec((d, tn), lambda j: (0, j)),
                  pl.BlockSpec((1, tn), lambda j: (0, j))],
        out_specs=pl.BlockSpec((1, tn), lambda j: (0, j)),
        compiler_params=_params(("parallel",)),
        name="mod",
    )(c_col, w, b)


def _rope_kernel(pos_ref, invf_ref, cos_ref, sin_ref):
    ang = pos_ref[...] * invf_ref[...]
    cos_ref[...] = jnp.cos(ang)
    s = jnp.sin(ang)
    lane = lax.broadcasted_iota(jnp.int32, s.shape, 1)
    sin_ref[...] = jnp.where(lane < HEAD_DIM // 2, -s, s)


def _rope_tables(pos_col, invf_row):
    s = pos_col.shape[0]
    ts = 2048
    return pl.pallas_call(
        _rope_kernel,
        out_shape=(jax.ShapeDtypeStruct((s, HEAD_DIM), F32),) * 2,
        grid=(s // ts,),
        in_specs=[pl.BlockSpec((ts, 1), lambda i: (i, 0)),
                  pl.BlockSpec((1, HEAD_DIM), lambda i: (0, 0))],
        out_specs=(pl.BlockSpec((ts, HEAD_DIM), lambda i: (i, 0)),) * 2,
        compiler_params=_params(("parallel",)),
        name="rope",
    )(pos_col, invf_row)


def _norm_mod_kernel(x_ref, g_ref, scale_ref, shift_ref, o_ref):
    y = _rms(x_ref[...]) * g_ref[...]
    o_ref[...] = (y * (1.0 + scale_ref[...]) + shift_ref[...]).astype(o_ref.dtype)


def _norm_mod(x, g, scale, shift):
    s, d = x.shape
    tm = 512
    vec = pl.BlockSpec((1, d), lambda i: (0, 0))
    return pl.pallas_call(
        _norm_mod_kernel,
        out_shape=jax.ShapeDtypeStruct((s, d), BF16),
        grid=(s // tm,),
        in_specs=[pl.BlockSpec((tm, d), lambda i: (i, 0)), vec, vec, vec],
        out_specs=pl.BlockSpec((tm, d), lambda i: (i, 0)),
        compiler_params=_params(("parallel",)),
        name="norm_mod",
    )(x, g, scale, shift)


def _qkv_kernel(h_ref, w_ref, cos_ref, sin_ref, o_ref, stage_ref):
    acc = jnp.dot(h_ref[...], w_ref[...], preferred_element_type=F32)
    part = pl.program_id(1)

    heads = [slice(hh * HEAD_DIM, (hh + 1) * HEAD_DIM) for hh in range(HEADS_PER_GROUP)]

    @pl.when(part < 2)
    def _():
        cos = cos_ref[...]
        sin = sin_ref[...]
        for hh, cols in enumerate(heads):
            t = acc[:, cols]
            stage_ref[hh] = t * cos + pltpu.roll(t, HEAD_DIM // 2, axis=1) * sin

    @pl.when(part == 2)
    def _():
        for hh, cols in enumerate(heads):
            stage_ref[hh] = acc[:, cols]

    dilation, rows, _ = o_ref.shape
    for r in range(dilation):
        for hh, cols in enumerate(heads):
            o_ref[r, :, cols] = stage_ref[hh, pl.ds(r, rows, stride=dilation), :].astype(o_ref.dtype)


def _qkv_proj(h, w_in, cos, sin, group, dilation):
    s, d = h.shape
    tm = 1024
    return pl.pallas_call(
        _qkv_kernel,
        out_shape=jax.ShapeDtypeStruct((dilation, s // dilation, 3 * GROUP_WIDTH), BF16),
        grid=(s // tm, 3),
        in_specs=[pl.BlockSpec((tm, d), lambda i, j: (i, 0)),
                  pl.BlockSpec((d, COL), lambda i, j: (0, j * N_GROUPS + group)),
                  pl.BlockSpec((tm, HEAD_DIM), lambda i, j: (i, 0)),
                  pl.BlockSpec((tm, HEAD_DIM), lambda i, j: (i, 0))],
        out_specs=pl.BlockSpec((dilation, tm // dilation, COL), lambda i, j: (0, i, j)),
        scratch_shapes=[pltpu.VMEM((HEADS_PER_GROUP, tm, HEAD_DIM), F32)],
        compiler_params=_params(("parallel", "arbitrary")),
        name=f"qkv_d{dilation}",
    )(h, w_in, cos, sin)


def _glu_kernel(h_ref, wa_ref, wb_ref, o_ref):
    h = h_ref[...]
    a = jnp.dot(h, wa_ref[...], preferred_element_type=F32)
    b = jnp.dot(h, wb_ref[...], preferred_element_type=F32)
    o_ref[...] = a * jax.nn.sigmoid(b)


def _glu_proj(h, w_in, conv_channels):
    s, d = h.shape
    tm = 1024
    a0 = 3 * ATTN_WIDTH // COL
    nb = conv_channels // COL
    return pl.pallas_call(
        _glu_kernel,
        out_shape=jax.ShapeDtypeStruct((s, conv_channels), F32),
        grid=(s // tm, nb),
        in_specs=[pl.BlockSpec((tm, d), lambda i, j: (i, 0)),
                  pl.BlockSpec((d, COL), lambda i, j: (0, a0 + j)),
                  pl.BlockSpec((d, COL), lambda i, j: (0, a0 + nb + j))],
        out_specs=pl.BlockSpec((tm, COL), lambda i, j: (i, j)),
        compiler_params=_params(("parallel", "arbitrary")),
        name="glu",
    )(h, w_in, w_in)


def _gates_kernel(h_ref, w_ref, o_ref):
    o_ref[...] = jax.nn.sigmoid(jnp.dot(h_ref[...], w_ref[...], preferred_element_type=F32))


def _gates_proj(h, w_in, col0, width):
    s, d = h.shape
    tm = 1024
    c0 = col0 // COL
    return pl.pallas_call(
        _gates_kernel,
        out_shape=jax.ShapeDtypeStruct((s, width), F32),
        grid=(s // tm, width // COL),
        in_specs=[pl.BlockSpec((tm, d), lambda i, j: (i, 0)),
                  pl.BlockSpec((d, COL), lambda i, j: (0, c0 + j))],
        out_specs=pl.BlockSpec((tm, COL), lambda i, j: (i, j)),
        compiler_params=_params(("parallel", "arbitrary")),
        name="gates",
    )(h, w_in)


ATTN_Q_BLOCKS = 4


def _attn_kernel(q_ref, kc_ref, kp_ref, vc_ref, vp_ref, o_ref, l_ref):
    n = pl.program_id(1)
    qi = lax.broadcasted_iota(jnp.int32, (ATTN_BLOCK, 2 * ATTN_BLOCK), 0)
    kj = lax.broadcasted_iota(jnp.int32, (ATTN_BLOCK, 2 * ATTN_BLOCK), 1)
    band = jnp.abs(kj - qi - ATTN_BLOCK // 2) <= ATTN_BLOCK // 2
    first_lo = jnp.where(n == 0, ATTN_BLOCK, 0)
    band_first = band & (kj >= first_lo)
    scale = HEAD_DIM ** -0.5
    for hh in range(HEADS_PER_GROUP):
        cols = slice(hh * HEAD_DIM, (hh + 1) * HEAD_DIM)
        for b in range(ATTN_Q_BLOCKS):
            rows = slice(b * ATTN_BLOCK, (b + 1) * ATTN_BLOCK)
            q = q_ref[rows, cols]
            if b == 0:
                k = jnp.concatenate([kp_ref[:, cols], kc_ref[0:ATTN_BLOCK, cols]], axis=0)
                v = jnp.concatenate([vp_ref[:, cols], vc_ref[0:ATTN_BLOCK, cols]], axis=0)
                valid = band_first
            else:
                band_rows = slice((b - 1) * ATTN_BLOCK, (b + 1) * ATTN_BLOCK)
                k = kc_ref[band_rows, cols]
                v = vc_ref[band_rows, cols]
                valid = band
            s = lax.dot_general(q, k, (((1,), (1,)), ((), ())), preferred_element_type=F32) * scale
            s = jnp.where(valid, s, MASK_VALUE)
            m = jnp.max(s, axis=-1, keepdims=True)
            p = jnp.exp(s - m)
            den = jnp.sum(p, axis=-1, keepdims=True)
            o = jnp.dot(p.astype(BF16), v, preferred_element_type=F32)
            o_ref[rows, cols] = o * (1.0 / den)
            l_ref[rows, cols] = jnp.broadcast_to(m + jnp.log(den), (ATTN_BLOCK, HEAD_DIM))


def _dilated_attention(qkv):
    dilation, sub_len, _ = qkv.shape
    qt = ATTN_Q_BLOCKS * ATTN_BLOCK

    def cur(part):
        return pl.BlockSpec((None, qt, COL), lambda r, n: (r, n, part))

    def prev(part):
        return pl.BlockSpec((None, ATTN_BLOCK, COL),
                            lambda r, n: (r, jnp.maximum(n * ATTN_Q_BLOCKS - 1, 0), part))

    out_sds = jax.ShapeDtypeStruct((dilation, sub_len, GROUP_WIDTH), F32)
    out_spec = pl.BlockSpec((None, qt, GROUP_WIDTH), lambda r, n: (r, n, 0))
    return pl.pallas_call(
        _attn_kernel,
        out_shape=(out_sds, out_sds),
        grid=(dilation, sub_len // qt),
        in_specs=[cur(0), cur(1), prev(1), cur(2), prev(2)],
        out_specs=(out_spec, out_spec),
        compiler_params=_params(("parallel", "arbitrary")),
        name=f"attn_d{dilation}",
    )(qkv, qkv, qkv, qkv, qkv)


def _mix_kernel(o0_ref, o1_ref, o2_ref, l0_ref, l1_ref, l2_ref, u_ref, uh_ref, ga_ref, gc_ref,
                wao_ref, cw_ref, cb_ref, lg_ref, lb_ref, wco_ref, out_ref, ucat_ref, nat_ref, shift_ref):
    i = pl.program_id(0)
    tm = u_ref.shape[0]

    def natural(ref, slot):
        dilation, rows, _ = ref.shape
        if dilation == 1:
            return ref[0]
        for hh in range(HEADS_PER_GROUP):
            for r in range(dilation):
                nat_ref[slot, hh, pl.ds(r, rows, stride=dilation), :] = ref[r, :, hh * HEAD_DIM:(hh + 1) * HEAD_DIM]
        return jnp.concatenate([nat_ref[slot, hh] for hh in range(HEADS_PER_GROUP)], axis=1)

    l0, l1, l2 = natural(l0_ref, 0), natural(l1_ref, 1), natural(l2_ref, 2)
    m = jnp.maximum(jnp.maximum(l0, l1), l2)
    e0, e1, e2 = jnp.exp(l0 - m), jnp.exp(l1 - m), jnp.exp(l2 - m)
    attn = (e0 * natural(o0_ref, 3) + e1 * natural(o1_ref, 4) + e2 * natural(o2_ref, 5)) / (e0 + e1 + e2)
    attn_o = jnp.dot(attn.astype(BF16), wao_ref[...], preferred_element_type=F32)
    ucat_ref[0:CONV_HALO, :] = jnp.where(i > 0, uh_ref[...], 0.0)
    ucat_ref[CONV_HALO:, :] = u_ref[...]
    off = CONV_HALO - (CONV_WIDTH - 1)
    span = tm + CONV_HALO - SUBLANES
    for s in range(1, SUBLANES):
        shift_ref[s - 1] = ucat_ref[s:s + span, :]
    conv = jnp.zeros(u_ref.shape, F32) + cb_ref[...]
    for w in range(CONV_WIDTH):
        q, s = divmod(off + w, SUBLANES)
        src = ucat_ref if s == 0 else shift_ref.at[s - 1]
        conv = conv + src[q * SUBLANES:q * SUBLANES + tm, :] * cw_ref[w:w + 1, :]
    mu = jnp.mean(conv, axis=-1, keepdims=True)
    cen = conv - mu
    var = jnp.mean(cen * cen, axis=-1, keepdims=True)
    y = cen * lax.rsqrt(var + NORM_EPS) * lg_ref[...] + lb_ref[...]
    y = y * jax.nn.sigmoid(y)
    conv_o = jnp.dot(y.astype(BF16), wco_ref[...], preferred_element_type=F32)
    out_ref[...] = (ga_ref[...] * attn_o + gc_ref[...] * conv_o).astype(out_ref.dtype)


def _mix(outs, lses, u, gates, w_attn_o, conv_w, conv_b, ln_g, ln_b, w_conv_o):
    s, cc = u.shape
    d = w_attn_o.shape[1]
    tm = 256
    row = lambda i: (i, 0)
    const = lambda i: (0, 0)
    cvec = pl.BlockSpec((1, cc), const)

    def grp(arr):
        dilation = arr.shape[0]
        return pl.BlockSpec((dilation, tm // dilation, GROUP_WIDTH), lambda i: (0, i, 0))

    return pl.pallas_call(
        _mix_kernel,
        out_shape=jax.ShapeDtypeStruct((s, d), BF16),
        grid=(s // tm,),
        in_specs=[grp(a) for a in (*outs, *lses)] + [
            pl.BlockSpec((tm, cc), row),
            pl.BlockSpec((CONV_HALO, cc), lambda i: (jnp.maximum(i * (tm // CONV_HALO) - 1, 0), 0)),
            pl.BlockSpec((tm, d), lambda i: (i, 0)),
            pl.BlockSpec((tm, d), lambda i: (i, 1)),
            pl.BlockSpec((GROUP_WIDTH, d), const),
            pl.BlockSpec((CONV_WIDTH, cc), const),
            cvec, cvec, cvec,
            pl.BlockSpec((cc, d), const)],
        out_specs=pl.BlockSpec((tm, d), row),
        scratch_shapes=[pltpu.VMEM((tm + CONV_HALO, cc), F32),
                        pltpu.VMEM((2 * N_GROUPS, HEADS_PER_GROUP, tm, HEAD_DIM), F32),
                        pltpu.VMEM((SUBLANES - 1, tm + CONV_HALO - SUBLANES, cc), F32)],
        compiler_params=_params(("parallel",)),
        name="mix",
    )(*outs, *lses, u, u, gates, gates, w_attn_o, conv_w, conv_b, ln_g, ln_b, w_conv_o)


def _out_proj_kernel(m_ref, w_ref, x_ref, gate_ref, g1_ref, g2_ref, scale_ref, shift_ref,
                     x1_ref, h2t_ref):
    y = jnp.dot(m_ref[...], w_ref[...], preferred_element_type=F32)
    x1 = x_ref[...] + gate_ref[...] * (_rms(y) * g1_ref[...])
    x1_ref[...] = x1
    h2 = (_rms(x1) * g2_ref[...]) * (1.0 + scale_ref[...]) + shift_ref[...]
    h2t_ref[...] = h2.T.astype(h2t_ref.dtype)


def _out_proj(merged, w_out, x, gate1, g1, g2, scale2, shift2):
    s, d = x.shape
    tm = 256
    row = lambda i: (i, 0)
    vec = pl.BlockSpec((1, d), lambda i: (0, 0))
    return pl.pallas_call(
        _out_proj_kernel,
        out_shape=(jax.ShapeDtypeStruct((s, d), F32), jax.ShapeDtypeStruct((d, s), BF16)),
        grid=(s // tm,),
        in_specs=[pl.BlockSpec((tm, d), row), pl.BlockSpec((d, d), lambda i: (0, 0)),
                  pl.BlockSpec((tm, d), row), vec, vec, vec, vec, vec],
        out_specs=(pl.BlockSpec((tm, d), row), pl.BlockSpec((d, tm), lambda i: (0, i))),
        compiler_params=_params(("parallel",)),
        name="out_proj",
    )(merged, w_out, x, gate1, g1, g2, scale2, shift2)


def _odd_even_merge_sort(lo, hi):
    def merge(lo, hi, r):
        step = 2 * r
        if step < hi - lo:
            yield from merge(lo, hi, step)
            yield from merge(lo + r, hi, step)
            for i in range(lo + r, hi - r, step):
                yield (i, i + r)
        else:
            yield (lo, lo + r)

    if hi > lo:
        mid = lo + (hi - lo) // 2
        yield from _odd_even_merge_sort(lo, mid)
        yield from _odd_even_merge_sort(mid + 1, hi)
        yield from merge(lo, hi, 1)


def _bitonic_merge(n):
    half = n // 2
    while half >= 1:
        for i in range(n):
            if (i // half) % 2 == 0:
                yield (i, i + half)
        half //= 2


_SORT_PAIRS = {n: tuple(_odd_even_merge_sort(0, n - 1)) for n in (PEER_TOPK // 2, PEER_TOPK)}
_MERGE_PAIRS = tuple(_bitonic_merge(PEER_TOPK))


def _compare_exchange(v, pairs):
    for i, j in pairs:
        v[i], v[j] = jnp.maximum(v[i], v[j]), jnp.minimum(v[i], v[j])


def _top16_rows(x):
    blocks = x.shape[0] // SUBLANES
    assert x.shape[0] == blocks * SUBLANES and blocks <= PEER_TOPK
    n = PEER_TOPK // 2 if blocks <= PEER_TOPK // 2 else PEER_TOPK
    v = [x[k * SUBLANES:(k + 1) * SUBLANES, :] for k in range(blocks)]
    v += [jnp.full_like(v[0], NEG_INF)] * (n - blocks)
    _compare_exchange(v, _SORT_PAIRS[n])
    shift = SUBLANES // 2
    while shift >= 1:
        partner = [pltpu.roll(blk, shift, axis=0) for blk in v]
        if len(v) < PEER_TOPK:
            v = v + partner[::-1]
        else:
            v = [jnp.maximum(v[k], partner[PEER_TOPK - 1 - k]) for k in range(PEER_TOPK)]
        _compare_exchange(v, _MERGE_PAIRS)
        shift //= 2
    return [blk[0:1, :] for blk in v]


def _peer_query_kernel(h2t_ref, wqt_ref, keys_ref, e1_ref, e2_ref, theta_ref,
                       qt_ref, top_ref, cand_ref):
    qt_ref[...] = jnp.dot(wqt_ref[...], h2t_ref[...], preferred_element_type=F32)
    for h in range(PEER_HEADS):
        u = []
        for side in range(2):
            r0 = (2 * h + side) * PEER_HALF
            qc = qt_ref[r0:r0 + PEER_HALF, :].astype(BF16)
            sc = jnp.dot(keys_ref[h, side].astype(BF16), qc, preferred_element_type=F32)
            us = jnp.exp(sc - jnp.max(sc, axis=0, keepdims=True))
            vals = _top16_rows(us)
            for k in range(PEER_TOPK):
                top_ref[side, k:k + 1, :] = vals[k]
            u.append(us)
        cand_ref[...] = jnp.full(cand_ref.shape, -1.0, F32)
        off = 0
        for a, nb in enumerate(_CAND_ROWS):
            cand_ref[off:off + nb, :] = top_ref[0, a:a + 1, :] * top_ref[1, 0:nb, :]
            off += nb
        cand = cand_ref[...]
        selected = cand >= _top16_rows(cand)[-1]
        inv_z = 1.0 / jnp.sum(jnp.where(selected, cand, 0.0), axis=0, keepdims=True)
        off = 0
        for a, nb in enumerate(_CAND_ROWS):
            cand_ref[off:off + nb, :] = (top_ref[0, a:a + 1, :] * inv_z) * top_ref[1, 0:nb, :]
            off += nb
        theta = jnp.min(jnp.where(selected, cand_ref[...], jnp.inf), axis=0, keepdims=True)
        e1 = u[0] * inv_z
        for blk in range(PEER_N_KEYS // 8):
            e1_ref[h, blk] = e1[blk * 8:(blk + 1) * 8, :]
        e2_ref[h] = u[1]
        theta_ref[h:h + 1, :] = theta


def _peer_query(h2t, wq_t, sub_keys):
    d, s = h2t.shape
    tt = 512
    nq = wq_t.shape[0]
    side_blocked = jax.ShapeDtypeStruct((PEER_HEADS, PEER_N_KEYS // 8, 8, s), F32)
    side_flat = jax.ShapeDtypeStruct((PEER_HEADS, PEER_N_KEYS, s), F32)
    blocked_spec = pl.BlockSpec((PEER_HEADS, PEER_N_KEYS // 8, 8, tt), lambda t: (0, 0, 0, t))
    flat_spec = pl.BlockSpec((PEER_HEADS, PEER_N_KEYS, tt), lambda t: (0, 0, t))
    return pl.pallas_call(
        _peer_query_kernel,
        out_shape=(side_blocked, side_flat, jax.ShapeDtypeStruct((PEER_HEADS, s), F32)),
        grid=(s // tt,),
        in_specs=[pl.BlockSpec((d, tt), lambda t: (0, t)),
                  pl.BlockSpec((nq, d), lambda t: (0, 0)),
                  pl.BlockSpec(sub_keys.shape, lambda t: (0, 0, 0, 0))],
        out_specs=(blocked_spec, flat_spec, pl.BlockSpec((PEER_HEADS, tt), lambda t: (0, t))),
        scratch_shapes=[pltpu.VMEM((nq, tt), F32),
                        pltpu.VMEM((2, PEER_TOPK, tt), F32),
                        pltpu.VMEM((_CAND_PAD, tt), F32)],
        compiler_params=_params(("parallel",)),
        name="peer_query",
    )(h2t, wq_t, sub_keys)


PEER_LANE_CHUNK = 128
PEER_HALF_KEYS = 4
PEER_HALF_TILE = PEER_HALF_KEYS * PEER_N_KEYS
PEER_TILE = 2 * PEER_HALF_TILE
PEER_TOKENS = 512


def _gate_act(pre_ref, act_ref, e1_ref, e2_ref, theta_ref, half):
    tt = pre_ref.shape[2]
    for ii in range(PEER_HALF_KEYS):
        rows = slice(ii * PEER_N_KEYS, (ii + 1) * PEER_N_KEYS)
        key = half * PEER_HALF_KEYS + ii
        for c in range(tt // PEER_LANE_CHUNK):
            lanes = slice(c * PEER_LANE_CHUNK, (c + 1) * PEER_LANE_CHUNK)
            w = jnp.zeros((PEER_N_KEYS, PEER_LANE_CHUNK), F32)
            for h in range(PEER_HEADS):
                p = e1_ref[h, 0, key:key + 1, lanes] * e2_ref[h, :, lanes]
                w = w + jnp.where(p >= theta_ref[h:h + 1, lanes], p, 0.0)
            act_ref[half, rows, lanes] = (pre_ref[half, rows, lanes] * w).astype(act_ref.dtype)


def _gelu(x):
    return (0.5 * x) * (1.0 + lax.erf(x * INV_SQRT2))


def _peer_project_kernel(down_ref, h2t_ref, pre_ref):
    for half in range(2):
        rows = slice(half * PEER_HALF_TILE, (half + 1) * PEER_HALF_TILE)
        pre_ref[half] = _gelu(jnp.dot(down_ref[rows, :], h2t_ref[...], preferred_element_type=F32))


def _peer_project_first(h2t, down):
    d = h2t.shape[0]
    return pl.pallas_call(
        _peer_project_kernel,
        out_shape=jax.ShapeDtypeStruct((2, PEER_HALF_TILE, PEER_TOKENS), F32),
        grid=(1,),
        in_specs=[pl.BlockSpec((PEER_TILE, d), lambda i: (0, 0)),
                  pl.BlockSpec((d, PEER_TOKENS), lambda i: (0, 0))],
        out_specs=pl.BlockSpec((2, PEER_HALF_TILE, PEER_TOKENS), lambda i: (0, 0, 0)),
        compiler_params=_params(("arbitrary",)),
        name="peer_project_first",
    )(down, h2t)


def _peer_dense_kernel(h2t_ref, down_ref, up_ref, e1_ref, e2_ref, theta_ref, pre0_ref,
                       x1_ref, gate_ref, g_ref, out_ref, pre_ref, act_ref):
    t = pl.program_id(0)
    e = pl.program_id(1)

    @pl.when((t == 0) & (e == 0))
    def _():
        pre_ref[...] = pre0_ref[...]

    @pl.when(e == 0)
    def _():
        out_ref[...] = jnp.zeros(out_ref.shape, F32)

    for half in range(2):
        _gate_act(pre_ref, act_ref, e1_ref, e2_ref, theta_ref, half)
    tt = pre_ref.shape[2]
    out_ref[...] += lax.dot_general(act_ref[...].reshape(PEER_TILE, tt), up_ref[...], (((0,), (0,)), ((), ())),
                                    preferred_element_type=F32)
    pre_ref[...] = _gelu(jnp.dot(down_ref[...], h2t_ref[...],
                                 preferred_element_type=F32)).reshape(2, PEER_HALF_TILE, tt)

    @pl.when(e == pl.num_programs(1) - 1)
    def _():
        y = out_ref[...]
        out_ref[...] = x1_ref[...] + gate_ref[...] * (_rms(y) * g_ref[...])


def _peer_dense(h2t, down, up, e1, e2, theta, x1, gate2, g3):
    d, s = h2t.shape
    tt = PEER_TOKENS
    n_e = down.shape[0] // PEER_TILE
    n_t = s // tt
    vec = pl.BlockSpec((1, d), lambda t, e: (0, 0))
    next_e = lambda e: (e + 1) % n_e
    next_t = lambda t, e: jnp.minimum(t + (e + 1) // n_e, n_t - 1)
    return pl.pallas_call(
        _peer_dense_kernel,
        out_shape=jax.ShapeDtypeStruct((s, d), F32),
        grid=(n_t, n_e),
        in_specs=[pl.BlockSpec((d, tt), lambda t, e: (0, next_t(t, e))),
                  pl.BlockSpec((PEER_TILE, d), lambda t, e: (next_e(e), 0)),
                  pl.BlockSpec((PEER_TILE, d), lambda t, e: (e, 0)),
                  pl.BlockSpec((PEER_HEADS, 1, 2 * PEER_HALF_KEYS, tt), lambda t, e: (0, e, 0, t)),
                  pl.BlockSpec((PEER_HEADS, PEER_N_KEYS, tt), lambda t, e: (0, 0, t)),
                  pl.BlockSpec((PEER_HEADS, tt), lambda t, e: (0, t)),
                  pl.BlockSpec((2, PEER_HALF_TILE, tt), lambda t, e: (0, 0, 0)),
                  pl.BlockSpec((tt, d), lambda t, e: (t, 0)),
                  vec, vec],
        out_specs=pl.BlockSpec((tt, d), lambda t, e: (t, 0)),
        scratch_shapes=[pltpu.VMEM((2, PEER_HALF_TILE, tt), F32),
                        pltpu.VMEM((2, PEER_HALF_TILE, tt), BF16)],
        compiler_params=_params(("arbitrary", "arbitrary")),
        name="peer_dense",
    )(h2t, down, up, e1, e2, theta, _peer_project_first(h2t, down), x1, gate2, g3)


def kernel(x, c, positions, ada_w, ada_b, norm_gains, w_in, w_attn_o, conv_w, conv_b, conv_ln_g,
           conv_ln_b, w_conv_o, w_out, peer_w_q, peer_sub_keys, peer_down, peer_up):
    batch, seq, d = x.shape
    depth = ada_w.shape[0]
    assert batch == 1, "kernels are written for a single sequence"
    assert all(window // dilation == ATTN_BLOCK for window, dilation in DILATED_GROUPS)
    cc = conv_w.shape[-1]
    xs = x[0]
    inv_freq = ROPE_THETA ** (-jnp.arange(0, HEAD_DIM, 2, dtype=F32) / HEAD_DIM)
    invf_row = jnp.concatenate([inv_freq, inv_freq])[None, :]
    cos, sin = _rope_tables(positions[0].astype(F32)[:, None], invf_row)
    row = lambda v: v[None, :]
    for l in range(depth):
        mod = _modulation(c[0][:, None], ada_w[l], ada_b[l][None, :])
        shift1, scale1, gate1, shift2, scale2, gate2 = [mod[:, k * d:(k + 1) * d] for k in range(6)]
        g = norm_gains[l]
        w_in_b = w_in[l].astype(BF16)

        h = _norm_mod(xs, row(g[0]), scale1, shift1)
        u = _glu_proj(h, w_in_b, cc)
        gates = _gates_proj(h, w_in_b, 3 * ATTN_WIDTH + 2 * cc, 2 * d)
        outs, lses = [], []
        for gi, (_, dilation) in enumerate(DILATED_GROUPS):
            o, lse = _dilated_attention(_qkv_proj(h, w_in_b, cos, sin, gi, dilation))
            outs.append(o)
            lses.append(lse)
        merged = _mix(outs, lses, u, gates, w_attn_o[l].astype(BF16), conv_w[l][:, 0, :],
                      row(conv_b[l]), row(conv_ln_g[l]), row(conv_ln_b[l]), w_conv_o[l].astype(BF16))
        x1, h2t = _out_proj(merged, w_out[l].astype(BF16), xs, gate1, row(g[1]), row(g[2]),
                            scale2, shift2)

        e1, e2, theta = _peer_query(h2t, peer_w_q[l].T.astype(BF16), peer_sub_keys[l])
        xs = _peer_dense(h2t, peer_down[l].astype(BF16), peer_up[l].astype(BF16),
                         e1, e2, theta, x1, gate2, row(g[3]))
    return xs[None]
```

```python
import functools
import math

import jax
import jax.numpy as jnp
from jax import lax
from jax.experimental import pallas as pl
from jax.experimental.pallas import tpu as pltpu

F32 = jnp.float32
BF16 = jnp.bfloat16

HEAD_DIM = 128
HEADS_PER_GROUP = 4
DILATED_GROUPS = ((128, 1), (512, 4), (2048, 16))
N_GROUPS = len(DILATED_GROUPS)
GROUP_WIDTH = HEADS_PER_GROUP * HEAD_DIM
ATTN_WIDTH = N_GROUPS * GROUP_WIDTH
ATTN_BLOCK = 128
ROPE_THETA = 10000.0
CONV_WIDTH = 31
CONV_HALO = 32
PEER_HEADS = 8
PEER_N_KEYS = 128
PEER_HALF = 128
PEER_TOPK = 16
NORM_EPS = 1e-6
MASK_VALUE = -1e30
NEG_INF = float("-inf")
INV_SQRT2 = 1.0 / math.sqrt(2.0)

SUBLANES = 8
COL = 512
VMEM_LIMIT = 56 * 1024 * 1024

_CAND_ROWS = [PEER_TOPK // (a + 1) for a in range(PEER_TOPK)]
_N_CAND = sum(_CAND_ROWS)
_CAND_PAD = -(-_N_CAND // 8) * 8


def _params(sem):
    return pltpu.CompilerParams(dimension_semantics=sem, vmem_limit_bytes=VMEM_LIMIT)


def _rms(x):
    return x * lax.rsqrt(jnp.mean(x * x, axis=-1, keepdims=True) + NORM_EPS)


def _mod_kernel(c_ref, w_ref, b_ref, o_ref):
    rows = 256

    def body(k, acc):
        r = pl.multiple_of(k * rows, rows)
        c = c_ref[pl.ds(r, rows), :]
        sc = c * jax.nn.sigmoid(c)
        return acc + jnp.sum(w_ref[pl.ds(r, rows), :] * sc, axis=0, keepdims=True)

    acc = lax.fori_loop(0, w_ref.shape[0] // rows, body, jnp.zeros(o_ref.shape, F32))
    o_ref[...] = acc + b_ref[...]


def _modulation(c_col, w, b):
    d, n = w.shape
    tn = 512
    return pl.pallas_call(
        _mod_kernel,
        out_shape=jax.ShapeDtypeStruct((1, n), F32),
        grid=(n // tn,),
        in_specs=[pl.BlockSpec((d, 1), lambda j: (0, 0)),
                  pl.BlockSpec((d, tn), lambda j: (0, j)),
                  pl.BlockSpec((1, tn), lambda j: (0, j))],
        out_specs=pl.BlockSpec((1, tn), lambda j: (0, j)),
        compiler_params=_params(("parallel",)),
        name="mod",
    )(c_col, w, b)


def _rope_kernel(pos_ref, invf_ref, cos_ref, sin_ref):
    ang = pos_ref[...] * invf_ref[...]
    cos_ref[...] = jnp.cos(ang)
    s = jnp.sin(ang)
    lane = lax.broadcasted_iota(jnp.int32, s.shape, 1)
    sin_ref[...] = jnp.where(lane < HEAD_DIM // 2, -s, s)


def _rope_tables(pos_col, invf_row):
    s = pos_col.shape[0]
    ts = 2048
    return pl.pallas_call(
        _rope_kernel,
        out_shape=(jax.ShapeDtypeStruct((s, HEAD_DIM), F32),) * 2,
        grid=(s // ts,),
        in_specs=[pl.BlockSpec((ts, 1), lambda i: (i, 0)),
                  pl.BlockSpec((1, HEAD_DIM), lambda i: (0, 0))],
        out_specs=(pl.BlockSpec((ts, HEAD_DIM), lambda i: (i, 0)),) * 2,
        compiler_params=_params(("parallel",)),
        name="rope",
    )(pos_col, invf_row)


def _norm_mod_kernel(x_ref, g_ref, scale_ref, shift_ref, o_ref):
    y = _rms(x_ref[...]) * g_ref[...]
    o_ref[...] = (y * (1.0 + scale_ref[...]) + shift_ref[...]).astype(o_ref.dtype)


def _norm_mod(x, g, scale, shift):
    s, d = x.shape
    tm = 512
    vec = pl.BlockSpec((1, d), lambda i: (0, 0))
    return pl.pallas_call(
        _norm_mod_kernel,
        out_shape=jax.ShapeDtypeStruct((s, d), BF16),
        grid=(s // tm,),
        in_specs=[pl.BlockSpec((tm, d), lambda i: (i, 0)), vec, vec, vec],
        out_specs=pl.BlockSpec((tm, d), lambda i: (i, 0)),
        compiler_params=_params(("parallel",)),
        name="norm_mod",
    )(x, g, scale, shift)


def _qkv_kernel(h_ref, w_ref, cos_ref, sin_ref, o_ref, stage_ref):
    acc = jnp.dot(h_ref[...], w_ref[...], preferred_element_type=F32)
    part = pl.program_id(1)

    heads = [slice(hh * HEAD_DIM, (hh + 1) * HEAD_DIM) for hh in range(HEADS_PER_GROUP)]

    @pl.when(part < 2)
    def _():
        cos = cos_ref[...]
        sin = sin_ref[...]
        for hh, cols in enumerate(heads):
            t = acc[:, cols]
            stage_ref[hh] = t * cos + pltpu.roll(t, HEAD_DIM // 2, axis=1) * sin

    @pl.when(part == 2)
    def _():
        for hh, cols in enumerate(heads):
            stage_ref[hh] = acc[:, cols]

    dilation, rows, _ = o_ref.shape
    for r in range(dilation):
        for hh, cols in enumerate(heads):
            o_ref[r, :, cols] = stage_ref[hh, pl.ds(r, rows, stride=dilation), :].astype(o_ref.dtype)


def _qkv_proj(h, w_in, cos, sin, group, dilation):
    s, d = h.shape
    tm = 1024
    return pl.pallas_call(
        _qkv_kernel,
        out_shape=jax.ShapeDtypeStruct((dilation, s // dilation, 3 * GROUP_WIDTH), BF16),
        grid=(s // tm, 3),
        in_specs=[pl.BlockSpec((tm, d), lambda i, j: (i, 0)),
                  pl.BlockSpec((d, COL), lambda i, j: (0, j * N_GROUPS + group)),
                  pl.BlockSpec((tm, HEAD_DIM), lambda i, j: (i, 0)),
                  pl.BlockSpec((tm, HEAD_DIM), lambda i, j: (i, 0))],
        out_specs=pl.BlockSpec((dilation, tm // dilation, COL), lambda i, j: (0, i, j)),
        scratch_shapes=[pltpu.VMEM((HEADS_PER_GROUP, tm, HEAD_DIM), F32)],
        compiler_params=_params(("parallel", "arbitrary")),
        name=f"qkv_d{dilation}",
    )(h, w_in, cos, sin)


def _glu_kernel(h_ref, wa_ref, wb_ref, o_ref):
    h = h_ref[...]
    a = jnp.dot(h, wa_ref[...], preferred_element_type=F32)
    b = jnp.dot(h, wb_ref[...], preferred_element_type=F32)
    o_ref[...] = a * jax.nn.sigmoid(b)


def _glu_proj(h, w_in, conv_channels):
    s, d = h.shape
    tm = 1024
    a0 = 3 * ATTN_WIDTH // COL
    nb = conv_channels // COL
    return pl.pallas_call(
        _glu_kernel,
        out_shape=jax.ShapeDtypeStruct((s, conv_channels), F32),
        grid=(s // tm, nb),
        in_specs=[pl.BlockSpec((tm, d), lambda i, j: (i, 0)),
                  pl.BlockSpec((d, COL), lambda i, j: (0, a0 + j)),
                  pl.BlockSpec((d, COL), lambda i, j: (0, a0 + nb + j))],
        out_specs=pl.BlockSpec((tm, COL), lambda i, j: (i, j)),
        compiler_params=_params(("parallel", "arbitrary")),
        name="glu",
    )(h, w_in, w_in)


def _gates_kernel(h_ref, w_ref, o_ref):
    o_ref[...] = jax.nn.sigmoid(jnp.dot(h_ref[...], w_ref[...], preferred_element_type=F32))


def _gates_proj(h, w_in, col0, width):
    s, d = h.shape
    tm = 1024
    c0 = col0 // COL
    return pl.pallas_call(
        _gates_kernel,
        out_shape=jax.ShapeDtypeStruct((s, width), F32),
        grid=(s // tm, width // COL),
        in_specs=[pl.BlockSpec((tm, d), lambda i, j: (i, 0)),
                  pl.BlockSpec((d, COL), lambda i, j: (0, c0 + j))],
        out_specs=pl.BlockSpec((tm, COL), lambda i, j: (i, j)),
        compiler_params=_params(("parallel", "arbitrary")),
        name="gates",
    )(h, w_in)


ATTN_Q_BLOCKS = 4


def _attn_kernel(q_ref, kc_ref, kp_ref, vc_ref, vp_ref, o_ref, l_ref):
    n = pl.program_id(1)
    qi = lax.broadcasted_iota(jnp.int32, (ATTN_BLOCK, 2 * ATTN_BLOCK), 0)
    kj = lax.broadcasted_iota(jnp.int32, (ATTN_BLOCK, 2 * ATTN_BLOCK), 1)
    band = jnp.abs(kj - qi - ATTN_BLOCK // 2) <= ATTN_BLOCK // 2
    first_lo = jnp.where(n == 0, ATTN_BLOCK, 0)
    band_first = band & (kj >= first_lo)
    scale = HEAD_DIM ** -0.5
    for hh in range(HEADS_PER_GROUP):
        cols = slice(hh * HEAD_DIM, (hh + 1) * HEAD_DIM)
        for b in range(ATTN_Q_BLOCKS):
            rows = slice(b * ATTN_BLOCK, (b + 1) * ATTN_BLOCK)
            q = q_ref[rows, cols]
            if b == 0:
                k = jnp.concatenate([kp_ref[:, cols], kc_ref[0:ATTN_BLOCK, cols]], axis=0)
                v = jnp.concatenate([vp_ref[:, cols], vc_ref[0:ATTN_BLOCK, cols]], axis=0)
                valid = band_first
            else:
                band_rows = slice((b - 1) * ATTN_BLOCK, (b + 1) * ATTN_BLOCK)
                k = kc_ref[band_rows, cols]
                v = vc_ref[band_rows, cols]
                valid = band
            s = lax.dot_general(q, k, (((1,), (1,)), ((), ())), preferred_element_type=F32) * scale
            s = jnp.where(valid, s, MASK_VALUE)
            m = jnp.max(s, axis=-1, keepdims=True)
            p = jnp.exp(s - m)
            den = jnp.sum(p, axis=-1, keepdims=True)
            o = jnp.dot(p.astype(BF16), v, preferred_element_type=F32)
            o_ref[rows, cols] = o * (1.0 / den)
            l_ref[rows, cols] = jnp.broadcast_to(m + jnp.log(den), (ATTN_BLOCK, HEAD_DIM))


def _dilated_attention(qkv):
    dilation, sub_len, _ = qkv.shape
    qt = ATTN_Q_BLOCKS * ATTN_BLOCK

    def cur(part):
        return pl.BlockSpec((None, qt, COL), lambda r, n: (r, n, part))

    def prev(part):
        return pl.BlockSpec((None, ATTN_BLOCK, COL),
                            lambda r, n: (r, jnp.maximum(n * ATTN_Q_BLOCKS - 1, 0), part))

    out_sds = jax.ShapeDtypeStruct((dilation, sub_len, GROUP_WIDTH), F32)
    out_spec = pl.BlockSpec((None, qt, GROUP_WIDTH), lambda r, n: (r, n, 0))
    return pl.pallas_call(
        _attn_kernel,
        out_shape=(out_sds, out_sds),
        grid=(dilation, sub_len // qt),
        in_specs=[cur(0), cur(1), prev(1), cur(2), prev(2)],
        out_specs=(out_spec, out_spec),
        compiler_params=_params(("parallel", "arbitrary")),
        name=f"attn_d{dilation}",
    )(qkv, qkv, qkv, qkv, qkv)


def _mix_kernel(o0_ref, o1_ref, o2_ref, l0_ref, l1_ref, l2_ref, u_ref, uh_ref, ga_ref, gc_ref,
                wao_ref, cw_ref, cb_ref, lg_ref, lb_ref, wco_ref, out_ref, ucat_ref, nat_ref, shift_ref):
    i = pl.program_id(0)
    tm = u_ref.shape[0]

    def natural(ref, slot):
        dilation, rows, _ = ref.shape
        if dilation == 1:
            return ref[0]
        for hh in range(HEADS_PER_GROUP):
            for r in range(dilation):
                nat_ref[slot, hh, pl.ds(r, rows, stride=dilation), :] = ref[r, :, hh * HEAD_DIM:(hh + 1) * HEAD_DIM]
        return jnp.concatenate([nat_ref[slot, hh] for hh in range(HEADS_PER_GROUP)], axis=1)

    l0, l1, l2 = natural(l0_ref, 0), natural(l1_ref, 1), natural(l2_ref, 2)
    m = jnp.maximum(jnp.maximum(l0, l1), l2)
    e0, e1, e2 = jnp.exp(l0 - m), jnp.exp(l1 - m), jnp.exp(l2 - m)
    attn = (e0 * natural(o0_ref, 3) + e1 * natural(o1_ref, 4) + e2 * natural(o2_ref, 5)) / (e0 + e1 + e2)
    attn_o = jnp.dot(attn.astype(BF16), wao_ref[...], preferred_element_type=F32)
    ucat_ref[0:CONV_HALO, :] = jnp.where(i > 0, uh_ref[...], 0.0)
    ucat_ref[CONV_HALO:, :] = u_ref[...]
    off = CONV_HALO - (CONV_WIDTH - 1)
    span = tm + CONV_HALO - SUBLANES
    for s in range(1, SUBLANES):
        shift_ref[s - 1] = ucat_ref[s:s + span, :]
    conv = jnp.zeros(u_ref.shape, F32) + cb_ref[...]
    for w in range(CONV_WIDTH):
        q, s = divmod(off + w, SUBLANES)
        src = ucat_ref if s == 0 else shift_ref.at[s - 1]
        conv = conv + src[q * SUBLANES:q * SUBLANES + tm, :] * cw_ref[w:w + 1, :]
    mu = jnp.mean(conv, axis=-1, keepdims=True)
    cen = conv - mu
    var = jnp.mean(cen * cen, axis=-1, keepdims=True)
    y = cen * lax.rsqrt(var + NORM_EPS) * lg_ref[...] + lb_ref[...]
    y = y * jax.nn.sigmoid(y)
    conv_o = jnp.dot(y.astype(BF16), wco_ref[...], preferred_element_type=F32)
    out_ref[...] = (ga_ref[...] * attn_o + gc_ref[...] * conv_o).astype(out_ref.dtype)


def _mix(outs, lses, u, gates, w_attn_o, conv_w, conv_b, ln_g, ln_b, w_conv_o):
    s, cc = u.shape
    d = w_attn_o.shape[1]
    tm = 256
    row = lambda i: (i, 0)
    const = lambda i: (0, 0)
    cvec = pl.BlockSpec((1, cc), const)

    def grp(arr):
        dilation = arr.shape[0]
        return pl.BlockSpec((dilation, tm // dilation, GROUP_WIDTH), lambda i: (0, i, 0))

    return pl.pallas_call(
        _mix_kernel,
        out_shape=jax.ShapeDtypeStruct((s, d), BF16),
        grid=(s // tm,),
        in_specs=[grp(a) for a in (*outs, *lses)] + [
            pl.BlockSpec((tm, cc), row),
            pl.BlockSpec((CONV_HALO, cc), lambda i: (jnp.maximum(i * (tm // CONV_HALO) - 1, 0), 0)),
            pl.BlockSpec((tm, d), lambda i: (i, 0)),
            pl.BlockSpec((tm, d), lambda i: (i, 1)),
            pl.BlockSpec((GROUP_WIDTH, d), const),
            pl.BlockSpec((CONV_WIDTH, cc), const),
            cvec, cvec, cvec,
            pl.BlockSpec((cc, d), const)],
        out_specs=pl.BlockSpec((tm, d), row),
        scratch_shapes=[pltpu.VMEM((tm + CONV_HALO, cc), F32),
                        pltpu.VMEM((2 * N_GROUPS, HEADS_PER_GROUP, tm, HEAD_DIM), F32),
                        pltpu.VMEM((SUBLANES - 1, tm + CONV_HALO - SUBLANES, cc), F32)],
        compiler_params=_params(("parallel",)),
        name="mix",
    )(*outs, *lses, u, u, gates, gates, w_attn_o, conv_w, conv_b, ln_g, ln_b, w_conv_o)


def _out_proj_kernel(m_ref, w_ref, x_ref, gate_ref, g1_ref, g2_ref, scale_ref, shift_ref,
                     x1_ref, h2t_ref):
    y = jnp.dot(m_ref[...], w_ref[...], preferred_element_type=F32)
    x1 = x_ref[...] + gate_ref[...] * (_rms(y) * g1_ref[...])
    x1_ref[...] = x1
    h2 = (_rms(x1) * g2_ref[...]) * (1.0 + scale_ref[...]) + shift_ref[...]
    h2t_ref[...] = h2.T.astype(h2t_ref.dtype)


def _out_proj(merged, w_out, x, gate1, g1, g2, scale2, shift2):
    s, d = x.shape
    tm = 256
    row = lambda i: (i, 0)
    vec = pl.BlockSpec((1, d), lambda i: (0, 0))
    return pl.pallas_call(
        _out_proj_kernel,
        out_shape=(jax.ShapeDtypeStruct((s, d), F32), jax.ShapeDtypeStruct((d, s), BF16)),
        grid=(s // tm,),
        in_specs=[pl.BlockSpec((tm, d), row), pl.BlockSpec((d, d), lambda i: (0, 0)),
                  pl.BlockSpec((tm, d), row), vec, vec, vec, vec, vec],
        out_specs=(pl.BlockSpec((tm, d), row), pl.BlockSpec((d, tm), lambda i: (0, i))),
        compiler_params=_params(("parallel",)),
        name="out_proj",
    )(merged, w_out, x, gate1, g1, g2, scale2, shift2)


def _odd_even_merge_sort(lo, hi):
    def merge(lo, hi, r):
        step = 2 * r
        if step < hi - lo:
            yield from merge(lo, hi, step)
            yield from merge(lo + r, hi, step)
            for i in range(lo + r, hi - r, step):
                yield (i, i + r)
        else:
            yield (lo, lo + r)

    if hi > lo:
        mid = lo + (hi - lo) // 2
        yield from _odd_even_merge_sort(lo, mid)
        yield from _odd_even_merge_sort(mid + 1, hi)
        yield from merge(lo, hi, 1)


def _bitonic_merge(n):
    half = n // 2
    while half >= 1:
        for i in range(n):
            if (i // half) % 2 == 0:
                yield (i, i + half)
        half //= 2


_SORT_PAIRS = {n: tuple(_odd_even_merge_sort(0, n - 1)) for n in (PEER_TOPK // 2, PEER_TOPK)}
_MERGE_PAIRS = tuple(_bitonic_merge(PEER_TOPK))


def _compare_exchange(v, pairs):
    for i, j in pairs:
        v[i], v[j] = jnp.maximum(v[i], v[j]), jnp.minimum(v[i], v[j])


def _top16_rows(x):
    blocks = x.shape[0] // SUBLANES
    assert x.shape[0] == blocks * SUBLANES and blocks <= PEER_TOPK
    n = PEER_TOPK // 2 if blocks <= PEER_TOPK // 2 else PEER_TOPK
    v = [x[k * SUBLANES:(k + 1) * SUBLANES, :] for k in range(blocks)]
    v += [jnp.full_like(v[0], NEG_INF)] * (n - blocks)
    _compare_exchange(v, _SORT_PAIRS[n])
    shift = SUBLANES // 2
    while shift >= 1:
        partner = [pltpu.roll(blk, shift, axis=0) for blk in v]
        if len(v) < PEER_TOPK:
            v = v + partner[::-1]
        else:
            v = [jnp.maximum(v[k], partner[PEER_TOPK - 1 - k]) for k in range(PEER_TOPK)]
        _compare_exchange(v, _MERGE_PAIRS)
        shift //= 2
    return [blk[0:1, :] for blk in v]


def _peer_query_kernel(h2t_ref, wqt_ref, keys_ref, e1_ref, e2_ref, theta_ref,
                       qt_ref, top_ref, cand_ref):
    qt_ref[...] = jnp.dot(wqt_ref[...], h2t_ref[...], preferred_element_type=F32)
    for h in range(PEER_HEADS):
        u = []
        for side in range(2):
            r0 = (2 * h + side) * PEER_HALF
            qc = qt_ref[r0:r0 + PEER_HALF, :].astype(BF16)
            sc = jnp.dot(keys_ref[h, side].astype(BF16), qc, preferred_element_type=F32)
            us = jnp.exp(sc - jnp.max(sc, axis=0, keepdims=True))
            vals = _top16_rows(us)
            for k in range(PEER_TOPK):
                top_ref[side, k:k + 1, :] = vals[k]
            u.append(us)
        cand_ref[...] = jnp.full(cand_ref.shape, -1.0, F32)
        off = 0
        for a, nb in enumerate(_CAND_ROWS):
            cand_ref[off:off + nb, :] = top_ref[0, a:a + 1, :] * top_ref[1, 0:nb, :]
            off += nb
        cand = cand_ref[...]
        selected = cand >= _top16_rows(cand)[-1]
        inv_z = 0.5 / jnp.sum(jnp.where(selected, cand, 0.0), axis=0, keepdims=True)
        off = 0
        for a, nb in enumerate(_CAND_ROWS):
            cand_ref[off:off + nb, :] = (top_ref[0, a:a + 1, :] * inv_z) * top_ref[1, 0:nb, :]
            off += nb
        theta = jnp.min(jnp.where(selected, cand_ref[...], jnp.inf), axis=0, keepdims=True)
        e1 = u[0] * inv_z
        for blk in range(PEER_N_KEYS // 8):
            e1_ref[h, blk] = e1[blk * 8:(blk + 1) * 8, :]
        e2_ref[h] = u[1]
        theta_ref[h:h + 1, :] = theta


def _peer_query(h2t, wq_t, sub_keys):
    d, s = h2t.shape
    tt = 512
    nq = wq_t.shape[0]
    side_blocked = jax.ShapeDtypeStruct((PEER_HEADS, PEER_N_KEYS // 8, 8, s), F32)
    side_flat = jax.ShapeDtypeStruct((PEER_HEADS, PEER_N_KEYS, s), F32)
    blocked_spec = pl.BlockSpec((PEER_HEADS, PEER_N_KEYS // 8, 8, tt), lambda t: (0, 0, 0, t))
    flat_spec = pl.BlockSpec((PEER_HEADS, PEER_N_KEYS, tt), lambda t: (0, 0, t))
    return pl.pallas_call(
        _peer_query_kernel,
        out_shape=(side_blocked, side_flat, jax.ShapeDtypeStruct((PEER_HEADS, s), F32)),
        grid=(s // tt,),
        in_specs=[pl.BlockSpec((d, tt), lambda t: (0, t)),
                  pl.BlockSpec((nq, d), lambda t: (0, 0)),
                  pl.BlockSpec(sub_keys.shape, lambda t: (0, 0, 0, 0))],
        out_specs=(blocked_spec, flat_spec, pl.BlockSpec((PEER_HEADS, tt), lambda t: (0, t))),
        scratch_shapes=[pltpu.VMEM((nq, tt), F32),
                        pltpu.VMEM((2, PEER_TOPK, tt), F32),
                        pltpu.VMEM((_CAND_PAD, tt), F32)],
        compiler_params=_params(("parallel",)),
        name="peer_query",
    )(h2t, wq_t, sub_keys)


PEER_LANE_CHUNK = 128
PEER_HALF_KEYS = 4
PEER_HALF_TILE = PEER_HALF_KEYS * PEER_N_KEYS
PEER_TILE = 2 * PEER_HALF_TILE
PEER_TOKENS = 512


def _gate_act(hid_ref, act_ref, e1_ref, e2_ref, theta_ref, half):
    tt = hid_ref.shape[2]
    for ii in range(PEER_HALF_KEYS):
        rows = slice(ii * PEER_N_KEYS, (ii + 1) * PEER_N_KEYS)
        key = half * PEER_HALF_KEYS + ii
        for c in range(tt // PEER_LANE_CHUNK):
            lanes = slice(c * PEER_LANE_CHUNK, (c + 1) * PEER_LANE_CHUNK)
            w = None
            for h in range(PEER_HEADS):
                p = e1_ref[h, 0, key:key + 1, lanes] * e2_ref[h, :, lanes]
                kept = jnp.where(p >= theta_ref[h:h + 1, lanes], p, 0.0)
                w = kept if w is None else w + kept
            act_ref[half, rows, lanes] = (hid_ref[half, rows, lanes] * w).astype(act_ref.dtype)


def _project(down, h2t):
    x = jnp.dot(down, h2t, preferred_element_type=F32)
    return (x * (1.0 + lax.erf(x * INV_SQRT2))).reshape(2, PEER_HALF_TILE, h2t.shape[1])


def _peer_project_kernel(down_ref, h2t_ref, hid_ref):
    hid_ref[...] = _project(down_ref[...], h2t_ref[...])


def _peer_project_first(h2t, down):
    d = h2t.shape[0]
    return pl.pallas_call(
        _peer_project_kernel,
        out_shape=jax.ShapeDtypeStruct((2, PEER_HALF_TILE, PEER_TOKENS), F32),
        grid=(1,),
        in_specs=[pl.BlockSpec((PEER_TILE, d), lambda i: (0, 0)),
                  pl.BlockSpec((d, PEER_TOKENS), lambda i: (0, 0))],
        out_specs=pl.BlockSpec((2, PEER_HALF_TILE, PEER_TOKENS), lambda i: (0, 0, 0)),
        compiler_params=_params(("arbitrary",)),
        name="peer_project_first",
    )(down, h2t)


def _peer_dense_kernel(h2t_ref, down_ref, up_ref, e1_ref, e2_ref, theta_ref, hid0_ref,
                       x1_ref, gate_ref, g_ref, out_ref, hid_ref, act_ref):
    t = pl.program_id(0)
    e = pl.program_id(1)

    @pl.when((t == 0) & (e == 0))
    def _():
        hid_ref[...] = hid0_ref[...]

    @pl.when(e == 0)
    def _():
        out_ref[...] = jnp.zeros(out_ref.shape, F32)

    for half in range(2):
        _gate_act(hid_ref, act_ref, e1_ref, e2_ref, theta_ref, half)
    act = act_ref[...].reshape(PEER_TILE, act_ref.shape[2])
    out_ref[...] += lax.dot_general(act, up_ref[...], (((0,), (0,)), ((), ())), preferred_element_type=F32)
    hid_ref[...] = _project(down_ref[...], h2t_ref[...])

    @pl.when(e == pl.num_programs(1) - 1)
    def _():
        y = out_ref[...]
        out_ref[...] = x1_ref[...] + gate_ref[...] * (_rms(y) * g_ref[...])


def _peer_dense(h2t, down, up, e1, e2, theta, x1, gate2, g3):
    d, s = h2t.shape
    tt = PEER_TOKENS
    n_e = down.shape[0] // PEER_TILE
    n_t = s // tt
    vec = pl.BlockSpec((1, d), lambda t, e: (0, 0))
    next_e = lambda e: (e + 1) % n_e
    next_t = lambda t, e: jnp.minimum(t + (e + 1) // n_e, n_t - 1)
    return pl.pallas_call(
        _peer_dense_kernel,
        out_shape=jax.ShapeDtypeStruct((s, d), F32),
        grid=(n_t, n_e),
        in_specs=[pl.BlockSpec((d, tt), lambda t, e: (0, next_t(t, e))),
                  pl.BlockSpec((PEER_TILE, d), lambda t, e: (next_e(e), 0)),
                  pl.BlockSpec((PEER_TILE, d), lambda t, e: (e, 0)),
                  pl.BlockSpec((PEER_HEADS, 1, 2 * PEER_HALF_KEYS, tt), lambda t, e: (0, e, 0, t)),
                  pl.BlockSpec((PEER_HEADS, PEER_N_KEYS, tt), lambda t, e: (0, 0, t)),
                  pl.BlockSpec((PEER_HEADS, tt), lambda t, e: (0, t)),
                  pl.BlockSpec((2, PEER_HALF_TILE, tt), lambda t, e: (0, 0, 0)),
                  pl.BlockSpec((tt, d), lambda t, e: (t, 0)),
                  vec, vec],
        out_specs=pl.BlockSpec((tt, d), lambda t, e: (t, 0)),
        scratch_shapes=[pltpu.VMEM((2, PEER_HALF_TILE, tt), F32),
                        pltpu.VMEM((2, PEER_HALF_TILE, tt), BF16)],
        compiler_params=_params(("arbitrary", "arbitrary")),
        name="peer_dense",
    )(h2t, down, up, e1, e2, theta, _peer_project_first(h2t, down), x1, gate2, g3)


def kernel(x, c, positions, ada_w, ada_b, norm_gains, w_in, w_attn_o, conv_w, conv_b, conv_ln_g,
           conv_ln_b, w_conv_o, w_out, peer_w_q, peer_sub_keys, peer_down, peer_up):
    batch, seq, d = x.shape
    depth = ada_w.shape[0]
    assert batch == 1, "kernels are written for a single sequence"
    assert all(window // dilation == ATTN_BLOCK for window, dilation in DILATED_GROUPS)
    cc = conv_w.shape[-1]
    xs = x[0]
    inv_freq = ROPE_THETA ** (-jnp.arange(0, HEAD_DIM, 2, dtype=F32) / HEAD_DIM)
    invf_row = jnp.concatenate([inv_freq, inv_freq])[None, :]
    cos, sin = _rope_tables(positions[0].astype(F32)[:, None], invf_row)
    row = lambda v: v[None, :]
    for l in range(depth):
        mod = _modulation(c[0][:, None], ada_w[l], ada_b[l][None, :])
        shift1, scale1, gate1, shift2, scale2, gate2 = [mod[:, k * d:(k + 1) * d] for k in range(6)]
        g = norm_gains[l]
        w_in_b = w_in[l].astype(BF16)

        h = _norm_mod(xs, row(g[0]), scale1, shift1)
        u = _glu_proj(h, w_in_b, cc)
        gates = _gates_proj(h, w_in_b, 3 * ATTN_WIDTH + 2 * cc, 2 * d)
        outs, lses = [], []
        for gi, (_, dilation) in enumerate(DILATED_GROUPS):
            o, lse = _dilated_attention(_qkv_proj(h, w_in_b, cos, sin, gi, dilation))
            outs.append(o)
            lses.append(lse)
        merged = _mix(outs, lses, u, gates, w_attn_o[l].astype(BF16), conv_w[l][:, 0, :],
                      row(conv_b[l]), row(conv_ln_g[l]), row(conv_ln_b[l]), w_conv_o[l].astype(BF16))
        x1, h2t = _out_proj(merged, w_out[l].astype(BF16), xs, gate1, row(g[1]), row(g[2]),
                            scale2, shift2)

        e1, e2, theta = _peer_query(h2t, peer_w_q[l].T.astype(BF16), peer_sub_keys[l])
        xs = _peer_dense(h2t, peer_down[l].astype(BF16), peer_up[l].astype(BF16),
                         e1, e2, theta, x1, gate2, row(g[3]))
    return xs[None]
```

```python
import functools
import math

import jax
import jax.numpy as jnp
from jax import lax
from jax.experimental import pallas as pl
from jax.experimental.pallas import tpu as pltpu

F32 = jnp.float32
BF16 = jnp.bfloat16

HEAD_DIM = 128
HEADS_PER_GROUP = 4
DILATED_GROUPS = ((128, 1), (512, 4), (2048, 16))
N_GROUPS = len(DILATED_GROUPS)
GROUP_WIDTH = HEADS_PER_GROUP * HEAD_DIM
ATTN_WIDTH = N_GROUPS * GROUP_WIDTH
ATTN_BLOCK = 128
ROPE_THETA = 10000.0
CONV_WIDTH = 31
CONV_HALO = 32
PEER_HEADS = 8
PEER_N_KEYS = 128
PEER_HALF = 128
PEER_TOPK = 16
NORM_EPS = 1e-6
MASK_VALUE = -1e30
NEG_INF = float("-inf")
INV_SQRT2 = 1.0 / math.sqrt(2.0)

SUBLANES = 8
COL = 512
PROJ_ROWS = 2048
VMEM_LIMIT = 56 * 1024 * 1024

_CAND_ROWS = [PEER_TOPK // (a + 1) for a in range(PEER_TOPK)]
_N_CAND = sum(_CAND_ROWS)
_CAND_PAD = -(-_N_CAND // 8) * 8


def _params(sem):
    return pltpu.CompilerParams(dimension_semantics=sem, vmem_limit_bytes=VMEM_LIMIT)


def _rms(x):
    return x * lax.rsqrt(jnp.mean(x * x, axis=-1, keepdims=True) + NORM_EPS)


def _mod_kernel(c_ref, w_ref, b_ref, o_ref):
    rows = 256

    def body(k, acc):
        r = pl.multiple_of(k * rows, rows)
        c = c_ref[pl.ds(r, rows), :]
        sc = c * jax.nn.sigmoid(c)
        return acc + jnp.sum(w_ref[pl.ds(r, rows), :] * sc, axis=0, keepdims=True)

    acc = lax.fori_loop(0, w_ref.shape[0] // rows, body, jnp.zeros(o_ref.shape, F32))
    o_ref[...] = acc + b_ref[...]


def _modulation(c_col, w, b):
    d, n = w.shape
    tn = 512
    return pl.pallas_call(
        _mod_kernel,
        out_shape=jax.ShapeDtypeStruct((1, n), F32),
        grid=(n // tn,),
        in_specs=[pl.BlockSpec((d, 1), lambda j: (0, 0)),
                  pl.BlockSpec((d, tn), lambda j: (0, j)),
                  pl.BlockSpec((1, tn), lambda j: (0, j))],
        out_specs=pl.BlockSpec((1, tn), lambda j: (0, j)),
        compiler_params=_params(("parallel",)),
        name="mod",
    )(c_col, w, b)


def _rope_kernel(pos_ref, invf_ref, cos_ref, sin_ref):
    ang = pos_ref[...] * invf_ref[...]
    cos_ref[...] = jnp.cos(ang)
    s = jnp.sin(ang)
    lane = lax.broadcasted_iota(jnp.int32, s.shape, 1)
    sin_ref[...] = jnp.where(lane < HEAD_DIM // 2, -s, s)


def _rope_tables(pos_col, invf_row):
    s = pos_col.shape[0]
    ts = 2048
    return pl.pallas_call(
        _rope_kernel,
        out_shape=(jax.ShapeDtypeStruct((s, HEAD_DIM), F32),) * 2,
        grid=(s // ts,),
        in_specs=[pl.BlockSpec((ts, 1), lambda i: (i, 0)),
                  pl.BlockSpec((1, HEAD_DIM), lambda i: (0, 0))],
        out_specs=(pl.BlockSpec((ts, HEAD_DIM), lambda i: (i, 0)),) * 2,
        compiler_params=_params(("parallel",)),
        name="rope",
    )(pos_col, invf_row)


def _norm_mod_kernel(x_ref, g_ref, scale_ref, shift_ref, o_ref):
    y = _rms(x_ref[...]) * g_ref[...]
    o_ref[...] = (y * (1.0 + scale_ref[...]) + shift_ref[...]).astype(o_ref.dtype)


def _norm_mod(x, g, scale, shift):
    s, d = x.shape
    tm = 512
    vec = pl.BlockSpec((1, d), lambda i: (0, 0))
    return pl.pallas_call(
        _norm_mod_kernel,
        out_shape=jax.ShapeDtypeStruct((s, d), BF16),
        grid=(s // tm,),
        in_specs=[pl.BlockSpec((tm, d), lambda i: (i, 0)), vec, vec, vec],
        out_specs=pl.BlockSpec((tm, d), lambda i: (i, 0)),
        compiler_params=_params(("parallel",)),
        name="norm_mod",
    )(x, g, scale, shift)


def _qkv_kernel(h_ref, w_ref, cos_ref, sin_ref, o_ref, stage_ref):
    acc = jnp.dot(h_ref[...], w_ref[...], preferred_element_type=F32)
    part = pl.program_id(1)

    heads = [slice(hh * HEAD_DIM, (hh + 1) * HEAD_DIM) for hh in range(HEADS_PER_GROUP)]

    @pl.when(part < 2)
    def _():
        cos = cos_ref[...]
        sin = sin_ref[...]
        for hh, cols in enumerate(heads):
            t = acc[:, cols]
            stage_ref[hh] = t * cos + pltpu.roll(t, HEAD_DIM // 2, axis=1) * sin

    @pl.when(part == 2)
    def _():
        for hh, cols in enumerate(heads):
            stage_ref[hh] = acc[:, cols]

    dilation, rows, _ = o_ref.shape
    for r in range(dilation):
        for hh, cols in enumerate(heads):
            o_ref[r, :, cols] = stage_ref[hh, pl.ds(r, rows, stride=dilation), :].astype(o_ref.dtype)


def _qkv_proj(h, w_in, cos, sin, group, dilation):
    s, d = h.shape
    tm = PROJ_ROWS
    return pl.pallas_call(
        _qkv_kernel,
        out_shape=jax.ShapeDtypeStruct((dilation, s // dilation, 3 * GROUP_WIDTH), BF16),
        grid=(s // tm, 3),
        in_specs=[pl.BlockSpec((tm, d), lambda i, j: (i, 0)),
                  pl.BlockSpec((d, COL), lambda i, j: (0, j * N_GROUPS + group)),
                  pl.BlockSpec((tm, HEAD_DIM), lambda i, j: (i, 0)),
                  pl.BlockSpec((tm, HEAD_DIM), lambda i, j: (i, 0))],
        out_specs=pl.BlockSpec((dilation, tm // dilation, COL), lambda i, j: (0, i, j)),
        scratch_shapes=[pltpu.VMEM((HEADS_PER_GROUP, tm, HEAD_DIM), F32)],
        compiler_params=_params(("parallel", "arbitrary")),
        name=f"qkv_d{dilation}",
    )(h, w_in, cos, sin)


def _glu_kernel(h_ref, wa_ref, wb_ref, o_ref):
    h = h_ref[...]
    a = jnp.dot(h, wa_ref[...], preferred_element_type=F32)
    b = jnp.dot(h, wb_ref[...], preferred_element_type=F32)
    o_ref[...] = a * jax.nn.sigmoid(b)


def _glu_proj(h, w_in, conv_channels):
    s, d = h.shape
    tm = PROJ_ROWS
    a0 = 3 * ATTN_WIDTH // COL
    nb = conv_channels // COL
    return pl.pallas_call(
        _glu_kernel,
        out_shape=jax.ShapeDtypeStruct((s, conv_channels), F32),
        grid=(s // tm, nb),
        in_specs=[pl.BlockSpec((tm, d), lambda i, j: (i, 0)),
                  pl.BlockSpec((d, COL), lambda i, j: (0, a0 + j)),
                  pl.BlockSpec((d, COL), lambda i, j: (0, a0 + nb + j))],
        out_specs=pl.BlockSpec((tm, COL), lambda i, j: (i, j)),
        compiler_params=_params(("parallel", "arbitrary")),
        name="glu",
    )(h, w_in, w_in)


def _gates_kernel(h_ref, w_ref, o_ref):
    o_ref[...] = jax.nn.sigmoid(jnp.dot(h_ref[...], w_ref[...], preferred_element_type=F32))


def _gates_proj(h, w_in, col0, width):
    s, d = h.shape
    tm = PROJ_ROWS
    c0 = col0 // COL
    return pl.pallas_call(
        _gates_kernel,
        out_shape=jax.ShapeDtypeStruct((s, width), F32),
        grid=(s // tm, width // COL),
        in_specs=[pl.BlockSpec((tm, d), lambda i, j: (i, 0)),
                  pl.BlockSpec((d, COL), lambda i, j: (0, c0 + j))],
        out_specs=pl.BlockSpec((tm, COL), lambda i, j: (i, j)),
        compiler_params=_params(("parallel", "arbitrary")),
        name="gates",
    )(h, w_in)


ATTN_Q_BLOCKS = 4


def _attn_kernel(q_ref, kc_ref, kp_ref, vc_ref, vp_ref, o_ref, l_ref):
    n = pl.program_id(1)
    qi = lax.broadcasted_iota(jnp.int32, (ATTN_BLOCK, 2 * ATTN_BLOCK), 0)
    kj = lax.broadcasted_iota(jnp.int32, (ATTN_BLOCK, 2 * ATTN_BLOCK), 1)
    band = jnp.abs(kj - qi - ATTN_BLOCK // 2) <= ATTN_BLOCK // 2
    first_lo = jnp.where(n == 0, ATTN_BLOCK, 0)
    band_first = band & (kj >= first_lo)
    scale = HEAD_DIM ** -0.5
    for hh in range(HEADS_PER_GROUP):
        cols = slice(hh * HEAD_DIM, (hh + 1) * HEAD_DIM)
        for b in range(ATTN_Q_BLOCKS):
            rows = slice(b * ATTN_BLOCK, (b + 1) * ATTN_BLOCK)
            q = q_ref[rows, cols]
            if b == 0:
                k = jnp.concatenate([kp_ref[:, cols], kc_ref[0:ATTN_BLOCK, cols]], axis=0)
                v = jnp.concatenate([vp_ref[:, cols], vc_ref[0:ATTN_BLOCK, cols]], axis=0)
                valid = band_first
            else:
                band_rows = slice((b - 1) * ATTN_BLOCK, (b + 1) * ATTN_BLOCK)
                k = kc_ref[band_rows, cols]
                v = vc_ref[band_rows, cols]
                valid = band
            s = lax.dot_general(q, k, (((1,), (1,)), ((), ())), preferred_element_type=F32) * scale
            s = jnp.where(valid, s, MASK_VALUE)
            m = jnp.max(s, axis=-1, keepdims=True)
            p = jnp.exp(s - m)
            den = jnp.sum(p, axis=-1, keepdims=True)
            o = jnp.dot(p.astype(BF16), v, preferred_element_type=F32)
            o_ref[rows, cols] = o * (1.0 / den)
            l_ref[rows, cols] = jnp.broadcast_to(m + jnp.log(den), (ATTN_BLOCK, HEAD_DIM))


def _dilated_attention(qkv):
    dilation, sub_len, _ = qkv.shape
    qt = ATTN_Q_BLOCKS * ATTN_BLOCK

    def cur(part):
        return pl.BlockSpec((None, qt, COL), lambda r, n: (r, n, part))

    def prev(part):
        return pl.BlockSpec((None, ATTN_BLOCK, COL),
                            lambda r, n: (r, jnp.maximum(n * ATTN_Q_BLOCKS - 1, 0), part))

    out_sds = jax.ShapeDtypeStruct((dilation, sub_len, GROUP_WIDTH), F32)
    out_spec = pl.BlockSpec((None, qt, GROUP_WIDTH), lambda r, n: (r, n, 0))
    return pl.pallas_call(
        _attn_kernel,
        out_shape=(out_sds, out_sds),
        grid=(dilation, sub_len // qt),
        in_specs=[cur(0), cur(1), prev(1), cur(2), prev(2)],
        out_specs=(out_spec, out_spec),
        compiler_params=_params(("parallel", "arbitrary")),
        name=f"attn_d{dilation}",
    )(qkv, qkv, qkv, qkv, qkv)


def _mix_kernel(o0_ref, o1_ref, o2_ref, l0_ref, l1_ref, l2_ref, u_ref, uh_ref, ga_ref, gc_ref,
                wao_ref, cw_ref, cb_ref, lg_ref, lb_ref, wco_ref, out_ref, ucat_ref, nat_ref, shift_ref):
    i = pl.program_id(0)
    tm = u_ref.shape[0]

    def natural(ref, slot):
        dilation, rows, _ = ref.shape
        if dilation == 1:
            return ref[0]
        for hh in range(HEADS_PER_GROUP):
            for r in range(dilation):
                nat_ref[slot, hh, pl.ds(r, rows, stride=dilation), :] = ref[r, :, hh * HEAD_DIM:(hh + 1) * HEAD_DIM]
        return jnp.concatenate([nat_ref[slot, hh] for hh in range(HEADS_PER_GROUP)], axis=1)

    l0, l1, l2 = natural(l0_ref, 0), natural(l1_ref, 1), natural(l2_ref, 2)
    m = jnp.maximum(jnp.maximum(l0, l1), l2)
    e0, e1, e2 = jnp.exp(l0 - m), jnp.exp(l1 - m), jnp.exp(l2 - m)
    attn = (e0 * natural(o0_ref, 3) + e1 * natural(o1_ref, 4) + e2 * natural(o2_ref, 5)) / (e0 + e1 + e2)
    attn_o = jnp.dot(attn.astype(BF16), wao_ref[...], preferred_element_type=F32)
    ucat_ref[0:CONV_HALO, :] = jnp.where(i > 0, uh_ref[...], 0.0)
    ucat_ref[CONV_HALO:, :] = u_ref[...]
    off = CONV_HALO - (CONV_WIDTH - 1)
    span = tm + CONV_HALO - SUBLANES
    for s in range(1, SUBLANES):
        shift_ref[s - 1] = ucat_ref[s:s + span, :]
    conv = jnp.zeros(u_ref.shape, F32) + cb_ref[...]
    for w in range(CONV_WIDTH):
        q, s = divmod(off + w, SUBLANES)
        src = ucat_ref if s == 0 else shift_ref.at[s - 1]
        conv = conv + src[q * SUBLANES:q * SUBLANES + tm, :] * cw_ref[w:w + 1, :]
    mu = jnp.mean(conv, axis=-1, keepdims=True)
    cen = conv - mu
    var = jnp.mean(cen * cen, axis=-1, keepdims=True)
    y = cen * lax.rsqrt(var + NORM_EPS) * lg_ref[...] + lb_ref[...]
    y = y * jax.nn.sigmoid(y)
    conv_o = jnp.dot(y.astype(BF16), wco_ref[...], preferred_element_type=F32)
    out_ref[...] = (ga_ref[...] * attn_o + gc_ref[...] * conv_o).astype(out_ref.dtype)


def _mix(outs, lses, u, gates, w_attn_o, conv_w, conv_b, ln_g, ln_b, w_conv_o):
    s, cc = u.shape
    d = w_attn_o.shape[1]
    tm = 256
    row = lambda i: (i, 0)
    const = lambda i: (0, 0)
    cvec = pl.BlockSpec((1, cc), const)

    def grp(arr):
        dilation = arr.shape[0]
        return pl.BlockSpec((dilation, tm // dilation, GROUP_WIDTH), lambda i: (0, i, 0))

    return pl.pallas_call(
        _mix_kernel,
        out_shape=jax.ShapeDtypeStruct((s, d), BF16),
        grid=(s // tm,),
        in_specs=[grp(a) for a in (*outs, *lses)] + [
            pl.BlockSpec((tm, cc), row),
            pl.BlockSpec((CONV_HALO, cc), lambda i: (jnp.maximum(i * (tm // CONV_HALO) - 1, 0), 0)),
            pl.BlockSpec((tm, d), lambda i: (i, 0)),
            pl.BlockSpec((tm, d), lambda i: (i, 1)),
            pl.BlockSpec((GROUP_WIDTH, d), const),
            pl.BlockSpec((CONV_WIDTH, cc), const),
            cvec, cvec, cvec,
            pl.BlockSpec((cc, d), const)],
        out_specs=pl.BlockSpec((tm, d), row),
        scratch_shapes=[pltpu.VMEM((tm + CONV_HALO, cc), F32),
                        pltpu.VMEM((2 * N_GROUPS, HEADS_PER_GROUP, tm, HEAD_DIM), F32),
                        pltpu.VMEM((SUBLANES - 1, tm + CONV_HALO - SUBLANES, cc), F32)],
        compiler_params=_params(("parallel",)),
        name="mix",
    )(*outs, *lses, u, u, gates, gates, w_attn_o, conv_w, conv_b, ln_g, ln_b, w_conv_o)


def _out_proj_kernel(m_ref, w_ref, x_ref, gate_ref, g1_ref, g2_ref, scale_ref, shift_ref,
                     x1_ref, h2t_ref):
    y = jnp.dot(m_ref[...], w_ref[...], preferred_element_type=F32)
    x1 = x_ref[...] + gate_ref[...] * (_rms(y) * g1_ref[...])
    x1_ref[...] = x1
    h2 = (_rms(x1) * g2_ref[...]) * (1.0 + scale_ref[...]) + shift_ref[...]
    h2t_ref[...] = h2.T.astype(h2t_ref.dtype)


def _out_proj(merged, w_out, x, gate1, g1, g2, scale2, shift2):
    s, d = x.shape
    tm = 512
    row = lambda i: (i, 0)
    vec = pl.BlockSpec((1, d), lambda i: (0, 0))
    return pl.pallas_call(
        _out_proj_kernel,
        out_shape=(jax.ShapeDtypeStruct((s, d), F32), jax.ShapeDtypeStruct((d, s), BF16)),
        grid=(s // tm,),
        in_specs=[pl.BlockSpec((tm, d), row), pl.BlockSpec((d, d), lambda i: (0, 0)),
                  pl.BlockSpec((tm, d), row), vec, vec, vec, vec, vec],
        out_specs=(pl.BlockSpec((tm, d), row), pl.BlockSpec((d, tm), lambda i: (0, i))),
        compiler_params=_params(("parallel",)),
        name="out_proj",
    )(merged, w_out, x, gate1, g1, g2, scale2, shift2)


def _odd_even_merge_sort(lo, hi):
    def merge(lo, hi, r):
        step = 2 * r
        if step < hi - lo:
            yield from merge(lo, hi, step)
            yield from merge(lo + r, hi, step)
            for i in range(lo + r, hi - r, step):
                yield (i, i + r)
        else:
            yield (lo, lo + r)

    if hi > lo:
        mid = lo + (hi - lo) // 2
        yield from _odd_even_merge_sort(lo, mid)
        yield from _odd_even_merge_sort(mid + 1, hi)
        yield from merge(lo, hi, 1)


def _bitonic_merge(n):
    half = n // 2
    while half >= 1:
        for i in range(n):
            if (i // half) % 2 == 0:
                yield (i, i + half)
        half //= 2


_SORT_PAIRS = {n: tuple(_odd_even_merge_sort(0, n - 1)) for n in (PEER_TOPK // 2, PEER_TOPK)}
_MERGE_PAIRS = tuple(_bitonic_merge(PEER_TOPK))


def _compare_exchange(v, pairs):
    for i, j in pairs:
        v[i], v[j] = jnp.maximum(v[i], v[j]), jnp.minimum(v[i], v[j])


def _top16_rows(x):
    blocks = x.shape[0] // SUBLANES
    assert x.shape[0] == blocks * SUBLANES and blocks <= PEER_TOPK
    n = PEER_TOPK // 2 if blocks <= PEER_TOPK // 2 else PEER_TOPK
    v = [x[k * SUBLANES:(k + 1) * SUBLANES, :] for k in range(blocks)]
    v += [jnp.full_like(v[0], NEG_INF)] * (n - blocks)
    _compare_exchange(v, _SORT_PAIRS[n])
    shift = SUBLANES // 2
    while shift >= 1:
        partner = [pltpu.roll(blk, shift, axis=0) for blk in v]
        if len(v) < PEER_TOPK:
            v = v + partner[::-1]
        else:
            v = [jnp.maximum(v[k], partner[PEER_TOPK - 1 - k]) for k in range(PEER_TOPK)]
        _compare_exchange(v, _MERGE_PAIRS)
        shift //= 2
    return [blk[0:1, :] for blk in v]


def _peer_query_kernel(h2t_ref, wqt_ref, keys_ref, e1_ref, e2_ref, theta_ref,
                       qt_ref, top_ref, cand_ref):
    qt_ref[...] = jnp.dot(wqt_ref[...], h2t_ref[...], preferred_element_type=F32)
    for h in range(PEER_HEADS):
        u = []
        for side in range(2):
            r0 = (2 * h + side) * PEER_HALF
            qc = qt_ref[r0:r0 + PEER_HALF, :].astype(BF16)
            sc = jnp.dot(keys_ref[h, side].astype(BF16), qc, preferred_element_type=F32)
            us = jnp.exp(sc - jnp.max(sc, axis=0, keepdims=True))
            vals = _top16_rows(us)
            for k in range(PEER_TOPK):
                top_ref[side, k:k + 1, :] = vals[k]
            u.append(us)
        cand_ref[...] = jnp.full(cand_ref.shape, -1.0, F32)
        off = 0
        for a, nb in enumerate(_CAND_ROWS):
            cand_ref[off:off + nb, :] = top_ref[0, a:a + 1, :] * top_ref[1, 0:nb, :]
            off += nb
        cand = cand_ref[...]
        selected = cand >= _top16_rows(cand)[-1]
        inv_z = 0.5 / jnp.sum(jnp.where(selected, cand, 0.0), axis=0, keepdims=True)
        off = 0
        for a, nb in enumerate(_CAND_ROWS):
            cand_ref[off:off + nb, :] = (top_ref[0, a:a + 1, :] * inv_z) * top_ref[1, 0:nb, :]
            off += nb
        theta = jnp.min(jnp.where(selected, cand_ref[...], jnp.inf), axis=0, keepdims=True)
        e1 = u[0] * inv_z
        for blk in range(PEER_N_KEYS // 8):
            e1_ref[h, blk] = e1[blk * 8:(blk + 1) * 8, :]
        e2_ref[h] = u[1]
        theta_ref[h:h + 1, :] = theta


def _peer_query(h2t, wq_t, sub_keys):
    d, s = h2t.shape
    tt = 512
    nq = wq_t.shape[0]
    side_blocked = jax.ShapeDtypeStruct((PEER_HEADS, PEER_N_KEYS // 8, 8, s), F32)
    side_flat = jax.ShapeDtypeStruct((PEER_HEADS, PEER_N_KEYS, s), F32)
    blocked_spec = pl.BlockSpec((PEER_HEADS, PEER_N_KEYS // 8, 8, tt), lambda t: (0, 0, 0, t))
    flat_spec = pl.BlockSpec((PEER_HEADS, PEER_N_KEYS, tt), lambda t: (0, 0, t))
    return pl.pallas_call(
        _peer_query_kernel,
        out_shape=(side_blocked, side_flat, jax.ShapeDtypeStruct((PEER_HEADS, s), F32)),
        grid=(s // tt,),
        in_specs=[pl.BlockSpec((d, tt), lambda t: (0, t)),
                  pl.BlockSpec((nq, d), lambda t: (0, 0)),
                  pl.BlockSpec(sub_keys.shape, lambda t: (0, 0, 0, 0))],
        out_specs=(blocked_spec, flat_spec, pl.BlockSpec((PEER_HEADS, tt), lambda t: (0, t))),
        scratch_shapes=[pltpu.VMEM((nq, tt), F32),
                        pltpu.VMEM((2, PEER_TOPK, tt), F32),
                        pltpu.VMEM((_CAND_PAD, tt), F32)],
        compiler_params=_params(("parallel",)),
        name="peer_query",
    )(h2t, wq_t, sub_keys)


PEER_LANE_CHUNK = 128
PEER_HALF_KEYS = 4
PEER_HALF_TILE = PEER_HALF_KEYS * PEER_N_KEYS
PEER_TILE = 2 * PEER_HALF_TILE
PEER_TOKENS = 512


def _gate_act(hid_ref, act_ref, e1_ref, e2_ref, theta_ref, half):
    tt = hid_ref.shape[2]
    for ii in range(PEER_HALF_KEYS):
        rows = slice(ii * PEER_N_KEYS, (ii + 1) * PEER_N_KEYS)
        key = half * PEER_HALF_KEYS + ii
        for c in range(tt // PEER_LANE_CHUNK):
            lanes = slice(c * PEER_LANE_CHUNK, (c + 1) * PEER_LANE_CHUNK)
            w = None
            for h in range(PEER_HEADS):
                p = e1_ref[h, 0, key:key + 1, lanes] * e2_ref[h, :, lanes]
                kept = jnp.where(p >= theta_ref[h:h + 1, lanes], p, 0.0)
                w = kept if w is None else w + kept
            act_ref[half, rows, lanes] = (hid_ref[half, rows, lanes] * w).astype(act_ref.dtype)


def _project(down, h2t):
    x = jnp.dot(down, h2t, preferred_element_type=F32)
    return (x * (1.0 + lax.erf(x * INV_SQRT2))).reshape(2, PEER_HALF_TILE, h2t.shape[1])


def _peer_project_kernel(down_ref, h2t_ref, hid_ref):
    hid_ref[...] = _project(down_ref[...], h2t_ref[...])


def _peer_project_first(h2t, down):
    d = h2t.shape[0]
    return pl.pallas_call(
        _peer_project_kernel,
        out_shape=jax.ShapeDtypeStruct((2, PEER_HALF_TILE, PEER_TOKENS), F32),
        grid=(1,),
        in_specs=[pl.BlockSpec((PEER_TILE, d), lambda i: (0, 0)),
                  pl.BlockSpec((d, PEER_TOKENS), lambda i: (0, 0))],
        out_specs=pl.BlockSpec((2, PEER_HALF_TILE, PEER_TOKENS), lambda i: (0, 0, 0)),
        compiler_params=_params(("arbitrary",)),
        name="peer_project_first",
    )(down, h2t)


def _peer_dense_kernel(h2t_ref, down_ref, up_ref, e1_ref, e2_ref, theta_ref, hid0_ref,
                       x1_ref, gate_ref, g_ref, out_ref, hid_ref, act_ref):
    t = pl.program_id(0)
    e = pl.program_id(1)

    @pl.when((t == 0) & (e == 0))
    def _():
        hid_ref[...] = hid0_ref[...]

    @pl.when(e == 0)
    def _():
        out_ref[...] = jnp.zeros(out_ref.shape, F32)

    for half in range(2):
        _gate_act(hid_ref, act_ref, e1_ref, e2_ref, theta_ref, half)
    act = act_ref[...].reshape(PEER_TILE, act_ref.shape[2])
    out_ref[...] += lax.dot_general(act, up_ref[...], (((0,), (0,)), ((), ())), preferred_element_type=F32)
    hid_ref[...] = _project(down_ref[...], h2t_ref[...])

    @pl.when(e == pl.num_programs(1) - 1)
    def _():
        y = out_ref[...]
        out_ref[...] = x1_ref[...] + gate_ref[...] * (_rms(y) * g_ref[...])


def _peer_dense(h2t, down, up, e1, e2, theta, x1, gate2, g3):
    d, s = h2t.shape
    tt = PEER_TOKENS
    n_e = down.shape[0] // PEER_TILE
    n_t = s // tt
    vec = pl.BlockSpec((1, d), lambda t, e: (0, 0))
    next_e = lambda e: (e + 1) % n_e
    next_t = lambda t, e: jnp.minimum(t + (e + 1) // n_e, n_t - 1)
    return pl.pallas_call(
        _peer_dense_kernel,
        out_shape=jax.ShapeDtypeStruct((s, d), F32),
        grid=(n_t, n_e),
        in_specs=[pl.BlockSpec((d, tt), lambda t, e: (0, next_t(t, e))),
                  pl.BlockSpec((PEER_TILE, d), lambda t, e: (next_e(e), 0)),
                  pl.BlockSpec((PEER_TILE, d), lambda t, e: (e, 0)),
                  pl.BlockSpec((PEER_HEADS, 1, 2 * PEER_HALF_KEYS, tt), lambda t, e: (0, e, 0, t)),
                  pl.BlockSpec((PEER_HEADS, PEER_N_KEYS, tt), lambda t, e: (0, 0, t)),
                  pl.BlockSpec((PEER_HEADS, tt), lambda t, e: (0, t)),
                  pl.BlockSpec((2, PEER_HALF_TILE, tt), lambda t, e: (0, 0, 0)),
                  pl.BlockSpec((tt, d), lambda t, e: (t, 0)),
                  vec, vec],
        out_specs=pl.BlockSpec((tt, d), lambda t, e: (t, 0)),
        scratch_shapes=[pltpu.VMEM((2, PEER_HALF_TILE, tt), F32),
                        pltpu.VMEM((2, PEER_HALF_TILE, tt), BF16)],
        compiler_params=_params(("arbitrary", "arbitrary")),
        name="peer_dense",
    )(h2t, down, up, e1, e2, theta, _peer_project_first(h2t, down), x1, gate2, g3)


def kernel(x, c, positions, ada_w, ada_b, norm_gains, w_in, w_attn_o, conv_w, conv_b, conv_ln_g,
           conv_ln_b, w_conv_o, w_out, peer_w_q, peer_sub_keys, peer_down, peer_up):
    batch, seq, d = x.shape
    depth = ada_w.shape[0]
    assert batch == 1, "kernels are written for a single sequence"
    assert all(window // dilation == ATTN_BLOCK for window, dilation in DILATED_GROUPS)
    cc = conv_w.shape[-1]
    xs = x[0]
    inv_freq = ROPE_THETA ** (-jnp.arange(0, HEAD_DIM, 2, dtype=F32) / HEAD_DIM)
    invf_row = jnp.concatenate([inv_freq, inv_freq])[None, :]
    cos, sin = _rope_tables(positions[0].astype(F32)[:, None], invf_row)
    row = lambda v: v[None, :]
    for l in range(depth):
        mod = _modulation(c[0][:, None], ada_w[l], ada_b[l][None, :])
        shift1, scale1, gate1, shift2, scale2, gate2 = [mod[:, k * d:(k + 1) * d] for k in range(6)]
        g = norm_gains[l]
        w_in_b = w_in[l].astype(BF16)

        h = _norm_mod(xs, row(g[0]), scale1, shift1)
        u = _glu_proj(h, w_in_b, cc)
        gates = _gates_proj(h, w_in_b, 3 * ATTN_WIDTH + 2 * cc, 2 * d)
        outs, lses = [], []
        for gi, (_, dilation) in enumerate(DILATED_GROUPS):
            o, lse = _dilated_attention(_qkv_proj(h, w_in_b, cos, sin, gi, dilation))
            outs.append(o)
            lses.append(lse)
        merged = _mix(outs, lses, u, gates, w_attn_o[l].astype(BF16), conv_w[l][:, 0, :],
                      row(conv_b[l]), row(conv_ln_g[l]), row(conv_ln_b[l]), w_conv_o[l].astype(BF16))
        x1, h2t = _out_proj(merged, w_out[l].astype(BF16), xs, gate1, row(g[1]), row(g[2]),
                            scale2, shift2)

        e1, e2, theta = _peer_query(h2t, peer_w_q[l].T.astype(BF16), peer_sub_keys[l])
        xs = _peer_dense(h2t, peer_down[l].astype(BF16), peer_up[l].astype(BF16),
                         e1, e2, theta, x1, gate2, row(g[3]))
    return xs[None]
```

```python
import functools
import math

import jax
import jax.numpy as jnp
from jax import lax
from jax.experimental import pallas as pl
from jax.experimental.pallas import tpu as pltpu

F32 = jnp.float32
BF16 = jnp.bfloat16

HEAD_DIM = 128
HEADS_PER_GROUP = 4
DILATED_GROUPS = ((128, 1), (512, 4), (2048, 16))
N_GROUPS = len(DILATED_GROUPS)
GROUP_WIDTH = HEADS_PER_GROUP * HEAD_DIM
ATTN_WIDTH = N_GROUPS * GROUP_WIDTH
ATTN_BLOCK = 128
ROPE_THETA = 10000.0
CONV_WIDTH = 31
CONV_HALO = 32
PEER_HEADS = 8
PEER_N_KEYS = 128
PEER_HALF = 128
PEER_TOPK = 16
NORM_EPS = 1e-6
MASK_VALUE = -1e30
NEG_INF = float("-inf")
INV_SQRT2 = 1.0 / math.sqrt(2.0)

SUBLANES = 8
COL = 512
PROJ_ROWS = 2048
VMEM_LIMIT = 56 * 1024 * 1024

_CAND_ROWS = [PEER_TOPK // (a + 1) for a in range(PEER_TOPK)]
_N_CAND = sum(_CAND_ROWS)
_CAND_PAD = -(-_N_CAND // 8) * 8


def _params(sem):
    return pltpu.CompilerParams(dimension_semantics=sem, vmem_limit_bytes=VMEM_LIMIT)


def _rms(x):
    return x * lax.rsqrt(jnp.mean(x * x, axis=-1, keepdims=True) + NORM_EPS)


def _mod_kernel(c_ref, w_ref, b_ref, o_ref):
    rows = 256

    def body(k, acc):
        r = pl.multiple_of(k * rows, rows)
        c = c_ref[pl.ds(r, rows), :]
        sc = c * jax.nn.sigmoid(c)
        return acc + jnp.sum(w_ref[pl.ds(r, rows), :] * sc, axis=0, keepdims=True)

    acc = lax.fori_loop(0, w_ref.shape[0] // rows, body, jnp.zeros(o_ref.shape, F32))
    o_ref[...] = acc + b_ref[...]


def _modulation(c_col, w, b):
    d, n = w.shape
    tn = 512
    return pl.pallas_call(
        _mod_kernel,
        out_shape=jax.ShapeDtypeStruct((1, n), F32),
        grid=(n // tn,),
        in_specs=[pl.BlockSpec((d, 1), lambda j: (0, 0)),
                  pl.BlockSpec((d, tn), lambda j: (0, j)),
                  pl.BlockSpec((1, tn), lambda j: (0, j))],
        out_specs=pl.BlockSpec((1, tn), lambda j: (0, j)),
        compiler_params=_params(("parallel",)),
        name="mod",
    )(c_col, w, b)


def _rope_kernel(pos_ref, invf_ref, cos_ref, sin_ref):
    ang = pos_ref[...] * invf_ref[...]
    cos_ref[...] = jnp.cos(ang)
    s = jnp.sin(ang)
    lane = lax.broadcasted_iota(jnp.int32, s.shape, 1)
    sin_ref[...] = jnp.where(lane < HEAD_DIM // 2, -s, s)


def _rope_tables(pos_col, invf_row):
    s = pos_col.shape[0]
    ts = 2048
    return pl.pallas_call(
        _rope_kernel,
        out_shape=(jax.ShapeDtypeStruct((s, HEAD_DIM), F32),) * 2,
        grid=(s // ts,),
        in_specs=[pl.BlockSpec((ts, 1), lambda i: (i, 0)),
                  pl.BlockSpec((1, HEAD_DIM), lambda i: (0, 0))],
        out_specs=(pl.BlockSpec((ts, HEAD_DIM), lambda i: (i, 0)),) * 2,
        compiler_params=_params(("parallel",)),
        name="rope",
    )(pos_col, invf_row)


def _norm_mod_kernel(x_ref, g_ref, scale_ref, shift_ref, o_ref):
    y = _rms(x_ref[...]) * g_ref[...]
    o_ref[...] = (y * (1.0 + scale_ref[...]) + shift_ref[...]).astype(o_ref.dtype)


def _norm_mod(x, g, scale, shift):
    s, d = x.shape
    tm = 512
    vec = pl.BlockSpec((1, d), lambda i: (0, 0))
    return pl.pallas_call(
        _norm_mod_kernel,
        out_shape=jax.ShapeDtypeStruct((s, d), BF16),
        grid=(s // tm,),
        in_specs=[pl.BlockSpec((tm, d), lambda i: (i, 0)), vec, vec, vec],
        out_specs=pl.BlockSpec((tm, d), lambda i: (i, 0)),
        compiler_params=_params(("parallel",)),
        name="norm_mod",
    )(x, g, scale, shift)


def _qkv_kernel(h_ref, w_ref, cos_ref, sin_ref, o_ref, stage_ref):
    acc = jnp.dot(h_ref[...], w_ref[...], preferred_element_type=F32)
    part = pl.program_id(1)

    heads = [slice(hh * HEAD_DIM, (hh + 1) * HEAD_DIM) for hh in range(HEADS_PER_GROUP)]

    @pl.when(part < 2)
    def _():
        cos = cos_ref[...]
        sin = sin_ref[...]
        for hh, cols in enumerate(heads):
            t = acc[:, cols]
            stage_ref[hh] = t * cos + pltpu.roll(t, HEAD_DIM // 2, axis=1) * sin

    @pl.when(part == 2)
    def _():
        for hh, cols in enumerate(heads):
            stage_ref[hh] = acc[:, cols]

    dilation, rows, _ = o_ref.shape
    for r in range(dilation):
        for hh, cols in enumerate(heads):
            o_ref[r, :, cols] = stage_ref[hh, pl.ds(r, rows, stride=dilation), :].astype(o_ref.dtype)


def _qkv_proj(h, w_in, cos, sin, group, dilation):
    s, d = h.shape
    tm = PROJ_ROWS
    return pl.pallas_call(
        _qkv_kernel,
        out_shape=jax.ShapeDtypeStruct((dilation, s // dilation, 3 * GROUP_WIDTH), BF16),
        grid=(s // tm, 3),
        in_specs=[pl.BlockSpec((tm, d), lambda i, j: (i, 0)),
                  pl.BlockSpec((d, COL), lambda i, j: (0, j * N_GROUPS + group)),
                  pl.BlockSpec((tm, HEAD_DIM), lambda i, j: (i, 0)),
                  pl.BlockSpec((tm, HEAD_DIM), lambda i, j: (i, 0))],
        out_specs=pl.BlockSpec((dilation, tm // dilation, COL), lambda i, j: (0, i, j)),
        scratch_shapes=[pltpu.VMEM((HEADS_PER_GROUP, tm, HEAD_DIM), F32)],
        compiler_params=_params(("parallel", "arbitrary")),
        name=f"qkv_d{dilation}",
    )(h, w_in, cos, sin)


def _glu_kernel(h_ref, wa_ref, wb_ref, o_ref):
    h = h_ref[...]
    a = jnp.dot(h, wa_ref[...], preferred_element_type=F32)
    b = jnp.dot(h, wb_ref[...], preferred_element_type=F32)
    o_ref[...] = a * jax.nn.sigmoid(b)


def _glu_proj(h, w_in, conv_channels):
    s, d = h.shape
    tm = PROJ_ROWS
    a0 = 3 * ATTN_WIDTH // COL
    nb = conv_channels // COL
    return pl.pallas_call(
        _glu_kernel,
        out_shape=jax.ShapeDtypeStruct((s, conv_channels), F32),
        grid=(s // tm, nb),
        in_specs=[pl.BlockSpec((tm, d), lambda i, j: (i, 0)),
                  pl.BlockSpec((d, COL), lambda i, j: (0, a0 + j)),
                  pl.BlockSpec((d, COL), lambda i, j: (0, a0 + nb + j))],
        out_specs=pl.BlockSpec((tm, COL), lambda i, j: (i, j)),
        compiler_params=_params(("parallel", "arbitrary")),
        name="glu",
    )(h, w_in, w_in)


def _gates_kernel(h_ref, w_ref, side_ref, o_ref, side_out_ref):
    o_ref[...] = jax.nn.sigmoid(jnp.dot(h_ref[...], w_ref[...], preferred_element_type=F32))
    side_out_ref[...] = side_ref[...].astype(side_out_ref.dtype)


def _gates_proj(h, w_in, col0, width, side):
    s, d = h.shape
    tm = PROJ_ROWS
    c0 = col0 // COL
    n_j = width // COL
    slab = side.shape[0] // ((s // tm) * n_j)
    assert slab * (s // tm) * n_j == side.shape[0]
    return pl.pallas_call(
        _gates_kernel,
        out_shape=(jax.ShapeDtypeStruct((s, width), F32), jax.ShapeDtypeStruct(side.shape, BF16)),
        grid=(s // tm, n_j),
        in_specs=[pl.BlockSpec((tm, d), lambda i, j: (i, 0)),
                  pl.BlockSpec((d, COL), lambda i, j: (0, c0 + j)),
                  pl.BlockSpec((slab, side.shape[1]), lambda i, j: (i * n_j + j, 0))],
        out_specs=(pl.BlockSpec((tm, COL), lambda i, j: (i, j)),
                   pl.BlockSpec((slab, side.shape[1]), lambda i, j: (i * n_j + j, 0))),
        compiler_params=_params(("parallel", "arbitrary")),
        name="gates",
    )(h, w_in, side)


ATTN_Q_BLOCKS = 4


def _attn_kernel(q_ref, kc_ref, kp_ref, vc_ref, vp_ref, o_ref, l_ref):
    n = pl.program_id(1)
    qi = lax.broadcasted_iota(jnp.int32, (ATTN_BLOCK, 2 * ATTN_BLOCK), 0)
    kj = lax.broadcasted_iota(jnp.int32, (ATTN_BLOCK, 2 * ATTN_BLOCK), 1)
    band = jnp.abs(kj - qi - ATTN_BLOCK // 2) <= ATTN_BLOCK // 2
    first_lo = jnp.where(n == 0, ATTN_BLOCK, 0)
    band_first = band & (kj >= first_lo)
    scale = HEAD_DIM ** -0.5
    for hh in range(HEADS_PER_GROUP):
        cols = slice(hh * HEAD_DIM, (hh + 1) * HEAD_DIM)
        for b in range(ATTN_Q_BLOCKS):
            rows = slice(b * ATTN_BLOCK, (b + 1) * ATTN_BLOCK)
            q = q_ref[rows, cols]
            if b == 0:
                k = jnp.concatenate([kp_ref[:, cols], kc_ref[0:ATTN_BLOCK, cols]], axis=0)
                v = jnp.concatenate([vp_ref[:, cols], vc_ref[0:ATTN_BLOCK, cols]], axis=0)
                valid = band_first
            else:
                band_rows = slice((b - 1) * ATTN_BLOCK, (b + 1) * ATTN_BLOCK)
                k = kc_ref[band_rows, cols]
                v = vc_ref[band_rows, cols]
                valid = band
            s = lax.dot_general(q, k, (((1,), (1,)), ((), ())), preferred_element_type=F32) * scale
            s = jnp.where(valid, s, MASK_VALUE)
            m = jnp.max(s, axis=-1, keepdims=True)
            p = jnp.exp(s - m)
            den = jnp.sum(p, axis=-1, keepdims=True)
            o = jnp.dot(p.astype(BF16), v, preferred_element_type=F32)
            o_ref[rows, cols] = o * (1.0 / den)
            l_ref[rows, cols] = jnp.broadcast_to(m + jnp.log(den), (ATTN_BLOCK, HEAD_DIM))


def _dilated_attention(qkv):
    dilation, sub_len, _ = qkv.shape
    qt = ATTN_Q_BLOCKS * ATTN_BLOCK

    def cur(part):
        return pl.BlockSpec((None, qt, COL), lambda r, n: (r, n, part))

    def prev(part):
        return pl.BlockSpec((None, ATTN_BLOCK, COL),
                            lambda r, n: (r, jnp.maximum(n * ATTN_Q_BLOCKS - 1, 0), part))

    out_sds = jax.ShapeDtypeStruct((dilation, sub_len, GROUP_WIDTH), F32)
    out_spec = pl.BlockSpec((None, qt, GROUP_WIDTH), lambda r, n: (r, n, 0))
    return pl.pallas_call(
        _attn_kernel,
        out_shape=(out_sds, out_sds),
        grid=(dilation, sub_len // qt),
        in_specs=[cur(0), cur(1), prev(1), cur(2), prev(2)],
        out_specs=(out_spec, out_spec),
        compiler_params=_params(("parallel", "arbitrary")),
        name=f"attn_d{dilation}",
    )(qkv, qkv, qkv, qkv, qkv)


def _mix_kernel(o0_ref, o1_ref, o2_ref, l0_ref, l1_ref, l2_ref, u_ref, uh_ref, ga_ref, gc_ref,
                wao_ref, cw_ref, cb_ref, lg_ref, lb_ref, wco_ref, side_ref,
                out_ref, side_out_ref, ucat_ref, nat_ref, shift_ref):
    i = pl.program_id(0)
    tm = u_ref.shape[0]
    side_out_ref[...] = side_ref[...].astype(side_out_ref.dtype)

    def natural(ref, slot):
        dilation, rows, _ = ref.shape
        if dilation == 1:
            return ref[0]
        for hh in range(HEADS_PER_GROUP):
            for r in range(dilation):
                nat_ref[slot, hh, pl.ds(r, rows, stride=dilation), :] = ref[r, :, hh * HEAD_DIM:(hh + 1) * HEAD_DIM]
        return jnp.concatenate([nat_ref[slot, hh] for hh in range(HEADS_PER_GROUP)], axis=1)

    l0, l1, l2 = natural(l0_ref, 0), natural(l1_ref, 1), natural(l2_ref, 2)
    m = jnp.maximum(jnp.maximum(l0, l1), l2)
    e0, e1, e2 = jnp.exp(l0 - m), jnp.exp(l1 - m), jnp.exp(l2 - m)
    attn = (e0 * natural(o0_ref, 3) + e1 * natural(o1_ref, 4) + e2 * natural(o2_ref, 5)) / (e0 + e1 + e2)
    attn_o = jnp.dot(attn.astype(BF16), wao_ref[...], preferred_element_type=F32)
    ucat_ref[0:CONV_HALO, :] = jnp.where(i > 0, uh_ref[...], 0.0)
    ucat_ref[CONV_HALO:, :] = u_ref[...]
    off = CONV_HALO - (CONV_WIDTH - 1)
    span = tm + CONV_HALO - SUBLANES
    for s in range(1, SUBLANES):
        shift_ref[s - 1] = ucat_ref[s:s + span, :]
    conv = jnp.zeros(u_ref.shape, F32) + cb_ref[...]
    for w in range(CONV_WIDTH):
        q, s = divmod(off + w, SUBLANES)
        src = ucat_ref if s == 0 else shift_ref.at[s - 1]
        conv = conv + src[q * SUBLANES:q * SUBLANES + tm, :] * cw_ref[w:w + 1, :]
    mu = jnp.mean(conv, axis=-1, keepdims=True)
    cen = conv - mu
    var = jnp.mean(cen * cen, axis=-1, keepdims=True)
    y = cen * lax.rsqrt(var + NORM_EPS) * lg_ref[...] + lb_ref[...]
    y = y * jax.nn.sigmoid(y)
    conv_o = jnp.dot(y.astype(BF16), wco_ref[...], preferred_element_type=F32)
    out_ref[...] = (ga_ref[...] * attn_o + gc_ref[...] * conv_o).astype(out_ref.dtype)


def _mix(outs, lses, u, gates, w_attn_o, conv_w, conv_b, ln_g, ln_b, w_conv_o, side):
    s, cc = u.shape
    d = w_attn_o.shape[1]
    tm = 256
    row = lambda i: (i, 0)
    const = lambda i: (0, 0)
    cvec = pl.BlockSpec((1, cc), const)
    slab = side.shape[0] // (s // tm)
    assert slab * (s // tm) == side.shape[0]
    side_spec = pl.BlockSpec((slab, side.shape[1]), row)

    def grp(arr):
        dilation = arr.shape[0]
        return pl.BlockSpec((dilation, tm // dilation, GROUP_WIDTH), lambda i: (0, i, 0))

    return pl.pallas_call(
        _mix_kernel,
        out_shape=(jax.ShapeDtypeStruct((s, d), BF16), jax.ShapeDtypeStruct(side.shape, BF16)),
        grid=(s // tm,),
        in_specs=[grp(a) for a in (*outs, *lses)] + [
            pl.BlockSpec((tm, cc), row),
            pl.BlockSpec((CONV_HALO, cc), lambda i: (jnp.maximum(i * (tm // CONV_HALO) - 1, 0), 0)),
            pl.BlockSpec((tm, d), lambda i: (i, 0)),
            pl.BlockSpec((tm, d), lambda i: (i, 1)),
            pl.BlockSpec((GROUP_WIDTH, d), const),
            pl.BlockSpec((CONV_WIDTH, cc), const),
            cvec, cvec, cvec,
            pl.BlockSpec((cc, d), const),
            side_spec],
        out_specs=(pl.BlockSpec((tm, d), row), side_spec),
        scratch_shapes=[pltpu.VMEM((tm + CONV_HALO, cc), F32),
                        pltpu.VMEM((2 * N_GROUPS, HEADS_PER_GROUP, tm, HEAD_DIM), F32),
                        pltpu.VMEM((SUBLANES - 1, tm + CONV_HALO - SUBLANES, cc), F32)],
        compiler_params=_params(("parallel",)),
        name="mix",
    )(*outs, *lses, u, u, gates, gates, w_attn_o, conv_w, conv_b, ln_g, ln_b, w_conv_o, side)


def _out_proj_kernel(m_ref, w_ref, x_ref, gate_ref, g1_ref, g2_ref, scale_ref, shift_ref,
                     x1_ref, h2t_ref):
    y = jnp.dot(m_ref[...], w_ref[...], preferred_element_type=F32)
    x1 = x_ref[...] + gate_ref[...] * (_rms(y) * g1_ref[...])
    x1_ref[...] = x1
    h2 = (_rms(x1) * g2_ref[...]) * (1.0 + scale_ref[...]) + shift_ref[...]
    h2t_ref[...] = h2.T.astype(h2t_ref.dtype)


def _out_proj(merged, w_out, x, gate1, g1, g2, scale2, shift2):
    s, d = x.shape
    tm = 512
    row = lambda i: (i, 0)
    vec = pl.BlockSpec((1, d), lambda i: (0, 0))
    return pl.pallas_call(
        _out_proj_kernel,
        out_shape=(jax.ShapeDtypeStruct((s, d), F32), jax.ShapeDtypeStruct((d, s), BF16)),
        grid=(s // tm,),
        in_specs=[pl.BlockSpec((tm, d), row), pl.BlockSpec((d, d), lambda i: (0, 0)),
                  pl.BlockSpec((tm, d), row), vec, vec, vec, vec, vec],
        out_specs=(pl.BlockSpec((tm, d), row), pl.BlockSpec((d, tm), lambda i: (0, i))),
        compiler_params=_params(("parallel",)),
        name="out_proj",
    )(merged, w_out, x, gate1, g1, g2, scale2, shift2)


def _odd_even_merge_sort(lo, hi):
    def merge(lo, hi, r):
        step = 2 * r
        if step < hi - lo:
            yield from merge(lo, hi, step)
            yield from merge(lo + r, hi, step)
            for i in range(lo + r, hi - r, step):
                yield (i, i + r)
        else:
            yield (lo, lo + r)

    if hi > lo:
        mid = lo + (hi - lo) // 2
        yield from _odd_even_merge_sort(lo, mid)
        yield from _odd_even_merge_sort(mid + 1, hi)
        yield from merge(lo, hi, 1)


def _bitonic_merge(n):
    half = n // 2
    while half >= 1:
        for i in range(n):
            if (i // half) % 2 == 0:
                yield (i, i + half)
        half //= 2


_SORT_PAIRS = {n: tuple(_odd_even_merge_sort(0, n - 1)) for n in (PEER_TOPK // 2, PEER_TOPK)}
_MERGE_PAIRS = tuple(_bitonic_merge(PEER_TOPK))


def _compare_exchange(v, pairs):
    for i, j in pairs:
        v[i], v[j] = jnp.maximum(v[i], v[j]), jnp.minimum(v[i], v[j])


def _top16_rows(x):
    blocks = x.shape[0] // SUBLANES
    assert x.shape[0] == blocks * SUBLANES and blocks <= PEER_TOPK
    n = PEER_TOPK // 2 if blocks <= PEER_TOPK // 2 else PEER_TOPK
    v = [x[k * SUBLANES:(k + 1) * SUBLANES, :] for k in range(blocks)]
    v += [jnp.full_like(v[0], NEG_INF)] * (n - blocks)
    _compare_exchange(v, _SORT_PAIRS[n])
    shift = SUBLANES // 2
    while shift >= 1:
        partner = [pltpu.roll(blk, shift, axis=0) for blk in v]
        if len(v) < PEER_TOPK:
            v = v + partner[::-1]
        else:
            v = [jnp.maximum(v[k], partner[PEER_TOPK - 1 - k]) for k in range(PEER_TOPK)]
        _compare_exchange(v, _MERGE_PAIRS)
        shift //= 2
    return [blk[0:1, :] for blk in v]


def _peer_query_kernel(h2t_ref, wqt_ref, keys_ref, e1_ref, e2_ref, theta_ref,
                       qt_ref, top_ref, cand_ref):
    qt_ref[...] = jnp.dot(wqt_ref[...], h2t_ref[...], preferred_element_type=F32)
    for h in range(PEER_HEADS):
        u = []
        for side in range(2):
            r0 = (2 * h + side) * PEER_HALF
            qc = qt_ref[r0:r0 + PEER_HALF, :].astype(BF16)
            sc = jnp.dot(keys_ref[h, side].astype(BF16), qc, preferred_element_type=F32)
            us = jnp.exp(sc - jnp.max(sc, axis=0, keepdims=True))
            vals = _top16_rows(us)
            for k in range(PEER_TOPK):
                top_ref[side, k:k + 1, :] = vals[k]
            u.append(us)
        cand_ref[...] = jnp.full(cand_ref.shape, -1.0, F32)
        off = 0
        for a, nb in enumerate(_CAND_ROWS):
            cand_ref[off:off + nb, :] = top_ref[0, a:a + 1, :] * top_ref[1, 0:nb, :]
            off += nb
        cand = cand_ref[...]
        selected = cand >= _top16_rows(cand)[-1]
        inv_z = 0.5 / jnp.sum(jnp.where(selected, cand, 0.0), axis=0, keepdims=True)
        off = 0
        for a, nb in enumerate(_CAND_ROWS):
            cand_ref[off:off + nb, :] = (top_ref[0, a:a + 1, :] * inv_z) * top_ref[1, 0:nb, :]
            off += nb
        theta = jnp.min(jnp.where(selected, cand_ref[...], jnp.inf), axis=0, keepdims=True)
        e1 = u[0] * inv_z
        for blk in range(PEER_N_KEYS // 8):
            e1_ref[h, blk] = e1[blk * 8:(blk + 1) * 8, :]
        e2_ref[h] = u[1]
        theta_ref[h:h + 1, :] = theta


def _peer_query(h2t, wq_t, sub_keys):
    d, s = h2t.shape
    tt = 512
    nq = wq_t.shape[0]
    side_blocked = jax.ShapeDtypeStruct((PEER_HEADS, PEER_N_KEYS // 8, 8, s), F32)
    side_flat = jax.ShapeDtypeStruct((PEER_HEADS, PEER_N_KEYS, s), F32)
    blocked_spec = pl.BlockSpec((PEER_HEADS, PEER_N_KEYS // 8, 8, tt), lambda t: (0, 0, 0, t))
    flat_spec = pl.BlockSpec((PEER_HEADS, PEER_N_KEYS, tt), lambda t: (0, 0, t))
    return pl.pallas_call(
        _peer_query_kernel,
        out_shape=(side_blocked, side_flat, jax.ShapeDtypeStruct((PEER_HEADS, s), F32)),
        grid=(s // tt,),
        in_specs=[pl.BlockSpec((d, tt), lambda t: (0, t)),
                  pl.BlockSpec((nq, d), lambda t: (0, 0)),
                  pl.BlockSpec(sub_keys.shape, lambda t: (0, 0, 0, 0))],
        out_specs=(blocked_spec, flat_spec, pl.BlockSpec((PEER_HEADS, tt), lambda t: (0, t))),
        scratch_shapes=[pltpu.VMEM((nq, tt), F32),
                        pltpu.VMEM((2, PEER_TOPK, tt), F32),
                        pltpu.VMEM((_CAND_PAD, tt), F32)],
        compiler_params=_params(("parallel",)),
        name="peer_query",
    )(h2t, wq_t, sub_keys)


PEER_LANE_CHUNK = 128
PEER_HALF_KEYS = 4
PEER_HALF_TILE = PEER_HALF_KEYS * PEER_N_KEYS
PEER_TILE = 2 * PEER_HALF_TILE
PEER_TOKENS = 512


def _gate_act(hid_ref, act_ref, e1_ref, e2_ref, theta_ref, half):
    tt = hid_ref.shape[2]
    for ii in range(PEER_HALF_KEYS):
        rows = slice(ii * PEER_N_KEYS, (ii + 1) * PEER_N_KEYS)
        key = half * PEER_HALF_KEYS + ii
        for c in range(tt // PEER_LANE_CHUNK):
            lanes = slice(c * PEER_LANE_CHUNK, (c + 1) * PEER_LANE_CHUNK)
            w = None
            for h in range(PEER_HEADS):
                p = e1_ref[h, 0, key:key + 1, lanes] * e2_ref[h, :, lanes]
                kept = jnp.where(p >= theta_ref[h:h + 1, lanes], p, 0.0)
                w = kept if w is None else w + kept
            act_ref[half, rows, lanes] = (hid_ref[half, rows, lanes] * w).astype(act_ref.dtype)


def _project(down, h2t):
    x = jnp.dot(down, h2t, preferred_element_type=F32)
    return (x * (1.0 + lax.erf(x * INV_SQRT2))).reshape(2, PEER_HALF_TILE, h2t.shape[1])


def _peer_project_kernel(down_ref, h2t_ref, hid_ref):
    hid_ref[...] = _project(down_ref[...], h2t_ref[...])


def _peer_project_first(h2t, down):
    d = h2t.shape[0]
    return pl.pallas_call(
        _peer_project_kernel,
        out_shape=jax.ShapeDtypeStruct((2, PEER_HALF_TILE, PEER_TOKENS), F32),
        grid=(1,),
        in_specs=[pl.BlockSpec((PEER_TILE, d), lambda i: (0, 0)),
                  pl.BlockSpec((d, PEER_TOKENS), lambda i: (0, 0))],
        out_specs=pl.BlockSpec((2, PEER_HALF_TILE, PEER_TOKENS), lambda i: (0, 0, 0)),
        compiler_params=_params(("arbitrary",)),
        name="peer_project_first",
    )(down, h2t)


def _peer_dense_kernel(h2t_ref, down_ref, up_ref, e1_ref, e2_ref, theta_ref, hid0_ref,
                       x1_ref, gate_ref, g_ref, out_ref, hid_ref, act_ref):
    t = pl.program_id(0)
    e = pl.program_id(1)

    @pl.when((t == 0) & (e == 0))
    def _():
        hid_ref[...] = hid0_ref[...]

    @pl.when(e == 0)
    def _():
        out_ref[...] = jnp.zeros(out_ref.shape, F32)

    for half in range(2):
        _gate_act(hid_ref, act_ref, e1_ref, e2_ref, theta_ref, half)
    act = act_ref[...].reshape(PEER_TILE, act_ref.shape[2])
    out_ref[...] += lax.dot_general(act, up_ref[...], (((0,), (0,)), ((), ())), preferred_element_type=F32)
    hid_ref[...] = _project(down_ref[...], h2t_ref[...])

    @pl.when(e == pl.num_programs(1) - 1)
    def _():
        y = out_ref[...]
        out_ref[...] = x1_ref[...] + gate_ref[...] * (_rms(y) * g_ref[...])


def _peer_dense(h2t, down, up, e1, e2, theta, x1, gate2, g3):
    d, s = h2t.shape
    tt = PEER_TOKENS
    n_e = down.shape[0] // PEER_TILE
    n_t = s // tt
    vec = pl.BlockSpec((1, d), lambda t, e: (0, 0))
    next_e = lambda e: (e + 1) % n_e
    next_t = lambda t, e: jnp.minimum(t + (e + 1) // n_e, n_t - 1)
    return pl.pallas_call(
        _peer_dense_kernel,
        out_shape=jax.ShapeDtypeStruct((s, d), F32),
        grid=(n_t, n_e),
        in_specs=[pl.BlockSpec((d, tt), lambda t, e: (0, next_t(t, e))),
                  pl.BlockSpec((PEER_TILE, d), lambda t, e: (next_e(e), 0)),
                  pl.BlockSpec((PEER_TILE, d), lambda t, e: (e, 0)),
                  pl.BlockSpec((PEER_HEADS, 1, 2 * PEER_HALF_KEYS, tt), lambda t, e: (0, e, 0, t)),
                  pl.BlockSpec((PEER_HEADS, PEER_N_KEYS, tt), lambda t, e: (0, 0, t)),
                  pl.BlockSpec((PEER_HEADS, tt), lambda t, e: (0, t)),
                  pl.BlockSpec((2, PEER_HALF_TILE, tt), lambda t, e: (0, 0, 0)),
                  pl.BlockSpec((tt, d), lambda t, e: (t, 0)),
                  vec, vec],
        out_specs=pl.BlockSpec((tt, d), lambda t, e: (t, 0)),
        scratch_shapes=[pltpu.VMEM((2, PEER_HALF_TILE, tt), F32),
                        pltpu.VMEM((2, PEER_HALF_TILE, tt), BF16)],
        compiler_params=_params(("arbitrary", "arbitrary")),
        name="peer_dense",
    )(h2t, down, up, e1, e2, theta, _peer_project_first(h2t, down), x1, gate2, g3)


def kernel(x, c, positions, ada_w, ada_b, norm_gains, w_in, w_attn_o, conv_w, conv_b, conv_ln_g,
           conv_ln_b, w_conv_o, w_out, peer_w_q, peer_sub_keys, peer_down, peer_up):
    batch, seq, d = x.shape
    depth = ada_w.shape[0]
    assert batch == 1, "kernels are written for a single sequence"
    assert all(window // dilation == ATTN_BLOCK for window, dilation in DILATED_GROUPS)
    cc = conv_w.shape[-1]
    xs = x[0]
    inv_freq = ROPE_THETA ** (-jnp.arange(0, HEAD_DIM, 2, dtype=F32) / HEAD_DIM)
    invf_row = jnp.concatenate([inv_freq, inv_freq])[None, :]
    cos, sin = _rope_tables(positions[0].astype(F32)[:, None], invf_row)
    row = lambda v: v[None, :]
    for l in range(depth):
        mod = _modulation(c[0][:, None], ada_w[l], ada_b[l][None, :])
        shift1, scale1, gate1, shift2, scale2, gate2 = [mod[:, k * d:(k + 1) * d] for k in range(6)]
        g = norm_gains[l]
        w_in_b = w_in[l].astype(BF16)

        h = _norm_mod(xs, row(g[0]), scale1, shift1)
        u = _glu_proj(h, w_in_b, cc)
        gates, down_b = _gates_proj(h, w_in_b, 3 * ATTN_WIDTH + 2 * cc, 2 * d, peer_down[l])
        outs, lses = [], []
        for gi, (_, dilation) in enumerate(DILATED_GROUPS):
            o, lse = _dilated_attention(_qkv_proj(h, w_in_b, cos, sin, gi, dilation))
            outs.append(o)
            lses.append(lse)
        merged, up_b = _mix(outs, lses, u, gates, w_attn_o[l].astype(BF16), conv_w[l][:, 0, :],
                            row(conv_b[l]), row(conv_ln_g[l]), row(conv_ln_b[l]), w_conv_o[l].astype(BF16),
                            peer_up[l])
        x1, h2t = _out_proj(merged, w_out[l].astype(BF16), xs, gate1, row(g[1]), row(g[2]),
                            scale2, shift2)

        e1, e2, theta = _peer_query(h2t, peer_w_q[l].T.astype(BF16), peer_sub_keys[l])
        xs = _peer_dense(h2t, down_b, up_b, e1, e2, theta, x1, gate2, row(g[3]))
    return xs[None]
```

```python
import functools
import math

import jax
import jax.numpy as jnp
from jax import lax
from jax.experimental import pallas as pl
from jax.experimental.pallas import tpu as pltpu

F32 = jnp.float32
BF16 = jnp.bfloat16

HEAD_DIM = 128
HEADS_PER_GROUP = 4
DILATED_GROUPS = ((128, 1), (512, 4), (2048, 16))
N_GROUPS = len(DILATED_GROUPS)
GROUP_WIDTH = HEADS_PER_GROUP * HEAD_DIM
ATTN_WIDTH = N_GROUPS * GROUP_WIDTH
ATTN_BLOCK = 128
ROPE_THETA = 10000.0
CONV_WIDTH = 31
CONV_HALO = 32
PEER_HEADS = 8
PEER_N_KEYS = 128
PEER_HALF = 128
PEER_TOPK = 16
NORM_EPS = 1e-6
MASK_VALUE = -1e30
NEG_INF = float("-inf")
INV_SQRT2 = 1.0 / math.sqrt(2.0)

SUBLANES = 8
COL = 512
PROJ_ROWS = 2048
VMEM_LIMIT = 56 * 1024 * 1024

_CAND_ROWS = [PEER_TOPK // (a + 1) for a in range(PEER_TOPK)]
_N_CAND = sum(_CAND_ROWS)
_CAND_PAD = -(-_N_CAND // 8) * 8


def _params(sem):
    return pltpu.CompilerParams(dimension_semantics=sem, vmem_limit_bytes=VMEM_LIMIT)


def _rms(x):
    return x * lax.rsqrt(jnp.mean(x * x, axis=-1, keepdims=True) + NORM_EPS)


def _mod_kernel(c_ref, w_ref, b_ref, o_ref):
    rows = 256

    def body(k, acc):
        r = pl.multiple_of(k * rows, rows)
        c = c_ref[pl.ds(r, rows), :]
        sc = c * jax.nn.sigmoid(c)
        return acc + jnp.sum(w_ref[pl.ds(r, rows), :] * sc, axis=0, keepdims=True)

    acc = lax.fori_loop(0, w_ref.shape[0] // rows, body, jnp.zeros(o_ref.shape, F32))
    o_ref[...] = acc + b_ref[...]


def _modulation(c_col, w, b):
    d, n = w.shape
    tn = 512
    return pl.pallas_call(
        _mod_kernel,
        out_shape=jax.ShapeDtypeStruct((1, n), F32),
        grid=(n // tn,),
        in_specs=[pl.BlockSpec((d, 1), lambda j: (0, 0)),
                  pl.BlockSpec((d, tn), lambda j: (0, j)),
                  pl.BlockSpec((1, tn), lambda j: (0, j))],
        out_specs=pl.BlockSpec((1, tn), lambda j: (0, j)),
        compiler_params=_params(("parallel",)),
        name="mod",
    )(c_col, w, b)


def _rope_kernel(pos_ref, invf_ref, cos_ref, sin_ref):
    ang = pos_ref[...] * invf_ref[...]
    cos_ref[...] = jnp.cos(ang)
    s = jnp.sin(ang)
    lane = lax.broadcasted_iota(jnp.int32, s.shape, 1)
    sin_ref[...] = jnp.where(lane < HEAD_DIM // 2, -s, s)


def _rope_tables(pos_col, invf_row):
    s = pos_col.shape[0]
    ts = 2048
    return pl.pallas_call(
        _rope_kernel,
        out_shape=(jax.ShapeDtypeStruct((s, HEAD_DIM), F32),) * 2,
        grid=(s // ts,),
        in_specs=[pl.BlockSpec((ts, 1), lambda i: (i, 0)),
                  pl.BlockSpec((1, HEAD_DIM), lambda i: (0, 0))],
        out_specs=(pl.BlockSpec((ts, HEAD_DIM), lambda i: (i, 0)),) * 2,
        compiler_params=_params(("parallel",)),
        name="rope",
    )(pos_col, invf_row)


def _norm_mod_kernel(x_ref, g_ref, scale_ref, shift_ref, o_ref):
    y = _rms(x_ref[...]) * g_ref[...]
    o_ref[...] = (y * (1.0 + scale_ref[...]) + shift_ref[...]).astype(o_ref.dtype)


def _norm_mod(x, g, scale, shift):
    s, d = x.shape
    tm = 512
    vec = pl.BlockSpec((1, d), lambda i: (0, 0))
    return pl.pallas_call(
        _norm_mod_kernel,
        out_shape=jax.ShapeDtypeStruct((s, d), BF16),
        grid=(s // tm,),
        in_specs=[pl.BlockSpec((tm, d), lambda i: (i, 0)), vec, vec, vec],
        out_specs=pl.BlockSpec((tm, d), lambda i: (i, 0)),
        compiler_params=_params(("parallel",)),
        name="norm_mod",
    )(x, g, scale, shift)


def _qkv_kernel(h_ref, w_ref, cos_ref, sin_ref, o_ref, stage_ref):
    acc = jnp.dot(h_ref[...], w_ref[...].astype(BF16), preferred_element_type=F32)
    part = pl.program_id(1)

    heads = [slice(hh * HEAD_DIM, (hh + 1) * HEAD_DIM) for hh in range(HEADS_PER_GROUP)]

    @pl.when(part < 2)
    def _():
        cos = cos_ref[...]
        sin = sin_ref[...]
        for hh, cols in enumerate(heads):
            t = acc[:, cols]
            stage_ref[hh] = t * cos + pltpu.roll(t, HEAD_DIM // 2, axis=1) * sin

    @pl.when(part == 2)
    def _():
        for hh, cols in enumerate(heads):
            stage_ref[hh] = acc[:, cols]

    dilation, rows, _ = o_ref.shape
    for r in range(dilation):
        for hh, cols in enumerate(heads):
            o_ref[r, :, cols] = stage_ref[hh, pl.ds(r, rows, stride=dilation), :].astype(o_ref.dtype)


def _qkv_proj(h, w_in, cos, sin, group, dilation):
    s, d = h.shape
    tm = PROJ_ROWS
    return pl.pallas_call(
        _qkv_kernel,
        out_shape=jax.ShapeDtypeStruct((dilation, s // dilation, 3 * GROUP_WIDTH), BF16),
        grid=(s // tm, 3),
        in_specs=[pl.BlockSpec((tm, d), lambda i, j: (i, 0)),
                  pl.BlockSpec((d, COL), lambda i, j: (0, j * N_GROUPS + group)),
                  pl.BlockSpec((tm, HEAD_DIM), lambda i, j: (i, 0)),
                  pl.BlockSpec((tm, HEAD_DIM), lambda i, j: (i, 0))],
        out_specs=pl.BlockSpec((dilation, tm // dilation, COL), lambda i, j: (0, i, j)),
        scratch_shapes=[pltpu.VMEM((HEADS_PER_GROUP, tm, HEAD_DIM), F32)],
        compiler_params=_params(("parallel", "arbitrary")),
        name=f"qkv_d{dilation}",
    )(h, w_in, cos, sin)


def _glu_kernel(h_ref, wa_ref, wb_ref, o_ref):
    h = h_ref[...]
    a = jnp.dot(h, wa_ref[...].astype(BF16), preferred_element_type=F32)
    b = jnp.dot(h, wb_ref[...].astype(BF16), preferred_element_type=F32)
    o_ref[...] = a * jax.nn.sigmoid(b)


def _glu_proj(h, w_in, conv_channels):
    s, d = h.shape
    tm = PROJ_ROWS
    a0 = 3 * ATTN_WIDTH // COL
    nb = conv_channels // COL
    return pl.pallas_call(
        _glu_kernel,
        out_shape=jax.ShapeDtypeStruct((s, conv_channels), F32),
        grid=(s // tm, nb),
        in_specs=[pl.BlockSpec((tm, d), lambda i, j: (i, 0)),
                  pl.BlockSpec((d, COL), lambda i, j: (0, a0 + j)),
                  pl.BlockSpec((d, COL), lambda i, j: (0, a0 + nb + j))],
        out_specs=pl.BlockSpec((tm, COL), lambda i, j: (i, j)),
        compiler_params=_params(("parallel", "arbitrary")),
        name="glu",
    )(h, w_in, w_in)


def _gates_kernel(h_ref, w_ref, side_ref, o_ref, side_out_ref):
    o_ref[...] = jax.nn.sigmoid(jnp.dot(h_ref[...], w_ref[...].astype(BF16), preferred_element_type=F32))
    side_out_ref[...] = side_ref[...].astype(side_out_ref.dtype)


def _gates_proj(h, w_in, col0, width, side):
    s, d = h.shape
    tm = PROJ_ROWS
    c0 = col0 // COL
    n_j = width // COL
    slab = side.shape[0] // ((s // tm) * n_j)
    assert slab * (s // tm) * n_j == side.shape[0]
    return pl.pallas_call(
        _gates_kernel,
        out_shape=(jax.ShapeDtypeStruct((s, width), F32), jax.ShapeDtypeStruct(side.shape, BF16)),
        grid=(s // tm, n_j),
        in_specs=[pl.BlockSpec((tm, d), lambda i, j: (i, 0)),
                  pl.BlockSpec((d, COL), lambda i, j: (0, c0 + j)),
                  pl.BlockSpec((slab, side.shape[1]), lambda i, j: (i * n_j + j, 0))],
        out_specs=(pl.BlockSpec((tm, COL), lambda i, j: (i, j)),
                   pl.BlockSpec((slab, side.shape[1]), lambda i, j: (i * n_j + j, 0))),
        compiler_params=_params(("parallel", "arbitrary")),
        name="gates",
    )(h, w_in, side)


ATTN_Q_BLOCKS = 4


def _attn_kernel(q_ref, kc_ref, kp_ref, vc_ref, vp_ref, o_ref, l_ref):
    n = pl.program_id(1)
    qi = lax.broadcasted_iota(jnp.int32, (ATTN_BLOCK, 2 * ATTN_BLOCK), 0)
    kj = lax.broadcasted_iota(jnp.int32, (ATTN_BLOCK, 2 * ATTN_BLOCK), 1)
    band = jnp.abs(kj - qi - ATTN_BLOCK // 2) <= ATTN_BLOCK // 2
    first_lo = jnp.where(n == 0, ATTN_BLOCK, 0)
    band_first = band & (kj >= first_lo)
    scale = HEAD_DIM ** -0.5
    for hh in range(HEADS_PER_GROUP):
        cols = slice(hh * HEAD_DIM, (hh + 1) * HEAD_DIM)
        for b in range(ATTN_Q_BLOCKS):
            rows = slice(b * ATTN_BLOCK, (b + 1) * ATTN_BLOCK)
            q = q_ref[rows, cols]
            if b == 0:
                k = jnp.concatenate([kp_ref[:, cols], kc_ref[0:ATTN_BLOCK, cols]], axis=0)
                v = jnp.concatenate([vp_ref[:, cols], vc_ref[0:ATTN_BLOCK, cols]], axis=0)
                valid = band_first
            else:
                band_rows = slice((b - 1) * ATTN_BLOCK, (b + 1) * ATTN_BLOCK)
                k = kc_ref[band_rows, cols]
                v = vc_ref[band_rows, cols]
                valid = band
            s = lax.dot_general(q, k, (((1,), (1,)), ((), ())), preferred_element_type=F32) * scale
            s = jnp.where(valid, s, MASK_VALUE)
            m = jnp.max(s, axis=-1, keepdims=True)
            p = jnp.exp(s - m)
            den = jnp.sum(p, axis=-1, keepdims=True)
            o = jnp.dot(p.astype(BF16), v, preferred_element_type=F32)
            o_ref[rows, cols] = o * (1.0 / den)
            l_ref[rows, cols] = jnp.broadcast_to(m + jnp.log(den), (ATTN_BLOCK, HEAD_DIM))


def _dilated_attention(qkv):
    dilation, sub_len, _ = qkv.shape
    qt = ATTN_Q_BLOCKS * ATTN_BLOCK

    def cur(part):
        return pl.BlockSpec((None, qt, COL), lambda r, n: (r, n, part))

    def prev(part):
        return pl.BlockSpec((None, ATTN_BLOCK, COL),
                            lambda r, n: (r, jnp.maximum(n * ATTN_Q_BLOCKS - 1, 0), part))

    out_sds = jax.ShapeDtypeStruct((dilation, sub_len, GROUP_WIDTH), F32)
    out_spec = pl.BlockSpec((None, qt, GROUP_WIDTH), lambda r, n: (r, n, 0))
    return pl.pallas_call(
        _attn_kernel,
        out_shape=(out_sds, out_sds),
        grid=(dilation, sub_len // qt),
        in_specs=[cur(0), cur(1), prev(1), cur(2), prev(2)],
        out_specs=(out_spec, out_spec),
        compiler_params=_params(("parallel", "arbitrary")),
        name=f"attn_d{dilation}",
    )(qkv, qkv, qkv, qkv, qkv)


def _mix_kernel(o0_ref, o1_ref, o2_ref, l0_ref, l1_ref, l2_ref, u_ref, uh_ref, ga_ref, gc_ref,
                wao_ref, cw_ref, cb_ref, lg_ref, lb_ref, wco_ref, side_ref,
                out_ref, side_out_ref, ucat_ref, nat_ref, shift_ref):
    i = pl.program_id(0)
    tm = u_ref.shape[0]
    side_out_ref[...] = side_ref[...].astype(side_out_ref.dtype)

    def natural(ref, slot):
        dilation, rows, _ = ref.shape
        if dilation == 1:
            return ref[0]
        for hh in range(HEADS_PER_GROUP):
            for r in range(dilation):
                nat_ref[slot, hh, pl.ds(r, rows, stride=dilation), :] = ref[r, :, hh * HEAD_DIM:(hh + 1) * HEAD_DIM]
        return jnp.concatenate([nat_ref[slot, hh] for hh in range(HEADS_PER_GROUP)], axis=1)

    l0, l1, l2 = natural(l0_ref, 0), natural(l1_ref, 1), natural(l2_ref, 2)
    m = jnp.maximum(jnp.maximum(l0, l1), l2)
    e0, e1, e2 = jnp.exp(l0 - m), jnp.exp(l1 - m), jnp.exp(l2 - m)
    attn = (e0 * natural(o0_ref, 3) + e1 * natural(o1_ref, 4) + e2 * natural(o2_ref, 5)) / (e0 + e1 + e2)
    attn_o = jnp.dot(attn.astype(BF16), wao_ref[...], preferred_element_type=F32)
    ucat_ref[0:CONV_HALO, :] = jnp.where(i > 0, uh_ref[...], 0.0)
    ucat_ref[CONV_HALO:, :] = u_ref[...]
    off = CONV_HALO - (CONV_WIDTH - 1)
    span = tm + CONV_HALO - SUBLANES
    for s in range(1, SUBLANES):
        shift_ref[s - 1] = ucat_ref[s:s + span, :]
    conv = jnp.zeros(u_ref.shape, F32) + cb_ref[...]
    for w in range(CONV_WIDTH):
        q, s = divmod(off + w, SUBLANES)
        src = ucat_ref if s == 0 else shift_ref.at[s - 1]
        conv = conv + src[q * SUBLANES:q * SUBLANES + tm, :] * cw_ref[w:w + 1, :]
    mu = jnp.mean(conv, axis=-1, keepdims=True)
    cen = conv - mu
    var = jnp.mean(cen * cen, axis=-1, keepdims=True)
    y = cen * lax.rsqrt(var + NORM_EPS) * lg_ref[...] + lb_ref[...]
    y = y * jax.nn.sigmoid(y)
    conv_o = jnp.dot(y.astype(BF16), wco_ref[...], preferred_element_type=F32)
    out_ref[...] = (ga_ref[...] * attn_o + gc_ref[...] * conv_o).astype(out_ref.dtype)


def _mix(outs, lses, u, gates, w_attn_o, conv_w, conv_b, ln_g, ln_b, w_conv_o, side):
    s, cc = u.shape
    d = w_attn_o.shape[1]
    tm = 256
    row = lambda i: (i, 0)
    const = lambda i: (0, 0)
    cvec = pl.BlockSpec((1, cc), const)
    slab = side.shape[0] // (s // tm)
    assert slab * (s // tm) == side.shape[0]
    side_spec = pl.BlockSpec((slab, side.shape[1]), row)

    def grp(arr):
        dilation = arr.shape[0]
        return pl.BlockSpec((dilation, tm // dilation, GROUP_WIDTH), lambda i: (0, i, 0))

    return pl.pallas_call(
        _mix_kernel,
        out_shape=(jax.ShapeDtypeStruct((s, d), BF16), jax.ShapeDtypeStruct(side.shape, BF16)),
        grid=(s // tm,),
        in_specs=[grp(a) for a in (*outs, *lses)] + [
            pl.BlockSpec((tm, cc), row),
            pl.BlockSpec((CONV_HALO, cc), lambda i: (jnp.maximum(i * (tm // CONV_HALO) - 1, 0), 0)),
            pl.BlockSpec((tm, d), lambda i: (i, 0)),
            pl.BlockSpec((tm, d), lambda i: (i, 1)),
            pl.BlockSpec((GROUP_WIDTH, d), const),
            pl.BlockSpec((CONV_WIDTH, cc), const),
            cvec, cvec, cvec,
            pl.BlockSpec((cc, d), const),
            side_spec],
        out_specs=(pl.BlockSpec((tm, d), row), side_spec),
        scratch_shapes=[pltpu.VMEM((tm + CONV_HALO, cc), F32),
                        pltpu.VMEM((2 * N_GROUPS, HEADS_PER_GROUP, tm, HEAD_DIM), F32),
                        pltpu.VMEM((SUBLANES - 1, tm + CONV_HALO - SUBLANES, cc), F32)],
        compiler_params=_params(("parallel",)),
        name="mix",
    )(*outs, *lses, u, u, gates, gates, w_attn_o, conv_w, conv_b, ln_g, ln_b, w_conv_o, side)


def _out_proj_kernel(m_ref, w_ref, x_ref, gate_ref, g1_ref, g2_ref, scale_ref, shift_ref,
                     x1_ref, h2t_ref):
    y = jnp.dot(m_ref[...], w_ref[...], preferred_element_type=F32)
    x1 = x_ref[...] + gate_ref[...] * (_rms(y) * g1_ref[...])
    x1_ref[...] = x1
    h2 = (_rms(x1) * g2_ref[...]) * (1.0 + scale_ref[...]) + shift_ref[...]
    h2t_ref[...] = h2.T.astype(h2t_ref.dtype)


def _out_proj(merged, w_out, x, gate1, g1, g2, scale2, shift2):
    s, d = x.shape
    tm = 512
    row = lambda i: (i, 0)
    vec = pl.BlockSpec((1, d), lambda i: (0, 0))
    return pl.pallas_call(
        _out_proj_kernel,
        out_shape=(jax.ShapeDtypeStruct((s, d), F32), jax.ShapeDtypeStruct((d, s), BF16)),
        grid=(s // tm,),
        in_specs=[pl.BlockSpec((tm, d), row), pl.BlockSpec((d, d), lambda i: (0, 0)),
                  pl.BlockSpec((tm, d), row), vec, vec, vec, vec, vec],
        out_specs=(pl.BlockSpec((tm, d), row), pl.BlockSpec((d, tm), lambda i: (0, i))),
        compiler_params=_params(("parallel",)),
        name="out_proj",
    )(merged, w_out, x, gate1, g1, g2, scale2, shift2)


def _odd_even_merge_sort(lo, hi):
    def merge(lo, hi, r):
        step = 2 * r
        if step < hi - lo:
            yield from merge(lo, hi, step)
            yield from merge(lo + r, hi, step)
            for i in range(lo + r, hi - r, step):
                yield (i, i + r)
        else:
            yield (lo, lo + r)

    if hi > lo:
        mid = lo + (hi - lo) // 2
        yield from _odd_even_merge_sort(lo, mid)
        yield from _odd_even_merge_sort(mid + 1, hi)
        yield from merge(lo, hi, 1)


def _bitonic_merge(n):
    half = n // 2
    while half >= 1:
        for i in range(n):
            if (i // half) % 2 == 0:
                yield (i, i + half)
        half //= 2


_SORT_PAIRS = {n: tuple(_odd_even_merge_sort(0, n - 1)) for n in (PEER_TOPK // 2, PEER_TOPK)}
_MERGE_PAIRS = tuple(_bitonic_merge(PEER_TOPK))


def _compare_exchange(v, pairs):
    for i, j in pairs:
        v[i], v[j] = jnp.maximum(v[i], v[j]), jnp.minimum(v[i], v[j])


def _top16_rows(x):
    blocks = x.shape[0] // SUBLANES
    assert x.shape[0] == blocks * SUBLANES and blocks <= PEER_TOPK
    n = PEER_TOPK // 2 if blocks <= PEER_TOPK // 2 else PEER_TOPK
    v = [x[k * SUBLANES:(k + 1) * SUBLANES, :] for k in range(blocks)]
    v += [jnp.full_like(v[0], NEG_INF)] * (n - blocks)
    _compare_exchange(v, _SORT_PAIRS[n])
    shift = SUBLANES // 2
    while shift >= 1:
        partner = [pltpu.roll(blk, shift, axis=0) for blk in v]
        if len(v) < PEER_TOPK:
            v = v + partner[::-1]
        else:
            v = [jnp.maximum(v[k], partner[PEER_TOPK - 1 - k]) for k in range(PEER_TOPK)]
        _compare_exchange(v, _MERGE_PAIRS)
        shift //= 2
    return [blk[0:1, :] for blk in v]


def _peer_query_kernel(h2t_ref, wqt_ref, keys_ref, e1_ref, e2_ref, theta_ref,
                       qt_ref, top_ref, cand_ref):
    qt_ref[...] = jnp.dot(wqt_ref[...], h2t_ref[...], preferred_element_type=F32)
    for h in range(PEER_HEADS):
        u = []
        for side in range(2):
            r0 = (2 * h + side) * PEER_HALF
            qc = qt_ref[r0:r0 + PEER_HALF, :].astype(BF16)
            sc = jnp.dot(keys_ref[h, side].astype(BF16), qc, preferred_element_type=F32)
            us = jnp.exp(sc - jnp.max(sc, axis=0, keepdims=True))
            vals = _top16_rows(us)
            for k in range(PEER_TOPK):
                top_ref[side, k:k + 1, :] = vals[k]
            u.append(us)
        cand_ref[...] = jnp.full(cand_ref.shape, -1.0, F32)
        off = 0
        for a, nb in enumerate(_CAND_ROWS):
            cand_ref[off:off + nb, :] = top_ref[0, a:a + 1, :] * top_ref[1, 0:nb, :]
            off += nb
        cand = cand_ref[...]
        selected = cand >= _top16_rows(cand)[-1]
        inv_z = 0.5 / jnp.sum(jnp.where(selected, cand, 0.0), axis=0, keepdims=True)
        off = 0
        for a, nb in enumerate(_CAND_ROWS):
            cand_ref[off:off + nb, :] = (top_ref[0, a:a + 1, :] * inv_z) * top_ref[1, 0:nb, :]
            off += nb
        theta = jnp.min(jnp.where(selected, cand_ref[...], jnp.inf), axis=0, keepdims=True)
        e1 = u[0] * inv_z
        for blk in range(PEER_N_KEYS // 8):
            e1_ref[h, blk] = e1[blk * 8:(blk + 1) * 8, :]
        e2_ref[h] = u[1]
        theta_ref[h:h + 1, :] = theta


def _peer_query(h2t, wq_t, sub_keys):
    d, s = h2t.shape
    tt = 512
    nq = wq_t.shape[0]
    side_blocked = jax.ShapeDtypeStruct((PEER_HEADS, PEER_N_KEYS // 8, 8, s), F32)
    side_flat = jax.ShapeDtypeStruct((PEER_HEADS, PEER_N_KEYS, s), F32)
    blocked_spec = pl.BlockSpec((PEER_HEADS, PEER_N_KEYS // 8, 8, tt), lambda t: (0, 0, 0, t))
    flat_spec = pl.BlockSpec((PEER_HEADS, PEER_N_KEYS, tt), lambda t: (0, 0, t))
    return pl.pallas_call(
        _peer_query_kernel,
        out_shape=(side_blocked, side_flat, jax.ShapeDtypeStruct((PEER_HEADS, s), F32)),
        grid=(s // tt,),
        in_specs=[pl.BlockSpec((d, tt), lambda t: (0, t)),
                  pl.BlockSpec((nq, d), lambda t: (0, 0)),
                  pl.BlockSpec(sub_keys.shape, lambda t: (0, 0, 0, 0))],
        out_specs=(blocked_spec, flat_spec, pl.BlockSpec((PEER_HEADS, tt), lambda t: (0, t))),
        scratch_shapes=[pltpu.VMEM((nq, tt), F32),
                        pltpu.VMEM((2, PEER_TOPK, tt), F32),
                        pltpu.VMEM((_CAND_PAD, tt), F32)],
        compiler_params=_params(("parallel",)),
        name="peer_query",
    )(h2t, wq_t, sub_keys)


PEER_LANE_CHUNK = 128
PEER_HALF_KEYS = 4
PEER_HALF_TILE = PEER_HALF_KEYS * PEER_N_KEYS
PEER_TILE = 2 * PEER_HALF_TILE
PEER_TOKENS = 512


def _gate_act(hid_ref, act_ref, e1_ref, e2_ref, theta_ref, half):
    tt = hid_ref.shape[2]
    for ii in range(PEER_HALF_KEYS):
        rows = slice(ii * PEER_N_KEYS, (ii + 1) * PEER_N_KEYS)
        key = half * PEER_HALF_KEYS + ii
        for c in range(tt // PEER_LANE_CHUNK):
            lanes = slice(c * PEER_LANE_CHUNK, (c + 1) * PEER_LANE_CHUNK)
            w = None
            for h in range(PEER_HEADS):
                p = e1_ref[h, 0, key:key + 1, lanes] * e2_ref[h, :, lanes]
                kept = jnp.where(p >= theta_ref[h:h + 1, lanes], p, 0.0)
                w = kept if w is None else w + kept
            act_ref[half, rows, lanes] = (hid_ref[half, rows, lanes] * w).astype(act_ref.dtype)


def _project(down, h2t):
    x = jnp.dot(down, h2t, preferred_element_type=F32)
    return (x * (1.0 + lax.erf(x * INV_SQRT2))).reshape(2, PEER_HALF_TILE, h2t.shape[1])


def _peer_project_kernel(down_ref, h2t_ref, hid_ref):
    hid_ref[...] = _project(down_ref[...], h2t_ref[...])


def _peer_project_first(h2t, down):
    d = h2t.shape[0]
    return pl.pallas_call(
        _peer_project_kernel,
        out_shape=jax.ShapeDtypeStruct((2, PEER_HALF_TILE, PEER_TOKENS), F32),
        grid=(1,),
        in_specs=[pl.BlockSpec((PEER_TILE, d), lambda i: (0, 0)),
                  pl.BlockSpec((d, PEER_TOKENS), lambda i: (0, 0))],
        out_specs=pl.BlockSpec((2, PEER_HALF_TILE, PEER_TOKENS), lambda i: (0, 0, 0)),
        compiler_params=_params(("arbitrary",)),
        name="peer_project_first",
    )(down, h2t)


def _peer_dense_kernel(h2t_ref, down_ref, up_ref, e1_ref, e2_ref, theta_ref, hid0_ref,
                       x1_ref, gate_ref, g_ref, out_ref, hid_ref, act_ref):
    t = pl.program_id(0)
    e = pl.program_id(1)

    @pl.when((t == 0) & (e == 0))
    def _():
        hid_ref[...] = hid0_ref[...]

    @pl.when(e == 0)
    def _():
        out_ref[...] = jnp.zeros(out_ref.shape, F32)

    for half in range(2):
        _gate_act(hid_ref, act_ref, e1_ref, e2_ref, theta_ref, half)
    act = act_ref[...].reshape(PEER_TILE, act_ref.shape[2])
    out_ref[...] += lax.dot_general(act, up_ref[...], (((0,), (0,)), ((), ())), preferred_element_type=F32)
    hid_ref[...] = _project(down_ref[...], h2t_ref[...])

    @pl.when(e == pl.num_programs(1) - 1)
    def _():
        y = out_ref[...]
        out_ref[...] = x1_ref[...] + gate_ref[...] * (_rms(y) * g_ref[...])


def _peer_dense(h2t, down, up, e1, e2, theta, x1, gate2, g3):
    d, s = h2t.shape
    tt = PEER_TOKENS
    n_e = down.shape[0] // PEER_TILE
    n_t = s // tt
    vec = pl.BlockSpec((1, d), lambda t, e: (0, 0))
    next_e = lambda e: (e + 1) % n_e
    next_t = lambda t, e: jnp.minimum(t + (e + 1) // n_e, n_t - 1)
    return pl.pallas_call(
        _peer_dense_kernel,
        out_shape=jax.ShapeDtypeStruct((s, d), F32),
        grid=(n_t, n_e),
        in_specs=[pl.BlockSpec((d, tt), lambda t, e: (0, next_t(t, e))),
                  pl.BlockSpec((PEER_TILE, d), lambda t, e: (next_e(e), 0)),
                  pl.BlockSpec((PEER_TILE, d), lambda t, e: (e, 0)),
                  pl.BlockSpec((PEER_HEADS, 1, 2 * PEER_HALF_KEYS, tt), lambda t, e: (0, e, 0, t)),
                  pl.BlockSpec((PEER_HEADS, PEER_N_KEYS, tt), lambda t, e: (0, 0, t)),
                  pl.BlockSpec((PEER_HEADS, tt), lambda t, e: (0, t)),
                  pl.BlockSpec((2, PEER_HALF_TILE, tt), lambda t, e: (0, 0, 0)),
                  pl.BlockSpec((tt, d), lambda t, e: (t, 0)),
                  vec, vec],
        out_specs=pl.BlockSpec((tt, d), lambda t, e: (t, 0)),
        scratch_shapes=[pltpu.VMEM((2, PEER_HALF_TILE, tt), F32),
                        pltpu.VMEM((2, PEER_HALF_TILE, tt), BF16)],
        compiler_params=_params(("arbitrary", "arbitrary")),
        name="peer_dense",
    )(h2t, down, up, e1, e2, theta, _peer_project_first(h2t, down), x1, gate2, g3)


def kernel(x, c, positions, ada_w, ada_b, norm_gains, w_in, w_attn_o, conv_w, conv_b, conv_ln_g,
           conv_ln_b, w_conv_o, w_out, peer_w_q, peer_sub_keys, peer_down, peer_up):
    batch, seq, d = x.shape
    depth = ada_w.shape[0]
    assert batch == 1, "kernels are written for a single sequence"
    assert all(window // dilation == ATTN_BLOCK for window, dilation in DILATED_GROUPS)
    cc = conv_w.shape[-1]
    xs = x[0]
    inv_freq = ROPE_THETA ** (-jnp.arange(0, HEAD_DIM, 2, dtype=F32) / HEAD_DIM)
    invf_row = jnp.concatenate([inv_freq, inv_freq])[None, :]
    cos, sin = _rope_tables(positions[0].astype(F32)[:, None], invf_row)
    row = lambda v: v[None, :]
    for l in range(depth):
        mod = _modulation(c[0][:, None], ada_w[l], ada_b[l][None, :])
        shift1, scale1, gate1, shift2, scale2, gate2 = [mod[:, k * d:(k + 1) * d] for k in range(6)]
        g = norm_gains[l]
        w_in_b = w_in[l]

        h = _norm_mod(xs, row(g[0]), scale1, shift1)
        u = _glu_proj(h, w_in_b, cc)
        gates, down_b = _gates_proj(h, w_in_b, 3 * ATTN_WIDTH + 2 * cc, 2 * d, peer_down[l])
        outs, lses = [], []
        for gi, (_, dilation) in enumerate(DILATED_GROUPS):
            o, lse = _dilated_attention(_qkv_proj(h, w_in_b, cos, sin, gi, dilation))
            outs.append(o)
            lses.append(lse)
        merged, up_b = _mix(outs, lses, u, gates, w_attn_o[l].astype(BF16), conv_w[l][:, 0, :],
                            row(conv_b[l]), row(conv_ln_g[l]), row(conv_ln_b[l]), w_conv_o[l].astype(BF16),
                            peer_up[l])
        x1, h2t = _out_proj(merged, w_out[l].astype(BF16), xs, gate1, row(g[1]), row(g[2]),
                            scale2, shift2)

        e1, e2, theta = _peer_query(h2t, peer_w_q[l].T.astype(BF16), peer_sub_keys[l])
        xs = _peer_dense(h2t, down_b, up_b, e1, e2, theta, x1, gate2, row(g[3]))
    return xs[None]
```

```python
import math

import jax
import jax.numpy as jnp
from jax import lax
from jax.experimental import pallas as pl
from jax.experimental.pallas import tpu as pltpu

F32 = jnp.float32
BF16 = jnp.bfloat16

HEAD_DIM = 128
HEADS_PER_GROUP = 4
DILATED_GROUPS = ((128, 1), (512, 4), (2048, 16))
N_GROUPS = len(DILATED_GROUPS)
GROUP_WIDTH = HEADS_PER_GROUP * HEAD_DIM
ATTN_WIDTH = N_GROUPS * GROUP_WIDTH
ATTN_BLOCK = 128
ROPE_THETA = 10000.0
CONV_WIDTH = 31
CONV_HALO = 32
PEER_HEADS = 8
PEER_N_KEYS = 128
PEER_HALF = 128
PEER_TOPK = 16
NORM_EPS = 1e-6
MASK_VALUE = -1e30
NEG_INF = float("-inf")
INV_SQRT2 = 1.0 / math.sqrt(2.0)

SUBLANES = 8
COL = 512
PROJ_ROWS = 2048
VMEM_LIMIT = 56 * 1024 * 1024

_CAND_ROWS = [PEER_TOPK // (a + 1) for a in range(PEER_TOPK)]
_N_CAND = sum(_CAND_ROWS)
_CAND_PAD = -(-_N_CAND // 8) * 8


def _params(sem):
    return pltpu.CompilerParams(dimension_semantics=sem, vmem_limit_bytes=VMEM_LIMIT)


def _rms(x):
    return x * lax.rsqrt(jnp.mean(x * x, axis=-1, keepdims=True) + NORM_EPS)


def _mod_kernel(c_ref, w_ref, b_ref, o_ref):
    rows = 256

    def body(k, acc):
        r = pl.multiple_of(k * rows, rows)
        c = c_ref[pl.ds(r, rows), :]
        sc = c * jax.nn.sigmoid(c)
        return acc + jnp.sum(w_ref[pl.ds(r, rows), :] * sc, axis=0, keepdims=True)

    acc = lax.fori_loop(0, w_ref.shape[0] // rows, body, jnp.zeros(o_ref.shape, F32))
    o_ref[...] = acc + b_ref[...]


def _modulation(c_col, w, b):
    d, n = w.shape
    tn = 1024
    return pl.pallas_call(
        _mod_kernel,
        out_shape=jax.ShapeDtypeStruct((1, n), F32),
        grid=(n // tn,),
        in_specs=[pl.BlockSpec((d, 1), lambda j: (0, 0)),
                  pl.BlockSpec((d, tn), lambda j: (0, j)),
                  pl.BlockSpec((1, tn), lambda j: (0, j))],
        out_specs=pl.BlockSpec((1, tn), lambda j: (0, j)),
        compiler_params=_params(("parallel",)),
        name="mod",
    )(c_col, w, b)


def _rope_kernel(pos_ref, invf_ref, cos_ref, sin_ref):
    ang = pos_ref[...] * invf_ref[...]
    cos_ref[...] = jnp.cos(ang)
    s = jnp.sin(ang)
    lane = lax.broadcasted_iota(jnp.int32, s.shape, 1)
    sin_ref[...] = jnp.where(lane < HEAD_DIM // 2, -s, s)


def _rope_tables(pos_col, invf_row):
    s = pos_col.shape[0]
    ts = 2048
    return pl.pallas_call(
        _rope_kernel,
        out_shape=(jax.ShapeDtypeStruct((s, HEAD_DIM), F32),) * 2,
        grid=(s // ts,),
        in_specs=[pl.BlockSpec((ts, 1), lambda i: (i, 0)),
                  pl.BlockSpec((1, HEAD_DIM), lambda i: (0, 0))],
        out_specs=(pl.BlockSpec((ts, HEAD_DIM), lambda i: (i, 0)),) * 2,
        compiler_params=_params(("parallel",)),
        name="rope",
    )(pos_col, invf_row)


def _norm_mod_kernel(x_ref, g_ref, scale_ref, shift_ref, o_ref):
    y = _rms(x_ref[...]) * g_ref[...]
    o_ref[...] = (y * (1.0 + scale_ref[...]) + shift_ref[...]).astype(o_ref.dtype)


def _norm_mod(x, g, scale, shift):
    s, d = x.shape
    tm = 1024
    vec = pl.BlockSpec((1, d), lambda i: (0, 0))
    return pl.pallas_call(
        _norm_mod_kernel,
        out_shape=jax.ShapeDtypeStruct((s, d), BF16),
        grid=(s // tm,),
        in_specs=[pl.BlockSpec((tm, d), lambda i: (i, 0)), vec, vec, vec],
        out_specs=pl.BlockSpec((tm, d), lambda i: (i, 0)),
        compiler_params=_params(("parallel",)),
        name="norm_mod",
    )(x, g, scale, shift)


def _qkv_kernel(h_ref, w_ref, cos_ref, sin_ref, o_ref, stage_ref):
    acc = jnp.dot(h_ref[...], w_ref[...].astype(BF16), preferred_element_type=F32)
    part = pl.program_id(1)

    heads = [slice(hh * HEAD_DIM, (hh + 1) * HEAD_DIM) for hh in range(HEADS_PER_GROUP)]

    @pl.when(part < 2)
    def _():
        cos = cos_ref[...]
        sin = sin_ref[...]
        for hh, cols in enumerate(heads):
            t = acc[:, cols]
            stage_ref[hh] = t * cos + pltpu.roll(t, HEAD_DIM // 2, axis=1) * sin

    @pl.when(part == 2)
    def _():
        for hh, cols in enumerate(heads):
            stage_ref[hh] = acc[:, cols]

    dilation, rows, _ = o_ref.shape
    for r in range(dilation):
        for hh, cols in enumerate(heads):
            o_ref[r, :, cols] = stage_ref[hh, pl.ds(r, rows, stride=dilation), :].astype(o_ref.dtype)


def _qkv_proj(h, w_in, cos, sin, group, dilation):
    s, d = h.shape
    tm = PROJ_ROWS
    return pl.pallas_call(
        _qkv_kernel,
        out_shape=jax.ShapeDtypeStruct((dilation, s // dilation, 3 * GROUP_WIDTH), BF16),
        grid=(s // tm, 3),
        in_specs=[pl.BlockSpec((tm, d), lambda i, j: (i, 0)),
                  pl.BlockSpec((d, COL), lambda i, j: (0, j * N_GROUPS + group)),
                  pl.BlockSpec((tm, HEAD_DIM), lambda i, j: (i, 0)),
                  pl.BlockSpec((tm, HEAD_DIM), lambda i, j: (i, 0))],
        out_specs=pl.BlockSpec((dilation, tm // dilation, COL), lambda i, j: (0, i, j)),
        scratch_shapes=[pltpu.VMEM((HEADS_PER_GROUP, tm, HEAD_DIM), F32)],
        compiler_params=_params(("parallel", "arbitrary")),
        name=f"qkv_d{dilation}",
    )(h, w_in, cos, sin)


def _glu_kernel(h_ref, wa_ref, wb_ref, o_ref):
    h = h_ref[...]
    a = jnp.dot(h, wa_ref[...].astype(BF16), preferred_element_type=F32)
    b = jnp.dot(h, wb_ref[...].astype(BF16), preferred_element_type=F32)
    o_ref[...] = a * jax.nn.sigmoid(b)


def _glu_proj(h, w_in, conv_channels):
    s, d = h.shape
    tm = PROJ_ROWS
    a0 = 3 * ATTN_WIDTH // COL
    nb = conv_channels // COL
    return pl.pallas_call(
        _glu_kernel,
        out_shape=jax.ShapeDtypeStruct((s, conv_channels), F32),
        grid=(s // tm, nb),
        in_specs=[pl.BlockSpec((tm, d), lambda i, j: (i, 0)),
                  pl.BlockSpec((d, COL), lambda i, j: (0, a0 + j)),
                  pl.BlockSpec((d, COL), lambda i, j: (0, a0 + nb + j))],
        out_specs=pl.BlockSpec((tm, COL), lambda i, j: (i, j)),
        compiler_params=_params(("parallel", "arbitrary")),
        name="glu",
    )(h, w_in, w_in)


def _gates_kernel(h_ref, w_ref, side_ref, o_ref, side_out_ref):
    o_ref[...] = jax.nn.sigmoid(jnp.dot(h_ref[...], w_ref[...].astype(BF16), preferred_element_type=F32))
    side_out_ref[...] = side_ref[...].astype(side_out_ref.dtype)


def _gates_proj(h, w_in, col0, width, side):
    s, d = h.shape
    tm = PROJ_ROWS
    c0 = col0 // COL
    n_j = width // COL
    slab = side.shape[0] // ((s // tm) * n_j)
    assert slab * (s // tm) * n_j == side.shape[0]
    return pl.pallas_call(
        _gates_kernel,
        out_shape=(jax.ShapeDtypeStruct((s, width), F32), jax.ShapeDtypeStruct(side.shape, BF16)),
        grid=(s // tm, n_j),
        in_specs=[pl.BlockSpec((tm, d), lambda i, j: (i, 0)),
                  pl.BlockSpec((d, COL), lambda i, j: (0, c0 + j)),
                  pl.BlockSpec((slab, side.shape[1]), lambda i, j: (i * n_j + j, 0))],
        out_specs=(pl.BlockSpec((tm, COL), lambda i, j: (i, j)),
                   pl.BlockSpec((slab, side.shape[1]), lambda i, j: (i * n_j + j, 0))),
        compiler_params=_params(("parallel", "arbitrary")),
        name="gates",
    )(h, w_in, side)


ATTN_Q_BLOCKS = 4


def _attn_kernel(q_ref, kc_ref, kp_ref, vc_ref, vp_ref, o_ref, l_ref):
    n = pl.program_id(1)
    qi = lax.broadcasted_iota(jnp.int32, (ATTN_BLOCK, 2 * ATTN_BLOCK), 0)
    kj = lax.broadcasted_iota(jnp.int32, (ATTN_BLOCK, 2 * ATTN_BLOCK), 1)
    band = jnp.abs(kj - qi - ATTN_BLOCK // 2) <= ATTN_BLOCK // 2
    first_lo = jnp.where(n == 0, ATTN_BLOCK, 0)
    band_first = band & (kj >= first_lo)
    scale = HEAD_DIM ** -0.5
    for hh in range(HEADS_PER_GROUP):
        cols = slice(hh * HEAD_DIM, (hh + 1) * HEAD_DIM)
        for b in range(ATTN_Q_BLOCKS):
            rows = slice(b * ATTN_BLOCK, (b + 1) * ATTN_BLOCK)
            q = q_ref[rows, cols]
            if b == 0:
                k = jnp.concatenate([kp_ref[:, cols], kc_ref[0:ATTN_BLOCK, cols]], axis=0)
                v = jnp.concatenate([vp_ref[:, cols], vc_ref[0:ATTN_BLOCK, cols]], axis=0)
                valid = band_first
            else:
                band_rows = slice((b - 1) * ATTN_BLOCK, (b + 1) * ATTN_BLOCK)
                k = kc_ref[band_rows, cols]
                v = vc_ref[band_rows, cols]
                valid = band
            s = lax.dot_general(q, k, (((1,), (1,)), ((), ())), preferred_element_type=F32) * scale
            s = jnp.where(valid, s, MASK_VALUE)
            m = jnp.max(s, axis=-1, keepdims=True)
            p = jnp.exp(s - m)
            den = jnp.sum(p, axis=-1, keepdims=True)
            o = jnp.dot(p.astype(BF16), v, preferred_element_type=F32)
            o_ref[rows, cols] = o * (1.0 / den)
            l_ref[rows, cols] = jnp.broadcast_to(m + jnp.log(den), (ATTN_BLOCK, HEAD_DIM))


def _dilated_attention(qkv):
    dilation, sub_len, _ = qkv.shape
    qt = ATTN_Q_BLOCKS * ATTN_BLOCK

    def cur(part):
        return pl.BlockSpec((None, qt, COL), lambda r, n: (r, n, part))

    def prev(part):
        return pl.BlockSpec((None, ATTN_BLOCK, COL),
                            lambda r, n: (r, jnp.maximum(n * ATTN_Q_BLOCKS - 1, 0), part))

    out_sds = jax.ShapeDtypeStruct((dilation, sub_len, GROUP_WIDTH), F32)
    out_spec = pl.BlockSpec((None, qt, GROUP_WIDTH), lambda r, n: (r, n, 0))
    return pl.pallas_call(
        _attn_kernel,
        out_shape=(out_sds, out_sds),
        grid=(dilation, sub_len // qt),
        in_specs=[cur(0), cur(1), prev(1), cur(2), prev(2)],
        out_specs=(out_spec, out_spec),
        compiler_params=_params(("parallel", "arbitrary")),
        name=f"attn_d{dilation}",
    )(qkv, qkv, qkv, qkv, qkv)


def _mix_kernel(o0_ref, o1_ref, o2_ref, l0_ref, l1_ref, l2_ref, u_ref, uh_ref, ga_ref, gc_ref,
                wao_ref, cw_ref, cb_ref, lg_ref, lb_ref, wco_ref, side_ref,
                out_ref, side_out_ref, ucat_ref, nat_ref, shift_ref):
    i = pl.program_id(0)
    tm = u_ref.shape[0]
    side_out_ref[...] = side_ref[...].astype(side_out_ref.dtype)

    def natural(ref, slot):
        dilation, rows, _ = ref.shape
        if dilation == 1:
            return ref[0]
        for hh in range(HEADS_PER_GROUP):
            for r in range(dilation):
                nat_ref[slot, hh, pl.ds(r, rows, stride=dilation), :] = ref[r, :, hh * HEAD_DIM:(hh + 1) * HEAD_DIM]
        return jnp.concatenate([nat_ref[slot, hh] for hh in range(HEADS_PER_GROUP)], axis=1)

    l0, l1, l2 = natural(l0_ref, 0), natural(l1_ref, 1), natural(l2_ref, 2)
    m = jnp.maximum(jnp.maximum(l0, l1), l2)
    e0, e1, e2 = jnp.exp(l0 - m), jnp.exp(l1 - m), jnp.exp(l2 - m)
    attn = (e0 * natural(o0_ref, 3) + e1 * natural(o1_ref, 4) + e2 * natural(o2_ref, 5)) / (e0 + e1 + e2)
    attn_o = jnp.dot(attn.astype(BF16), wao_ref[...], preferred_element_type=F32)
    ucat_ref[0:CONV_HALO, :] = jnp.where(i > 0, uh_ref[...], 0.0)
    ucat_ref[CONV_HALO:, :] = u_ref[...]
    off = CONV_HALO - (CONV_WIDTH - 1)
    span = tm + CONV_HALO - SUBLANES
    for s in range(1, SUBLANES):
        shift_ref[s - 1] = ucat_ref[s:s + span, :]
    conv = jnp.zeros(u_ref.shape, F32) + cb_ref[...]
    for w in range(CONV_WIDTH):
        q, s = divmod(off + w, SUBLANES)
        src = ucat_ref if s == 0 else shift_ref.at[s - 1]
        conv = conv + src[q * SUBLANES:q * SUBLANES + tm, :] * cw_ref[w:w + 1, :]
    mu = jnp.mean(conv, axis=-1, keepdims=True)
    cen = conv - mu
    var = jnp.mean(cen * cen, axis=-1, keepdims=True)
    y = cen * lax.rsqrt(var + NORM_EPS) * lg_ref[...] + lb_ref[...]
    y = y * jax.nn.sigmoid(y)
    conv_o = jnp.dot(y.astype(BF16), wco_ref[...], preferred_element_type=F32)
    out_ref[...] = (ga_ref[...] * attn_o + gc_ref[...] * conv_o).astype(out_ref.dtype)


def _mix(outs, lses, u, gates, w_attn_o, conv_w, conv_b, ln_g, ln_b, w_conv_o, side):
    s, cc = u.shape
    d = w_attn_o.shape[1]
    tm = 256
    row = lambda i: (i, 0)
    const = lambda i: (0, 0)
    cvec = pl.BlockSpec((1, cc), const)
    slab = side.shape[0] // (s // tm)
    assert slab * (s // tm) == side.shape[0]
    side_spec = pl.BlockSpec((slab, side.shape[1]), row)

    def grp(arr):
        dilation = arr.shape[0]
        return pl.BlockSpec((dilation, tm // dilation, GROUP_WIDTH), lambda i: (0, i, 0))

    return pl.pallas_call(
        _mix_kernel,
        out_shape=(jax.ShapeDtypeStruct((s, d), BF16), jax.ShapeDtypeStruct(side.shape, BF16)),
        grid=(s // tm,),
        in_specs=[grp(a) for a in (*outs, *lses)] + [
            pl.BlockSpec((tm, cc), row),
            pl.BlockSpec((CONV_HALO, cc), lambda i: (jnp.maximum(i * (tm // CONV_HALO) - 1, 0), 0)),
            pl.BlockSpec((tm, d), lambda i: (i, 0)),
            pl.BlockSpec((tm, d), lambda i: (i, 1)),
            pl.BlockSpec((GROUP_WIDTH, d), const),
            pl.BlockSpec((CONV_WIDTH, cc), const),
            cvec, cvec, cvec,
            pl.BlockSpec((cc, d), const),
            side_spec],
        out_specs=(pl.BlockSpec((tm, d), row), side_spec),
        scratch_shapes=[pltpu.VMEM((tm + CONV_HALO, cc), F32),
                        pltpu.VMEM((2 * N_GROUPS, HEADS_PER_GROUP, tm, HEAD_DIM), F32),
                        pltpu.VMEM((SUBLANES - 1, tm + CONV_HALO - SUBLANES, cc), F32)],
        compiler_params=_params(("parallel",)),
        name="mix",
    )(*outs, *lses, u, u, gates, gates, w_attn_o, conv_w, conv_b, ln_g, ln_b, w_conv_o, side)


def _out_proj_kernel(m_ref, w_ref, x_ref, gate_ref, g1_ref, g2_ref, scale_ref, shift_ref,
                     x1_ref, h2t_ref):
    y = jnp.dot(m_ref[...], w_ref[...], preferred_element_type=F32)
    x1 = x_ref[...] + gate_ref[...] * (_rms(y) * g1_ref[...])
    x1_ref[...] = x1
    h2 = (_rms(x1) * g2_ref[...]) * (1.0 + scale_ref[...]) + shift_ref[...]
    h2t_ref[...] = h2.T.astype(h2t_ref.dtype)


def _out_proj(merged, w_out, x, gate1, g1, g2, scale2, shift2):
    s, d = x.shape
    tm = 512
    row = lambda i: (i, 0)
    vec = pl.BlockSpec((1, d), lambda i: (0, 0))
    return pl.pallas_call(
        _out_proj_kernel,
        out_shape=(jax.ShapeDtypeStruct((s, d), F32), jax.ShapeDtypeStruct((d, s), BF16)),
        grid=(s // tm,),
        in_specs=[pl.BlockSpec((tm, d), row), pl.BlockSpec((d, d), lambda i: (0, 0)),
                  pl.BlockSpec((tm, d), row), vec, vec, vec, vec, vec],
        out_specs=(pl.BlockSpec((tm, d), row), pl.BlockSpec((d, tm), lambda i: (0, i))),
        compiler_params=_params(("parallel",)),
        name="out_proj",
    )(merged, w_out, x, gate1, g1, g2, scale2, shift2)


def _odd_even_merge_sort(lo, hi):
    def merge(lo, hi, r):
        step = 2 * r
        if step < hi - lo:
            yield from merge(lo, hi, step)
            yield from merge(lo + r, hi, step)
            for i in range(lo + r, hi - r, step):
                yield (i, i + r)
        else:
            yield (lo, lo + r)

    if hi > lo:
        mid = lo + (hi - lo) // 2
        yield from _odd_even_merge_sort(lo, mid)
        yield from _odd_even_merge_sort(mid + 1, hi)
        yield from merge(lo, hi, 1)


def _bitonic_merge(n):
    half = n // 2
    while half >= 1:
        for i in range(n):
            if (i // half) % 2 == 0:
                yield (i, i + half)
        half //= 2


_SORT_PAIRS = {n: tuple(_odd_even_merge_sort(0, n - 1)) for n in (PEER_TOPK // 2, PEER_TOPK)}
_MERGE_PAIRS = tuple(_bitonic_merge(PEER_TOPK))


def _compare_exchange(v, pairs):
    for i, j in pairs:
        v[i], v[j] = jnp.maximum(v[i], v[j]), jnp.minimum(v[i], v[j])


def _top16_rows(x):
    blocks = x.shape[0] // SUBLANES
    assert x.shape[0] == blocks * SUBLANES and blocks <= PEER_TOPK
    n = PEER_TOPK // 2 if blocks <= PEER_TOPK // 2 else PEER_TOPK
    v = [x[k * SUBLANES:(k + 1) * SUBLANES, :] for k in range(blocks)]
    v += [jnp.full_like(v[0], NEG_INF)] * (n - blocks)
    _compare_exchange(v, _SORT_PAIRS[n])
    shift = SUBLANES // 2
    while shift >= 1:
        partner = [pltpu.roll(blk, shift, axis=0) for blk in v]
        if len(v) < PEER_TOPK:
            v = v + partner[::-1]
        else:
            v = [jnp.maximum(v[k], partner[PEER_TOPK - 1 - k]) for k in range(PEER_TOPK)]
        _compare_exchange(v, _MERGE_PAIRS)
        shift //= 2
    return [blk[0:1, :] for blk in v]


def _peer_query_kernel(h2t_ref, wqt_ref, keys_ref, e1_ref, e2_ref, theta_ref,
                       qt_ref, top_ref, cand_ref):
    qt_ref[...] = jnp.dot(wqt_ref[...], h2t_ref[...], preferred_element_type=F32)
    for h in range(PEER_HEADS):
        u = []
        for side in range(2):
            r0 = (2 * h + side) * PEER_HALF
            qc = qt_ref[r0:r0 + PEER_HALF, :].astype(BF16)
            sc = jnp.dot(keys_ref[h, side].astype(BF16), qc, preferred_element_type=F32)
            us = jnp.exp(sc - jnp.max(sc, axis=0, keepdims=True))
            vals = _top16_rows(us)
            for k in range(PEER_TOPK):
                top_ref[side, k:k + 1, :] = vals[k]
            u.append(us)
        cand_ref[...] = jnp.full(cand_ref.shape, -1.0, F32)
        off = 0
        for a, nb in enumerate(_CAND_ROWS):
            cand_ref[off:off + nb, :] = top_ref[0, a:a + 1, :] * top_ref[1, 0:nb, :]
            off += nb
        cand = cand_ref[...]
        selected = cand >= _top16_rows(cand)[-1]
        inv_z = 0.5 / jnp.sum(jnp.where(selected, cand, 0.0), axis=0, keepdims=True)
        off = 0
        for a, nb in enumerate(_CAND_ROWS):
            cand_ref[off:off + nb, :] = (top_ref[0, a:a + 1, :] * inv_z) * top_ref[1, 0:nb, :]
            off += nb
        theta = jnp.min(jnp.where(selected, cand_ref[...], jnp.inf), axis=0, keepdims=True)
        e1 = u[0] * inv_z
        for blk in range(PEER_N_KEYS // 8):
            e1_ref[h, blk] = e1[blk * 8:(blk + 1) * 8, :]
        e2_ref[h] = u[1]
        theta_ref[h:h + 1, :] = theta


def _peer_query(h2t, wq_t, sub_keys):
    d, s = h2t.shape
    tt = 512
    nq = wq_t.shape[0]
    side_blocked = jax.ShapeDtypeStruct((PEER_HEADS, PEER_N_KEYS // 8, 8, s), F32)
    side_flat = jax.ShapeDtypeStruct((PEER_HEADS, PEER_N_KEYS, s), F32)
    blocked_spec = pl.BlockSpec((PEER_HEADS, PEER_N_KEYS // 8, 8, tt), lambda t: (0, 0, 0, t))
    flat_spec = pl.BlockSpec((PEER_HEADS, PEER_N_KEYS, tt), lambda t: (0, 0, t))
    return pl.pallas_call(
        _peer_query_kernel,
        out_shape=(side_blocked, side_flat, jax.ShapeDtypeStruct((PEER_HEADS, s), F32)),
        grid=(s // tt,),
        in_specs=[pl.BlockSpec((d, tt), lambda t: (0, t)),
                  pl.BlockSpec((nq, d), lambda t: (0, 0)),
                  pl.BlockSpec(sub_keys.shape, lambda t: (0, 0, 0, 0))],
        out_specs=(blocked_spec, flat_spec, pl.BlockSpec((PEER_HEADS, tt), lambda t: (0, t))),
        scratch_shapes=[pltpu.VMEM((nq, tt), F32),
                        pltpu.VMEM((2, PEER_TOPK, tt), F32),
                        pltpu.VMEM((_CAND_PAD, tt), F32)],
        compiler_params=_params(("parallel",)),
        name="peer_query",
    )(h2t, wq_t, sub_keys)


PEER_LANE_CHUNK = 128
PEER_HALF_KEYS = 4
PEER_HALF_TILE = PEER_HALF_KEYS * PEER_N_KEYS
PEER_TILE = 2 * PEER_HALF_TILE
PEER_TOKENS = 512


def _gate_act(hid_ref, act_ref, e1_ref, e2_ref, theta_ref, half):
    tt = hid_ref.shape[2]
    for ii in range(PEER_HALF_KEYS):
        rows = slice(ii * PEER_N_KEYS, (ii + 1) * PEER_N_KEYS)
        key = half * PEER_HALF_KEYS + ii
        for c in range(tt // PEER_LANE_CHUNK):
            lanes = slice(c * PEER_LANE_CHUNK, (c + 1) * PEER_LANE_CHUNK)
            w = None
            for h in range(PEER_HEADS):
                p = e1_ref[h, 0, key:key + 1, lanes] * e2_ref[h, :, lanes]
                kept = jnp.where(p >= theta_ref[h:h + 1, lanes], p, 0.0)
                w = kept if w is None else w + kept
            act_ref[half, rows, lanes] = (hid_ref[half, rows, lanes] * w).astype(act_ref.dtype)


def _project(down, h2t):
    x = jnp.dot(down, h2t, preferred_element_type=F32)
    return (x * (1.0 + lax.erf(x * INV_SQRT2))).reshape(2, PEER_HALF_TILE, h2t.shape[1])


def _peer_project_kernel(down_ref, h2t_ref, hid_ref):
    hid_ref[...] = _project(down_ref[...], h2t_ref[...])


def _peer_project_first(h2t, down):
    d = h2t.shape[0]
    return pl.pallas_call(
        _peer_project_kernel,
        out_shape=jax.ShapeDtypeStruct((2, PEER_HALF_TILE, PEER_TOKENS), F32),
        grid=(1,),
        in_specs=[pl.BlockSpec((PEER_TILE, d), lambda i: (0, 0)),
                  pl.BlockSpec((d, PEER_TOKENS), lambda i: (0, 0))],
        out_specs=pl.BlockSpec((2, PEER_HALF_TILE, PEER_TOKENS), lambda i: (0, 0, 0)),
        compiler_params=_params(("arbitrary",)),
        name="peer_project_first",
    )(down, h2t)


def _peer_dense_kernel(h2t_ref, down_ref, up_ref, e1_ref, e2_ref, theta_ref, hid0_ref,
                       x1_ref, gate_ref, g_ref, out_ref, hid_ref, act_ref):
    t = pl.program_id(0)
    e = pl.program_id(1)

    @pl.when((t == 0) & (e == 0))
    def _():
        hid_ref[...] = hid0_ref[...]

    @pl.when(e == 0)
    def _():
        out_ref[...] = jnp.zeros(out_ref.shape, F32)

    for half in range(2):
        _gate_act(hid_ref, act_ref, e1_ref, e2_ref, theta_ref, half)
    act = act_ref[...].reshape(PEER_TILE, act_ref.shape[2])
    out_ref[...] += lax.dot_general(act, up_ref[...], (((0,), (0,)), ((), ())), preferred_element_type=F32)
    hid_ref[...] = _project(down_ref[...], h2t_ref[...])

    @pl.when(e == pl.num_programs(1) - 1)
    def _():
        y = out_ref[...]
        out_ref[...] = x1_ref[...] + gate_ref[...] * (_rms(y) * g_ref[...])


def _peer_dense(h2t, down, up, e1, e2, theta, x1, gate2, g3):
    d, s = h2t.shape
    tt = PEER_TOKENS
    n_e = down.shape[0] // PEER_TILE
    n_t = s // tt
    vec = pl.BlockSpec((1, d), lambda t, e: (0, 0))
    next_e = lambda e: (e + 1) % n_e
    next_t = lambda t, e: jnp.minimum(t + (e + 1) // n_e, n_t - 1)
    return pl.pallas_call(
        _peer_dense_kernel,
        out_shape=jax.ShapeDtypeStruct((s, d), F32),
        grid=(n_t, n_e),
        in_specs=[pl.BlockSpec((d, tt), lambda t, e: (0, next_t(t, e))),
                  pl.BlockSpec((PEER_TILE, d), lambda t, e: (next_e(e), 0)),
                  pl.BlockSpec((PEER_TILE, d), lambda t, e: (e, 0)),
                  pl.BlockSpec((PEER_HEADS, 1, 2 * PEER_HALF_KEYS, tt), lambda t, e: (0, e, 0, t)),
                  pl.BlockSpec((PEER_HEADS, PEER_N_KEYS, tt), lambda t, e: (0, 0, t)),
                  pl.BlockSpec((PEER_HEADS, tt), lambda t, e: (0, t)),
                  pl.BlockSpec((2, PEER_HALF_TILE, tt), lambda t, e: (0, 0, 0)),
                  pl.BlockSpec((tt, d), lambda t, e: (t, 0)),
                  vec, vec],
        out_specs=pl.BlockSpec((tt, d), lambda t, e: (t, 0)),
        scratch_shapes=[pltpu.VMEM((2, PEER_HALF_TILE, tt), F32),
                        pltpu.VMEM((2, PEER_HALF_TILE, tt), BF16)],
        compiler_params=_params(("arbitrary", "arbitrary")),
        name="peer_dense",
    )(h2t, down, up, e1, e2, theta, _peer_project_first(h2t, down), x1, gate2, g3)


def kernel(x, c, positions, ada_w, ada_b, norm_gains, w_in, w_attn_o, conv_w, conv_b, conv_ln_g,
           conv_ln_b, w_conv_o, w_out, peer_w_q, peer_sub_keys, peer_down, peer_up):
    batch, seq, d = x.shape
    depth = ada_w.shape[0]
    assert batch == 1, "kernels are written for a single sequence"
    assert all(window // dilation == ATTN_BLOCK for window, dilation in DILATED_GROUPS)
    cc = conv_w.shape[-1]
    xs = x[0]
    inv_freq = ROPE_THETA ** (-jnp.arange(0, HEAD_DIM, 2, dtype=F32) / HEAD_DIM)
    invf_row = jnp.concatenate([inv_freq, inv_freq])[None, :]
    cos, sin = _rope_tables(positions[0].astype(F32)[:, None], invf_row)
    row = lambda v: v[None, :]
    for l in range(depth):
        mod = _modulation(c[0][:, None], ada_w[l], ada_b[l][None, :])
        shift1, scale1, gate1, shift2, scale2, gate2 = [mod[:, k * d:(k + 1) * d] for k in range(6)]
        g = norm_gains[l]
        w_in_b = w_in[l]

        h = _norm_mod(xs, row(g[0]), scale1, shift1)
        u = _glu_proj(h, w_in_b, cc)
        gates, down_b = _gates_proj(h, w_in_b, 3 * ATTN_WIDTH + 2 * cc, 2 * d, peer_down[l])
        outs, lses = [], []
        for gi, (_, dilation) in enumerate(DILATED_GROUPS):
            o, lse = _dilated_attention(_qkv_proj(h, w_in_b, cos, sin, gi, dilation))
            outs.append(o)
            lses.append(lse)
        merged, up_b = _mix(outs, lses, u, gates, w_attn_o[l].astype(BF16), conv_w[l][:, 0, :],
                            row(conv_b[l]), row(conv_ln_g[l]), row(conv_ln_b[l]), w_conv_o[l].astype(BF16),
                            peer_up[l])
        x1, h2t = _out_proj(merged, w_out[l].astype(BF16), xs, gate1, row(g[1]), row(g[2]),
                            scale2, shift2)

        e1, e2, theta = _peer_query(h2t, peer_w_q[l].T.astype(BF16), peer_sub_keys[l])
        xs = _peer_dense(h2t, down_b, up_b, e1, e2, theta, x1, gate2, row(g[3]))
    return xs[None]
```

```python
import math

import jax
import jax.numpy as jnp
from jax import lax
from jax.experimental import pallas as pl
from jax.experimental.pallas import tpu as pltpu

F32 = jnp.float32
BF16 = jnp.bfloat16

HEAD_DIM = 128
HEADS_PER_GROUP = 4
DILATED_GROUPS = ((128, 1), (512, 4), (2048, 16))
N_GROUPS = len(DILATED_GROUPS)
GROUP_WIDTH = HEADS_PER_GROUP * HEAD_DIM
ATTN_WIDTH = N_GROUPS * GROUP_WIDTH
ATTN_BLOCK = 128
ROPE_THETA = 10000.0
CONV_WIDTH = 31
CONV_HALO = 32
PEER_HEADS = 8
PEER_N_KEYS = 128
PEER_HALF = 128
PEER_TOPK = 16
NORM_EPS = 1e-6
MASK_VALUE = -1e30
NEG_INF = float("-inf")
INV_SQRT2 = 1.0 / math.sqrt(2.0)

SUBLANES = 8
COL = 512
PROJ_ROWS = 2048
VMEM_LIMIT = 56 * 1024 * 1024

_CAND_ROWS = [PEER_TOPK // (a + 1) for a in range(PEER_TOPK)]
_N_CAND = sum(_CAND_ROWS)
_CAND_PAD = -(-_N_CAND // 8) * 8


def _params(sem):
    return pltpu.CompilerParams(dimension_semantics=sem, vmem_limit_bytes=VMEM_LIMIT)


def _rms(x):
    return x * lax.rsqrt(jnp.mean(x * x, axis=-1, keepdims=True) + NORM_EPS)


def _sigmoid(x):
    return 0.5 * jnp.tanh(0.5 * x) + 0.5


def _mod_kernel(c_ref, w_ref, b_ref, o_ref):
    rows = 256

    def body(k, acc):
        r = pl.multiple_of(k * rows, rows)
        c = c_ref[pl.ds(r, rows), :]
        sc = c * jax.nn.sigmoid(c)
        return acc + jnp.sum(w_ref[pl.ds(r, rows), :] * sc, axis=0, keepdims=True)

    acc = lax.fori_loop(0, w_ref.shape[0] // rows, body, jnp.zeros(o_ref.shape, F32))
    o_ref[...] = acc + b_ref[...]


def _modulation(c_col, w, b):
    d, n = w.shape
    tn = 1024
    return pl.pallas_call(
        _mod_kernel,
        out_shape=jax.ShapeDtypeStruct((1, n), F32),
        grid=(n // tn,),
        in_specs=[pl.BlockSpec((d, 1), lambda j: (0, 0)),
                  pl.BlockSpec((d, tn), lambda j: (0, j)),
                  pl.BlockSpec((1, tn), lambda j: (0, j))],
        out_specs=pl.BlockSpec((1, tn), lambda j: (0, j)),
        compiler_params=_params(("parallel",)),
        name="mod",
    )(c_col, w, b)


def _rope_kernel(pos_ref, invf_ref, cos_ref, sin_ref):
    ang = pos_ref[...] * invf_ref[...]
    cos_ref[...] = jnp.cos(ang)
    s = jnp.sin(ang)
    lane = lax.broadcasted_iota(jnp.int32, s.shape, 1)
    sin_ref[...] = jnp.where(lane < HEAD_DIM // 2, -s, s)


def _rope_tables(pos_col, invf_row):
    s = pos_col.shape[0]
    ts = 2048
    return pl.pallas_call(
        _rope_kernel,
        out_shape=(jax.ShapeDtypeStruct((s, HEAD_DIM), F32),) * 2,
        grid=(s // ts,),
        in_specs=[pl.BlockSpec((ts, 1), lambda i: (i, 0)),
                  pl.BlockSpec((1, HEAD_DIM), lambda i: (0, 0))],
        out_specs=(pl.BlockSpec((ts, HEAD_DIM), lambda i: (i, 0)),) * 2,
        compiler_params=_params(("parallel",)),
        name="rope",
    )(pos_col, invf_row)


def _norm_mod_kernel(x_ref, g_ref, scale_ref, shift_ref, o_ref):
    y = _rms(x_ref[...]) * g_ref[...]
    o_ref[...] = (y * (1.0 + scale_ref[...]) + shift_ref[...]).astype(o_ref.dtype)


def _norm_mod(x, g, scale, shift):
    s, d = x.shape
    tm = 1024
    vec = pl.BlockSpec((1, d), lambda i: (0, 0))
    return pl.pallas_call(
        _norm_mod_kernel,
        out_shape=jax.ShapeDtypeStruct((s, d), BF16),
        grid=(s // tm,),
        in_specs=[pl.BlockSpec((tm, d), lambda i: (i, 0)), vec, vec, vec],
        out_specs=pl.BlockSpec((tm, d), lambda i: (i, 0)),
        compiler_params=_params(("parallel",)),
        name="norm_mod",
    )(x, g, scale, shift)


def _qkv_kernel(h_ref, w_ref, cos_ref, sin_ref, o_ref, stage_ref):
    acc = jnp.dot(h_ref[...], w_ref[...].astype(BF16), preferred_element_type=F32)
    part = pl.program_id(1)

    heads = [slice(hh * HEAD_DIM, (hh + 1) * HEAD_DIM) for hh in range(HEADS_PER_GROUP)]

    @pl.when(part < 2)
    def _():
        cos = cos_ref[...]
        sin = sin_ref[...]
        for hh, cols in enumerate(heads):
            t = acc[:, cols]
            stage_ref[hh] = t * cos + pltpu.roll(t, HEAD_DIM // 2, axis=1) * sin

    @pl.when(part == 2)
    def _():
        for hh, cols in enumerate(heads):
            stage_ref[hh] = acc[:, cols]

    dilation, rows, _ = o_ref.shape
    for r in range(dilation):
        for hh, cols in enumerate(heads):
            o_ref[r, :, cols] = stage_ref[hh, pl.ds(r, rows, stride=dilation), :].astype(o_ref.dtype)


def _qkv_proj(h, w_in, cos, sin, group, dilation):
    s, d = h.shape
    tm = PROJ_ROWS
    return pl.pallas_call(
        _qkv_kernel,
        out_shape=jax.ShapeDtypeStruct((dilation, s // dilation, 3 * GROUP_WIDTH), BF16),
        grid=(s // tm, 3),
        in_specs=[pl.BlockSpec((tm, d), lambda i, j: (i, 0)),
                  pl.BlockSpec((d, COL), lambda i, j: (0, j * N_GROUPS + group)),
                  pl.BlockSpec((tm, HEAD_DIM), lambda i, j: (i, 0)),
                  pl.BlockSpec((tm, HEAD_DIM), lambda i, j: (i, 0))],
        out_specs=pl.BlockSpec((dilation, tm // dilation, COL), lambda i, j: (0, i, j)),
        scratch_shapes=[pltpu.VMEM((HEADS_PER_GROUP, tm, HEAD_DIM), F32)],
        compiler_params=_params(("parallel", "arbitrary")),
        name=f"qkv_d{dilation}",
    )(h, w_in, cos, sin)


def _glu_kernel(h_ref, wa_ref, wb_ref, o_ref):
    h = h_ref[...]
    a = jnp.dot(h, wa_ref[...].astype(BF16), preferred_element_type=F32)
    b = jnp.dot(h, wb_ref[...].astype(BF16), preferred_element_type=F32)
    o_ref[...] = a * _sigmoid(b)


def _glu_proj(h, w_in, conv_channels):
    s, d = h.shape
    tm = PROJ_ROWS
    a0 = 3 * ATTN_WIDTH // COL
    nb = conv_channels // COL
    return pl.pallas_call(
        _glu_kernel,
        out_shape=jax.ShapeDtypeStruct((s, conv_channels), F32),
        grid=(s // tm, nb),
        in_specs=[pl.BlockSpec((tm, d), lambda i, j: (i, 0)),
                  pl.BlockSpec((d, COL), lambda i, j: (0, a0 + j)),
                  pl.BlockSpec((d, COL), lambda i, j: (0, a0 + nb + j))],
        out_specs=pl.BlockSpec((tm, COL), lambda i, j: (i, j)),
        compiler_params=_params(("parallel", "arbitrary")),
        name="glu",
    )(h, w_in, w_in)


def _gates_kernel(h_ref, w_ref, side_ref, o_ref, side_out_ref):
    o_ref[...] = _sigmoid(jnp.dot(h_ref[...], w_ref[...].astype(BF16), preferred_element_type=F32))
    side_out_ref[...] = side_ref[...].astype(side_out_ref.dtype)


def _gates_proj(h, w_in, col0, width, side):
    s, d = h.shape
    tm = PROJ_ROWS
    c0 = col0 // COL
    n_j = width // COL
    slab = side.shape[0] // ((s // tm) * n_j)
    assert slab * (s // tm) * n_j == side.shape[0]
    return pl.pallas_call(
        _gates_kernel,
        out_shape=(jax.ShapeDtypeStruct((s, width), F32), jax.ShapeDtypeStruct(side.shape, BF16)),
        grid=(s // tm, n_j),
        in_specs=[pl.BlockSpec((tm, d), lambda i, j: (i, 0)),
                  pl.BlockSpec((d, COL), lambda i, j: (0, c0 + j)),
                  pl.BlockSpec((slab, side.shape[1]), lambda i, j: (i * n_j + j, 0))],
        out_specs=(pl.BlockSpec((tm, COL), lambda i, j: (i, j)),
                   pl.BlockSpec((slab, side.shape[1]), lambda i, j: (i * n_j + j, 0))),
        compiler_params=_params(("parallel", "arbitrary")),
        name="gates",
    )(h, w_in, side)


ATTN_Q_BLOCKS = 4


def _attn_kernel(q_ref, kc_ref, kp_ref, vc_ref, vp_ref, o_ref, l_ref):
    n = pl.program_id(1)
    qi = lax.broadcasted_iota(jnp.int32, (ATTN_BLOCK, 2 * ATTN_BLOCK), 0)
    kj = lax.broadcasted_iota(jnp.int32, (ATTN_BLOCK, 2 * ATTN_BLOCK), 1)
    band = jnp.abs(kj - qi - ATTN_BLOCK // 2) <= ATTN_BLOCK // 2
    first_lo = jnp.where(n == 0, ATTN_BLOCK, 0)
    band_first = band & (kj >= first_lo)
    scale = HEAD_DIM ** -0.5
    for hh in range(HEADS_PER_GROUP):
        cols = slice(hh * HEAD_DIM, (hh + 1) * HEAD_DIM)
        for b in range(ATTN_Q_BLOCKS):
            rows = slice(b * ATTN_BLOCK, (b + 1) * ATTN_BLOCK)
            q = q_ref[rows, cols]
            if b == 0:
                k = jnp.concatenate([kp_ref[:, cols], kc_ref[0:ATTN_BLOCK, cols]], axis=0)
                v = jnp.concatenate([vp_ref[:, cols], vc_ref[0:ATTN_BLOCK, cols]], axis=0)
                valid = band_first
            else:
                band_rows = slice((b - 1) * ATTN_BLOCK, (b + 1) * ATTN_BLOCK)
                k = kc_ref[band_rows, cols]
                v = vc_ref[band_rows, cols]
                valid = band
            s = lax.dot_general(q, k, (((1,), (1,)), ((), ())), preferred_element_type=F32) * scale
            s = jnp.where(valid, s, MASK_VALUE)
            m = jnp.max(s, axis=-1, keepdims=True)
            p = jnp.exp(s - m)
            den = jnp.sum(p, axis=-1, keepdims=True)
            o = jnp.dot(p.astype(BF16), v, preferred_element_type=F32)
            o_ref[rows, cols] = o * (1.0 / den)
            l_ref[rows, cols] = jnp.broadcast_to(m + jnp.log(den), (ATTN_BLOCK, HEAD_DIM))


def _dilated_attention(qkv):
    dilation, sub_len, _ = qkv.shape
    qt = ATTN_Q_BLOCKS * ATTN_BLOCK

    def cur(part):
        return pl.BlockSpec((None, qt, COL), lambda r, n: (r, n, part))

    def prev(part):
        return pl.BlockSpec((None, ATTN_BLOCK, COL),
                            lambda r, n: (r, jnp.maximum(n * ATTN_Q_BLOCKS - 1, 0), part))

    out_sds = jax.ShapeDtypeStruct((dilation, sub_len, GROUP_WIDTH), F32)
    out_spec = pl.BlockSpec((None, qt, GROUP_WIDTH), lambda r, n: (r, n, 0))
    return pl.pallas_call(
        _attn_kernel,
        out_shape=(out_sds, out_sds),
        grid=(dilation, sub_len // qt),
        in_specs=[cur(0), cur(1), prev(1), cur(2), prev(2)],
        out_specs=(out_spec, out_spec),
        compiler_params=_params(("parallel", "arbitrary")),
        name=f"attn_d{dilation}",
    )(qkv, qkv, qkv, qkv, qkv)


def _mix_kernel(o0_ref, o1_ref, o2_ref, l0_ref, l1_ref, l2_ref, u_ref, uh_ref, ga_ref, gc_ref,
                wao_ref, cw_ref, cb_ref, lg_ref, lb_ref, wco_ref, side_ref,
                out_ref, side_out_ref, ucat_ref, nat_ref, shift_ref):
    i = pl.program_id(0)
    tm = u_ref.shape[0]
    side_out_ref[...] = side_ref[...].astype(side_out_ref.dtype)

    def natural(ref, slot):
        dilation, rows, _ = ref.shape
        if dilation == 1:
            return ref[0]
        for hh in range(HEADS_PER_GROUP):
            for r in range(dilation):
                nat_ref[slot, hh, pl.ds(r, rows, stride=dilation), :] = ref[r, :, hh * HEAD_DIM:(hh + 1) * HEAD_DIM]
        return jnp.concatenate([nat_ref[slot, hh] for hh in range(HEADS_PER_GROUP)], axis=1)

    l0, l1, l2 = natural(l0_ref, 0), natural(l1_ref, 1), natural(l2_ref, 2)
    m = jnp.maximum(jnp.maximum(l0, l1), l2)
    e0, e1, e2 = jnp.exp(l0 - m), jnp.exp(l1 - m), jnp.exp(l2 - m)
    attn = (e0 * natural(o0_ref, 3) + e1 * natural(o1_ref, 4) + e2 * natural(o2_ref, 5)) / (e0 + e1 + e2)
    attn_o = jnp.dot(attn.astype(BF16), wao_ref[...], preferred_element_type=F32)
    ucat_ref[0:CONV_HALO, :] = jnp.where(i > 0, uh_ref[...], 0.0)
    ucat_ref[CONV_HALO:, :] = u_ref[...]
    off = CONV_HALO - (CONV_WIDTH - 1)
    span = tm + CONV_HALO - SUBLANES
    for s in range(1, SUBLANES):
        shift_ref[s - 1] = ucat_ref[s:s + span, :]
    conv = jnp.zeros(u_ref.shape, F32) + cb_ref[...]
    for w in range(CONV_WIDTH):
        q, s = divmod(off + w, SUBLANES)
        src = ucat_ref if s == 0 else shift_ref.at[s - 1]
        conv = conv + src[q * SUBLANES:q * SUBLANES + tm, :] * cw_ref[w:w + 1, :]
    mu = jnp.mean(conv, axis=-1, keepdims=True)
    cen = conv - mu
    var = jnp.mean(cen * cen, axis=-1, keepdims=True)
    y = cen * lax.rsqrt(var + NORM_EPS) * lg_ref[...] + lb_ref[...]
    y = y * _sigmoid(y)
    conv_o = jnp.dot(y.astype(BF16), wco_ref[...], preferred_element_type=F32)
    out_ref[...] = (ga_ref[...] * attn_o + gc_ref[...] * conv_o).astype(out_ref.dtype)


def _mix(outs, lses, u, gates, w_attn_o, conv_w, conv_b, ln_g, ln_b, w_conv_o, side):
    s, cc = u.shape
    d = w_attn_o.shape[1]
    tm = 256
    row = lambda i: (i, 0)
    const = lambda i: (0, 0)
    cvec = pl.BlockSpec((1, cc), const)
    slab = side.shape[0] // (s // tm)
    assert slab * (s // tm) == side.shape[0]
    side_spec = pl.BlockSpec((slab, side.shape[1]), row)

    def grp(arr):
        dilation = arr.shape[0]
        return pl.BlockSpec((dilation, tm // dilation, GROUP_WIDTH), lambda i: (0, i, 0))

    return pl.pallas_call(
        _mix_kernel,
        out_shape=(jax.ShapeDtypeStruct((s, d), BF16), jax.ShapeDtypeStruct(side.shape, BF16)),
        grid=(s // tm,),
        in_specs=[grp(a) for a in (*outs, *lses)] + [
            pl.BlockSpec((tm, cc), row),
            pl.BlockSpec((CONV_HALO, cc), lambda i: (jnp.maximum(i * (tm // CONV_HALO) - 1, 0), 0)),
            pl.BlockSpec((tm, d), lambda i: (i, 0)),
            pl.BlockSpec((tm, d), lambda i: (i, 1)),
            pl.BlockSpec((GROUP_WIDTH, d), const),
            pl.BlockSpec((CONV_WIDTH, cc), const),
            cvec, cvec, cvec,
            pl.BlockSpec((cc, d), const),
            side_spec],
        out_specs=(pl.BlockSpec((tm, d), row), side_spec),
        scratch_shapes=[pltpu.VMEM((tm + CONV_HALO, cc), F32),
                        pltpu.VMEM((2 * N_GROUPS, HEADS_PER_GROUP, tm, HEAD_DIM), F32),
                        pltpu.VMEM((SUBLANES - 1, tm + CONV_HALO - SUBLANES, cc), F32)],
        compiler_params=_params(("parallel",)),
        name="mix",
    )(*outs, *lses, u, u, gates, gates, w_attn_o, conv_w, conv_b, ln_g, ln_b, w_conv_o, side)


def _out_proj_kernel(m_ref, w_ref, x_ref, gate_ref, g1_ref, g2_ref, scale_ref, shift_ref,
                     x1_ref, h2t_ref):
    y = jnp.dot(m_ref[...], w_ref[...], preferred_element_type=F32)
    x1 = x_ref[...] + gate_ref[...] * (_rms(y) * g1_ref[...])
    x1_ref[...] = x1
    h2 = (_rms(x1) * g2_ref[...]) * (1.0 + scale_ref[...]) + shift_ref[...]
    h2t_ref[...] = h2.T.astype(h2t_ref.dtype)


def _out_proj(merged, w_out, x, gate1, g1, g2, scale2, shift2):
    s, d = x.shape
    tm = 512
    row = lambda i: (i, 0)
    vec = pl.BlockSpec((1, d), lambda i: (0, 0))
    return pl.pallas_call(
        _out_proj_kernel,
        out_shape=(jax.ShapeDtypeStruct((s, d), F32), jax.ShapeDtypeStruct((d, s), BF16)),
        grid=(s // tm,),
        in_specs=[pl.BlockSpec((tm, d), row), pl.BlockSpec((d, d), lambda i: (0, 0)),
                  pl.BlockSpec((tm, d), row), vec, vec, vec, vec, vec],
        out_specs=(pl.BlockSpec((tm, d), row), pl.BlockSpec((d, tm), lambda i: (0, i))),
        compiler_params=_params(("parallel",)),
        name="out_proj",
    )(merged, w_out, x, gate1, g1, g2, scale2, shift2)


def _odd_even_merge_sort(lo, hi):
    def merge(lo, hi, r):
        step = 2 * r
        if step < hi - lo:
            yield from merge(lo, hi, step)
            yield from merge(lo + r, hi, step)
            for i in range(lo + r, hi - r, step):
                yield (i, i + r)
        else:
            yield (lo, lo + r)

    if hi > lo:
        mid = lo + (hi - lo) // 2
        yield from _odd_even_merge_sort(lo, mid)
        yield from _odd_even_merge_sort(mid + 1, hi)
        yield from merge(lo, hi, 1)


def _bitonic_merge(n):
    half = n // 2
    while half >= 1:
        for i in range(n):
            if (i // half) % 2 == 0:
                yield (i, i + half)
        half //= 2


_SORT_PAIRS = {n: tuple(_odd_even_merge_sort(0, n - 1)) for n in (PEER_TOPK // 2, PEER_TOPK)}
_MERGE_PAIRS = tuple(_bitonic_merge(PEER_TOPK))


def _compare_exchange(v, pairs):
    for i, j in pairs:
        v[i], v[j] = jnp.maximum(v[i], v[j]), jnp.minimum(v[i], v[j])


def _top16_rows(x):
    blocks = x.shape[0] // SUBLANES
    assert x.shape[0] == blocks * SUBLANES and blocks <= PEER_TOPK
    n = PEER_TOPK // 2 if blocks <= PEER_TOPK // 2 else PEER_TOPK
    v = [x[k * SUBLANES:(k + 1) * SUBLANES, :] for k in range(blocks)]
    v += [jnp.full_like(v[0], NEG_INF)] * (n - blocks)
    _compare_exchange(v, _SORT_PAIRS[n])
    shift = SUBLANES // 2
    while shift >= 1:
        partner = [pltpu.roll(blk, shift, axis=0) for blk in v]
        if len(v) < PEER_TOPK:
            v = v + partner[::-1]
        else:
            v = [jnp.maximum(v[k], partner[PEER_TOPK - 1 - k]) for k in range(PEER_TOPK)]
        _compare_exchange(v, _MERGE_PAIRS)
        shift //= 2
    return [blk[0:1, :] for blk in v]


def _peer_query_kernel(h2t_ref, wqt_ref, keys_ref, e1_ref, e2_ref, theta_ref,
                       qt_ref, top_ref, cand_ref):
    qt_ref[...] = jnp.dot(wqt_ref[...], h2t_ref[...], preferred_element_type=F32)
    for h in range(PEER_HEADS):
        u = []
        for side in range(2):
            r0 = (2 * h + side) * PEER_HALF
            qc = qt_ref[r0:r0 + PEER_HALF, :].astype(BF16)
            sc = jnp.dot(keys_ref[h, side].astype(BF16), qc, preferred_element_type=F32)
            us = jnp.exp(sc - jnp.max(sc, axis=0, keepdims=True))
            vals = _top16_rows(us)
            for k in range(PEER_TOPK):
                top_ref[side, k:k + 1, :] = vals[k]
            u.append(us)
        cand_ref[...] = jnp.full(cand_ref.shape, -1.0, F32)
        off = 0
        for a, nb in enumerate(_CAND_ROWS):
            cand_ref[off:off + nb, :] = top_ref[0, a:a + 1, :] * top_ref[1, 0:nb, :]
            off += nb
        cand = cand_ref[...]
        selected = cand >= _top16_rows(cand)[-1]
        inv_z = 0.5 / jnp.sum(jnp.where(selected, cand, 0.0), axis=0, keepdims=True)
        off = 0
        for a, nb in enumerate(_CAND_ROWS):
            cand_ref[off:off + nb, :] = (top_ref[0, a:a + 1, :] * inv_z) * top_ref[1, 0:nb, :]
            off += nb
        theta = jnp.min(jnp.where(selected, cand_ref[...], jnp.inf), axis=0, keepdims=True)
        e1 = u[0] * inv_z
        for blk in range(PEER_N_KEYS // 8):
            e1_ref[h, blk] = e1[blk * 8:(blk + 1) * 8, :]
        e2_ref[h] = u[1]
        theta_ref[h:h + 1, :] = theta


def _peer_query(h2t, wq_t, sub_keys):
    d, s = h2t.shape
    tt = 512
    nq = wq_t.shape[0]
    side_blocked = jax.ShapeDtypeStruct((PEER_HEADS, PEER_N_KEYS // 8, 8, s), F32)
    side_flat = jax.ShapeDtypeStruct((PEER_HEADS, PEER_N_KEYS, s), F32)
    blocked_spec = pl.BlockSpec((PEER_HEADS, PEER_N_KEYS // 8, 8, tt), lambda t: (0, 0, 0, t))
    flat_spec = pl.BlockSpec((PEER_HEADS, PEER_N_KEYS, tt), lambda t: (0, 0, t))
    return pl.pallas_call(
        _peer_query_kernel,
        out_shape=(side_blocked, side_flat, jax.ShapeDtypeStruct((PEER_HEADS, s), F32)),
        grid=(s // tt,),
        in_specs=[pl.BlockSpec((d, tt), lambda t: (0, t)),
                  pl.BlockSpec((nq, d), lambda t: (0, 0)),
                  pl.BlockSpec(sub_keys.shape, lambda t: (0, 0, 0, 0))],
        out_specs=(blocked_spec, flat_spec, pl.BlockSpec((PEER_HEADS, tt), lambda t: (0, t))),
        scratch_shapes=[pltpu.VMEM((nq, tt), F32),
                        pltpu.VMEM((2, PEER_TOPK, tt), F32),
                        pltpu.VMEM((_CAND_PAD, tt), F32)],
        compiler_params=_params(("parallel",)),
        name="peer_query",
    )(h2t, wq_t, sub_keys)


PEER_LANE_CHUNK = 128
PEER_HALF_KEYS = 4
PEER_HALF_TILE = PEER_HALF_KEYS * PEER_N_KEYS
PEER_TILE = 2 * PEER_HALF_TILE
PEER_TOKENS = 512


def _gate_act(hid_ref, act_ref, e1_ref, e2_ref, theta_ref, half):
    tt = hid_ref.shape[2]
    for ii in range(PEER_HALF_KEYS):
        rows = slice(ii * PEER_N_KEYS, (ii + 1) * PEER_N_KEYS)
        key = half * PEER_HALF_KEYS + ii
        for c in range(tt // PEER_LANE_CHUNK):
            lanes = slice(c * PEER_LANE_CHUNK, (c + 1) * PEER_LANE_CHUNK)
            w = None
            for h in range(PEER_HEADS):
                p = e1_ref[h, 0, key:key + 1, lanes] * e2_ref[h, :, lanes]
                kept = jnp.where(p >= theta_ref[h:h + 1, lanes], p, 0.0)
                w = kept if w is None else w + kept
            act_ref[half, rows, lanes] = (hid_ref[half, rows, lanes] * w).astype(act_ref.dtype)


def _project(down, h2t):
    x = jnp.dot(down, h2t, preferred_element_type=F32)
    return (x * (1.0 + lax.erf(x * INV_SQRT2))).reshape(2, PEER_HALF_TILE, h2t.shape[1])


def _peer_project_kernel(down_ref, h2t_ref, hid_ref):
    hid_ref[...] = _project(down_ref[...], h2t_ref[...])


def _peer_project_first(h2t, down):
    d = h2t.shape[0]
    return pl.pallas_call(
        _peer_project_kernel,
        out_shape=jax.ShapeDtypeStruct((2, PEER_HALF_TILE, PEER_TOKENS), F32),
        grid=(1,),
        in_specs=[pl.BlockSpec((PEER_TILE, d), lambda i: (0, 0)),
                  pl.BlockSpec((d, PEER_TOKENS), lambda i: (0, 0))],
        out_specs=pl.BlockSpec((2, PEER_HALF_TILE, PEER_TOKENS), lambda i: (0, 0, 0)),
        compiler_params=_params(("arbitrary",)),
        name="peer_project_first",
    )(down, h2t)


def _peer_dense_kernel(h2t_ref, down_ref, up_ref, e1_ref, e2_ref, theta_ref, hid0_ref,
                       x1_ref, gate_ref, g_ref, out_ref, hid_ref, act_ref):
    t = pl.program_id(0)
    e = pl.program_id(1)

    @pl.when((t == 0) & (e == 0))
    def _():
        hid_ref[...] = hid0_ref[...]

    @pl.when(e == 0)
    def _():
        out_ref[...] = jnp.zeros(out_ref.shape, F32)

    for half in range(2):
        _gate_act(hid_ref, act_ref, e1_ref, e2_ref, theta_ref, half)
    act = act_ref[...].reshape(PEER_TILE, act_ref.shape[2])
    out_ref[...] += lax.dot_general(act, up_ref[...], (((0,), (0,)), ((), ())), preferred_element_type=F32)
    hid_ref[...] = _project(down_ref[...], h2t_ref[...])

    @pl.when(e == pl.num_programs(1) - 1)
    def _():
        y = out_ref[...]
        out_ref[...] = x1_ref[...] + gate_ref[...] * (_rms(y) * g_ref[...])


def _peer_dense(h2t, down, up, e1, e2, theta, x1, gate2, g3):
    d, s = h2t.shape
    tt = PEER_TOKENS
    n_e = down.shape[0] // PEER_TILE
    n_t = s // tt
    vec = pl.BlockSpec((1, d), lambda t, e: (0, 0))
    next_e = lambda e: (e + 1) % n_e
    next_t = lambda t, e: jnp.minimum(t + (e + 1) // n_e, n_t - 1)
    return pl.pallas_call(
        _peer_dense_kernel,
        out_shape=jax.ShapeDtypeStruct((s, d), F32),
        grid=(n_t, n_e),
        in_specs=[pl.BlockSpec((d, tt), lambda t, e: (0, next_t(t, e))),
                  pl.BlockSpec((PEER_TILE, d), lambda t, e: (next_e(e), 0)),
                  pl.BlockSpec((PEER_TILE, d), lambda t, e: (e, 0)),
                  pl.BlockSpec((PEER_HEADS, 1, 2 * PEER_HALF_KEYS, tt), lambda t, e: (0, e, 0, t)),
                  pl.BlockSpec((PEER_HEADS, PEER_N_KEYS, tt), lambda t, e: (0, 0, t)),
                  pl.BlockSpec((PEER_HEADS, tt), lambda t, e: (0, t)),
                  pl.BlockSpec((2, PEER_HALF_TILE, tt), lambda t, e: (0, 0, 0)),
                  pl.BlockSpec((tt, d), lambda t, e: (t, 0)),
                  vec, vec],
        out_specs=pl.BlockSpec((tt, d), lambda t, e: (t, 0)),
        scratch_shapes=[pltpu.VMEM((2, PEER_HALF_TILE, tt), F32),
                        pltpu.VMEM((2, PEER_HALF_TILE, tt), BF16)],
        compiler_params=_params(("arbitrary", "arbitrary")),
        name="peer_dense",
    )(h2t, down, up, e1, e2, theta, _peer_project_first(h2t, down), x1, gate2, g3)


def kernel(x, c, positions, ada_w, ada_b, norm_gains, w_in, w_attn_o, conv_w, conv_b, conv_ln_g,
           conv_ln_b, w_conv_o, w_out, peer_w_q, peer_sub_keys, peer_down, peer_up):
    batch, seq, d = x.shape
    depth = ada_w.shape[0]
    assert batch == 1, "kernels are written for a single sequence"
    assert all(window // dilation == ATTN_BLOCK for window, dilation in DILATED_GROUPS)
    cc = conv_w.shape[-1]
    xs = x[0]
    inv_freq = ROPE_THETA ** (-jnp.arange(0, HEAD_DIM, 2, dtype=F32) / HEAD_DIM)
    invf_row = jnp.concatenate([inv_freq, inv_freq])[None, :]
    cos, sin = _rope_tables(positions[0].astype(F32)[:, None], invf_row)
    row = lambda v: v[None, :]
    for l in range(depth):
        mod = _modulation(c[0][:, None], ada_w[l], ada_b[l][None, :])
        shift1, scale1, gate1, shift2, scale2, gate2 = [mod[:, k * d:(k + 1) * d] for k in range(6)]
        g = norm_gains[l]
        w_in_b = w_in[l]

        h = _norm_mod(xs, row(g[0]), scale1, shift1)
        u = _glu_proj(h, w_in_b, cc)
        gates, down_b = _gates_proj(h, w_in_b, 3 * ATTN_WIDTH + 2 * cc, 2 * d, peer_down[l])
        outs, lses = [], []
        for gi, (_, dilation) in enumerate(DILATED_GROUPS):
            o, lse = _dilated_attention(_qkv_proj(h, w_in_b, cos, sin, gi, dilation))
            outs.append(o)
            lses.append(lse)
        merged, up_b = _mix(outs, lses, u, gates, w_attn_o[l].astype(BF16), conv_w[l][:, 0, :],
                            row(conv_b[l]), row(conv_ln_g[l]), row(conv_ln_b[l]), w_conv_o[l].astype(BF16),
                            peer_up[l])
        x1, h2t = _out_proj(merged, w_out[l].astype(BF16), xs, gate1, row(g[1]), row(g[2]),
                            scale2, shift2)

        e1, e2, theta = _peer_query(h2t, peer_w_q[l].T.astype(BF16), peer_sub_keys[l])
        xs = _peer_dense(h2t, down_b, up_b, e1, e2, theta, x1, gate2, row(g[3]))
    return xs[None]
```

```python
import math

import jax
import jax.numpy as jnp
from jax import lax
from jax.experimental import pallas as pl
from jax.experimental.pallas import tpu as pltpu

F32 = jnp.float32
BF16 = jnp.bfloat16

HEAD_DIM = 128
HEADS_PER_GROUP = 4
DILATED_GROUPS = ((128, 1), (512, 4), (2048, 16))
N_GROUPS = len(DILATED_GROUPS)
GROUP_WIDTH = HEADS_PER_GROUP * HEAD_DIM
ATTN_WIDTH = N_GROUPS * GROUP_WIDTH
ATTN_BLOCK = 128
ROPE_THETA = 10000.0
CONV_WIDTH = 31
CONV_HALO = 32
PEER_HEADS = 8
PEER_N_KEYS = 128
PEER_HALF = 128
PEER_TOPK = 16
NORM_EPS = 1e-6
MASK_VALUE = -1e30
NEG_INF = float("-inf")
INV_SQRT2 = 1.0 / math.sqrt(2.0)

SUBLANES = 8
COL = 512
PROJ_ROWS = 2048
VMEM_LIMIT = 56 * 1024 * 1024

_CAND_ROWS = [PEER_TOPK // (a + 1) for a in range(PEER_TOPK)]
_N_CAND = sum(_CAND_ROWS)
_CAND_PAD = -(-_N_CAND // 8) * 8


def _params(sem):
    return pltpu.CompilerParams(dimension_semantics=sem, vmem_limit_bytes=VMEM_LIMIT)


def _rms(x):
    return x * lax.rsqrt(jnp.mean(x * x, axis=-1, keepdims=True) + NORM_EPS)


def _sigmoid(x):
    return 0.5 * jnp.tanh(0.5 * x) + 0.5


def _mod_kernel(c_ref, w_ref, b_ref, o_ref):
    rows = 256

    def body(k, acc):
        r = pl.multiple_of(k * rows, rows)
        c = c_ref[pl.ds(r, rows), :]
        sc = c * jax.nn.sigmoid(c)
        return acc + jnp.sum(w_ref[pl.ds(r, rows), :] * sc, axis=0, keepdims=True)

    acc = lax.fori_loop(0, w_ref.shape[0] // rows, body, jnp.zeros(o_ref.shape, F32))
    o_ref[...] = acc + b_ref[...]


def _modulation(c_col, w, b):
    d, n = w.shape
    tn = 1024
    return pl.pallas_call(
        _mod_kernel,
        out_shape=jax.ShapeDtypeStruct((1, n), F32),
        grid=(n // tn,),
        in_specs=[pl.BlockSpec((d, 1), lambda j: (0, 0)),
                  pl.BlockSpec((d, tn), lambda j: (0, j)),
                  pl.BlockSpec((1, tn), lambda j: (0, j))],
        out_specs=pl.BlockSpec((1, tn), lambda j: (0, j)),
        compiler_params=_params(("parallel",)),
        name="mod",
    )(c_col, w, b)


def _rope_kernel(pos_ref, invf_ref, cos_ref, sin_ref):
    ang = pos_ref[...] * invf_ref[...]
    cos_ref[...] = jnp.cos(ang)
    s = jnp.sin(ang)
    lane = lax.broadcasted_iota(jnp.int32, s.shape, 1)
    sin_ref[...] = jnp.where(lane < HEAD_DIM // 2, -s, s)


def _rope_tables(pos_col, invf_row):
    s = pos_col.shape[0]
    ts = 2048
    return pl.pallas_call(
        _rope_kernel,
        out_shape=(jax.ShapeDtypeStruct((s, HEAD_DIM), F32),) * 2,
        grid=(s // ts,),
        in_specs=[pl.BlockSpec((ts, 1), lambda i: (i, 0)),
                  pl.BlockSpec((1, HEAD_DIM), lambda i: (0, 0))],
        out_specs=(pl.BlockSpec((ts, HEAD_DIM), lambda i: (i, 0)),) * 2,
        compiler_params=_params(("parallel",)),
        name="rope",
    )(pos_col, invf_row)


def _norm_mod_kernel(x_ref, g_ref, scale_ref, shift_ref, o_ref):
    y = _rms(x_ref[...]) * g_ref[...]
    o_ref[...] = (y * (1.0 + scale_ref[...]) + shift_ref[...]).astype(o_ref.dtype)


def _norm_mod(x, g, scale, shift):
    s, d = x.shape
    tm = 1024
    vec = pl.BlockSpec((1, d), lambda i: (0, 0))
    return pl.pallas_call(
        _norm_mod_kernel,
        out_shape=jax.ShapeDtypeStruct((s, d), BF16),
        grid=(s // tm,),
        in_specs=[pl.BlockSpec((tm, d), lambda i: (i, 0)), vec, vec, vec],
        out_specs=pl.BlockSpec((tm, d), lambda i: (i, 0)),
        compiler_params=_params(("parallel",)),
        name="norm_mod",
    )(x, g, scale, shift)


def _qkv_kernel(h_ref, w_ref, cos_ref, sin_ref, o_ref, stage_ref):
    acc = jnp.dot(h_ref[...], w_ref[...].astype(BF16), preferred_element_type=F32)
    part = pl.program_id(1)

    heads = [slice(hh * HEAD_DIM, (hh + 1) * HEAD_DIM) for hh in range(HEADS_PER_GROUP)]

    @pl.when(part < 2)
    def _():
        cos = cos_ref[...]
        sin = sin_ref[...]
        for hh, cols in enumerate(heads):
            t = acc[:, cols]
            stage_ref[hh] = t * cos + pltpu.roll(t, HEAD_DIM // 2, axis=1) * sin

    @pl.when(part == 2)
    def _():
        for hh, cols in enumerate(heads):
            stage_ref[hh] = acc[:, cols]

    dilation, rows, _ = o_ref.shape
    for r in range(dilation):
        for hh, cols in enumerate(heads):
            o_ref[r, :, cols] = stage_ref[hh, pl.ds(r, rows, stride=dilation), :].astype(o_ref.dtype)


def _qkv_proj(h, w_in, cos, sin, group, dilation):
    s, d = h.shape
    tm = PROJ_ROWS
    return pl.pallas_call(
        _qkv_kernel,
        out_shape=jax.ShapeDtypeStruct((dilation, s // dilation, 3 * GROUP_WIDTH), BF16),
        grid=(s // tm, 3),
        in_specs=[pl.BlockSpec((tm, d), lambda i, j: (i, 0)),
                  pl.BlockSpec((d, COL), lambda i, j: (0, j * N_GROUPS + group)),
                  pl.BlockSpec((tm, HEAD_DIM), lambda i, j: (i, 0)),
                  pl.BlockSpec((tm, HEAD_DIM), lambda i, j: (i, 0))],
        out_specs=pl.BlockSpec((dilation, tm // dilation, COL), lambda i, j: (0, i, j)),
        scratch_shapes=[pltpu.VMEM((HEADS_PER_GROUP, tm, HEAD_DIM), F32)],
        compiler_params=_params(("parallel", "arbitrary")),
        name=f"qkv_d{dilation}",
    )(h, w_in, cos, sin)


def _glu_kernel(h_ref, wa_ref, wb_ref, o_ref):
    h = h_ref[...]
    a = jnp.dot(h, wa_ref[...].astype(BF16), preferred_element_type=F32)
    b = jnp.dot(h, wb_ref[...].astype(BF16), preferred_element_type=F32)
    o_ref[...] = a * _sigmoid(b)


def _glu_proj(h, w_in, conv_channels):
    s, d = h.shape
    tm = PROJ_ROWS
    a0 = 3 * ATTN_WIDTH // COL
    nb = conv_channels // COL
    return pl.pallas_call(
        _glu_kernel,
        out_shape=jax.ShapeDtypeStruct((s, conv_channels), F32),
        grid=(s // tm, nb),
        in_specs=[pl.BlockSpec((tm, d), lambda i, j: (i, 0)),
                  pl.BlockSpec((d, COL), lambda i, j: (0, a0 + j)),
                  pl.BlockSpec((d, COL), lambda i, j: (0, a0 + nb + j))],
        out_specs=pl.BlockSpec((tm, COL), lambda i, j: (i, j)),
        compiler_params=_params(("parallel", "arbitrary")),
        name="glu",
    )(h, w_in, w_in)


def _gates_kernel(h_ref, w_ref, side_ref, o_ref, side_out_ref):
    o_ref[...] = _sigmoid(jnp.dot(h_ref[...], w_ref[...].astype(BF16), preferred_element_type=F32))
    side_out_ref[...] = side_ref[...].astype(side_out_ref.dtype)


def _gates_proj(h, w_in, col0, width, side):
    s, d = h.shape
    tm = PROJ_ROWS
    c0 = col0 // COL
    n_j = width // COL
    slab = side.shape[0] // ((s // tm) * n_j)
    assert slab * (s // tm) * n_j == side.shape[0]
    return pl.pallas_call(
        _gates_kernel,
        out_shape=(jax.ShapeDtypeStruct((s, width), F32), jax.ShapeDtypeStruct(side.shape, BF16)),
        grid=(s // tm, n_j),
        in_specs=[pl.BlockSpec((tm, d), lambda i, j: (i, 0)),
                  pl.BlockSpec((d, COL), lambda i, j: (0, c0 + j)),
                  pl.BlockSpec((slab, side.shape[1]), lambda i, j: (i * n_j + j, 0))],
        out_specs=(pl.BlockSpec((tm, COL), lambda i, j: (i, j)),
                   pl.BlockSpec((slab, side.shape[1]), lambda i, j: (i * n_j + j, 0))),
        compiler_params=_params(("parallel", "arbitrary")),
        name="gates",
    )(h, w_in, side)


ATTN_Q_BLOCKS = 4


def _attn_kernel(q_ref, kc_ref, kp_ref, vc_ref, vp_ref, o_ref, l_ref):
    n = pl.program_id(1)
    qi = lax.broadcasted_iota(jnp.int32, (ATTN_BLOCK, 2 * ATTN_BLOCK), 0)
    kj = lax.broadcasted_iota(jnp.int32, (ATTN_BLOCK, 2 * ATTN_BLOCK), 1)
    band = jnp.abs(kj - qi - ATTN_BLOCK // 2) <= ATTN_BLOCK // 2
    first_lo = jnp.where(n == 0, ATTN_BLOCK, 0)
    band_first = band & (kj >= first_lo)
    scale = HEAD_DIM ** -0.5
    for hh in range(HEADS_PER_GROUP):
        cols = slice(hh * HEAD_DIM, (hh + 1) * HEAD_DIM)
        for b in range(ATTN_Q_BLOCKS):
            rows = slice(b * ATTN_BLOCK, (b + 1) * ATTN_BLOCK)
            q = q_ref[rows, cols]
            if b == 0:
                k = jnp.concatenate([kp_ref[:, cols], kc_ref[0:ATTN_BLOCK, cols]], axis=0)
                v = jnp.concatenate([vp_ref[:, cols], vc_ref[0:ATTN_BLOCK, cols]], axis=0)
                valid = band_first
            else:
                band_rows = slice((b - 1) * ATTN_BLOCK, (b + 1) * ATTN_BLOCK)
                k = kc_ref[band_rows, cols]
                v = vc_ref[band_rows, cols]
                valid = band
            s = lax.dot_general(q, k, (((1,), (1,)), ((), ())), preferred_element_type=F32) * scale
            s = jnp.where(valid, s, MASK_VALUE)
            m = jnp.max(s, axis=-1, keepdims=True)
            p = jnp.exp(s - m)
            den = jnp.sum(p, axis=-1, keepdims=True)
            o = jnp.dot(p.astype(BF16), v, preferred_element_type=F32)
            o_ref[rows, cols] = o * (1.0 / den)
            l_ref[rows, cols] = jnp.broadcast_to(m + jnp.log(den), (ATTN_BLOCK, HEAD_DIM))


def _dilated_attention(qkv):
    dilation, sub_len, _ = qkv.shape
    qt = ATTN_Q_BLOCKS * ATTN_BLOCK

    def cur(part):
        return pl.BlockSpec((None, qt, COL), lambda r, n: (r, n, part))

    def prev(part):
        return pl.BlockSpec((None, ATTN_BLOCK, COL),
                            lambda r, n: (r, jnp.maximum(n * ATTN_Q_BLOCKS - 1, 0), part))

    out_sds = jax.ShapeDtypeStruct((dilation, sub_len, GROUP_WIDTH), F32)
    out_spec = pl.BlockSpec((None, qt, GROUP_WIDTH), lambda r, n: (r, n, 0))
    return pl.pallas_call(
        _attn_kernel,
        out_shape=(out_sds, out_sds),
        grid=(dilation, sub_len // qt),
        in_specs=[cur(0), cur(1), prev(1), cur(2), prev(2)],
        out_specs=(out_spec, out_spec),
        compiler_params=_params(("parallel", "arbitrary")),
        name=f"attn_d{dilation}",
    )(qkv, qkv, qkv, qkv, qkv)


def _mix_kernel(o0_ref, o1_ref, o2_ref, l0_ref, l1_ref, l2_ref, u_ref, uh_ref, ga_ref, gc_ref,
                wao_ref, cw_ref, cb_ref, lg_ref, lb_ref, wco_ref, side_ref,
                out_ref, side_out_ref, ucat_ref, nat_ref, shift_ref):
    i = pl.program_id(0)
    tm = u_ref.shape[0]
    side_out_ref[...] = side_ref[...].astype(side_out_ref.dtype)

    def natural(ref, slot):
        dilation, rows, _ = ref.shape
        if dilation == 1:
            return ref[0]
        for hh in range(HEADS_PER_GROUP):
            for r in range(dilation):
                nat_ref[slot, hh, pl.ds(r, rows, stride=dilation), :] = ref[r, :, hh * HEAD_DIM:(hh + 1) * HEAD_DIM]
        return jnp.concatenate([nat_ref[slot, hh] for hh in range(HEADS_PER_GROUP)], axis=1)

    l0, l1, l2 = natural(l0_ref, 0), natural(l1_ref, 1), natural(l2_ref, 2)
    m = jnp.maximum(jnp.maximum(l0, l1), l2)
    e0, e1, e2 = jnp.exp(l0 - m), jnp.exp(l1 - m), jnp.exp(l2 - m)
    attn = (e0 * natural(o0_ref, 3) + e1 * natural(o1_ref, 4) + e2 * natural(o2_ref, 5)) / (e0 + e1 + e2)
    attn_o = jnp.dot(attn.astype(BF16), wao_ref[...], preferred_element_type=F32)
    ucat_ref[0:CONV_HALO, :] = jnp.where(i > 0, uh_ref[...], 0.0)
    ucat_ref[CONV_HALO:, :] = u_ref[...]
    off = CONV_HALO - (CONV_WIDTH - 1)
    span = tm + CONV_HALO - SUBLANES
    for s in range(1, SUBLANES):
        shift_ref[s - 1] = ucat_ref[s:s + span, :]
    conv = jnp.zeros(u_ref.shape, F32) + cb_ref[...]
    for w in range(CONV_WIDTH):
        q, s = divmod(off + w, SUBLANES)
        src = ucat_ref if s == 0 else shift_ref.at[s - 1]
        conv = conv + src[q * SUBLANES:q * SUBLANES + tm, :] * cw_ref[w:w + 1, :]
    mu = jnp.mean(conv, axis=-1, keepdims=True)
    cen = conv - mu
    var = jnp.mean(cen * cen, axis=-1, keepdims=True)
    y = cen * lax.rsqrt(var + NORM_EPS) * lg_ref[...] + lb_ref[...]
    y = y * _sigmoid(y)
    conv_o = jnp.dot(y.astype(BF16), wco_ref[...], preferred_element_type=F32)
    out_ref[...] = (ga_ref[...] * attn_o + gc_ref[...] * conv_o).astype(out_ref.dtype)


def _mix(outs, lses, u, gates, w_attn_o, conv_w, conv_b, ln_g, ln_b, w_conv_o, side):
    s, cc = u.shape
    d = w_attn_o.shape[1]
    tm = 256
    row = lambda i: (i, 0)
    const = lambda i: (0, 0)
    cvec = pl.BlockSpec((1, cc), const)
    slab = side.shape[0] // (s // tm)
    assert slab * (s // tm) == side.shape[0]
    side_spec = pl.BlockSpec((slab, side.shape[1]), row)

    def grp(arr):
        dilation = arr.shape[0]
        return pl.BlockSpec((dilation, tm // dilation, GROUP_WIDTH), lambda i: (0, i, 0))

    return pl.pallas_call(
        _mix_kernel,
        out_shape=(jax.ShapeDtypeStruct((s, d), BF16), jax.ShapeDtypeStruct(side.shape, BF16)),
        grid=(s // tm,),
        in_specs=[grp(a) for a in (*outs, *lses)] + [
            pl.BlockSpec((tm, cc), row),
            pl.BlockSpec((CONV_HALO, cc), lambda i: (jnp.maximum(i * (tm // CONV_HALO) - 1, 0), 0)),
            pl.BlockSpec((tm, d), lambda i: (i, 0)),
            pl.BlockSpec((tm, d), lambda i: (i, 1)),
            pl.BlockSpec((GROUP_WIDTH, d), const),
            pl.BlockSpec((CONV_WIDTH, cc), const),
            cvec, cvec, cvec,
            pl.BlockSpec((cc, d), const),
            side_spec],
        out_specs=(pl.BlockSpec((tm, d), row), side_spec),
        scratch_shapes=[pltpu.VMEM((tm + CONV_HALO, cc), F32),
                        pltpu.VMEM((2 * N_GROUPS, HEADS_PER_GROUP, tm, HEAD_DIM), F32),
                        pltpu.VMEM((SUBLANES - 1, tm + CONV_HALO - SUBLANES, cc), F32)],
        compiler_params=_params(("parallel",)),
        name="mix",
    )(*outs, *lses, u, u, gates, gates, w_attn_o, conv_w, conv_b, ln_g, ln_b, w_conv_o, side)


def _out_proj_kernel(m_ref, w_ref, x_ref, gate_ref, g1_ref, g2_ref, scale_ref, shift_ref,
                     x1_ref, h2t_ref):
    y = jnp.dot(m_ref[...], w_ref[...], preferred_element_type=F32)
    x1 = x_ref[...] + gate_ref[...] * (_rms(y) * g1_ref[...])
    x1_ref[...] = x1
    h2 = (_rms(x1) * g2_ref[...]) * (1.0 + scale_ref[...]) + shift_ref[...]
    h2t_ref[...] = h2.T.astype(h2t_ref.dtype)


def _out_proj(merged, w_out, x, gate1, g1, g2, scale2, shift2):
    s, d = x.shape
    tm = 512
    row = lambda i: (i, 0)
    vec = pl.BlockSpec((1, d), lambda i: (0, 0))
    return pl.pallas_call(
        _out_proj_kernel,
        out_shape=(jax.ShapeDtypeStruct((s, d), F32), jax.ShapeDtypeStruct((d, s), BF16)),
        grid=(s // tm,),
        in_specs=[pl.BlockSpec((tm, d), row), pl.BlockSpec((d, d), lambda i: (0, 0)),
                  pl.BlockSpec((tm, d), row), vec, vec, vec, vec, vec],
        out_specs=(pl.BlockSpec((tm, d), row), pl.BlockSpec((d, tm), lambda i: (0, i))),
        compiler_params=_params(("parallel",)),
        name="out_proj",
    )(merged, w_out, x, gate1, g1, g2, scale2, shift2)


def _odd_even_merge_sort(lo, hi):
    def merge(lo, hi, r):
        step = 2 * r
        if step < hi - lo:
            yield from merge(lo, hi, step)
            yield from merge(lo + r, hi, step)
            for i in range(lo + r, hi - r, step):
                yield (i, i + r)
        else:
            yield (lo, lo + r)

    if hi > lo:
        mid = lo + (hi - lo) // 2
        yield from _odd_even_merge_sort(lo, mid)
        yield from _odd_even_merge_sort(mid + 1, hi)
        yield from merge(lo, hi, 1)


def _bitonic_merge(n):
    half = n // 2
    while half >= 1:
        for i in range(n):
            if (i // half) % 2 == 0:
                yield (i, i + half)
        half //= 2


_SORT_PAIRS = {n: tuple(_odd_even_merge_sort(0, n - 1)) for n in (PEER_TOPK // 2, PEER_TOPK)}
_MERGE_PAIRS = tuple(_bitonic_merge(PEER_TOPK))


def _compare_exchange(v, pairs):
    for i, j in pairs:
        v[i], v[j] = jnp.maximum(v[i], v[j]), jnp.minimum(v[i], v[j])


def _top16_rows(x):
    blocks = x.shape[0] // SUBLANES
    assert x.shape[0] == blocks * SUBLANES and blocks <= PEER_TOPK
    n = PEER_TOPK // 2 if blocks <= PEER_TOPK // 2 else PEER_TOPK
    v = [x[k * SUBLANES:(k + 1) * SUBLANES, :] for k in range(blocks)]
    v += [jnp.full_like(v[0], NEG_INF)] * (n - blocks)
    _compare_exchange(v, _SORT_PAIRS[n])
    shift = SUBLANES // 2
    while shift >= 1:
        partner = [pltpu.roll(blk, shift, axis=0) for blk in v]
        if len(v) < PEER_TOPK:
            v = v + partner[::-1]
        else:
            v = [jnp.maximum(v[k], partner[PEER_TOPK - 1 - k]) for k in range(PEER_TOPK)]
        _compare_exchange(v, _MERGE_PAIRS)
        shift //= 2
    return [blk[0:1, :] for blk in v]


def _peer_query_kernel(h2t_ref, wqt_ref, keys_ref, e1_ref, e2_ref, theta_ref,
                       qt_ref, top_ref, cand_ref):
    qt_ref[...] = jnp.dot(wqt_ref[...], h2t_ref[...], preferred_element_type=F32)
    width = top_ref.shape[2]
    for h, part in [(h, part) for h in range(PEER_HEADS) for part in range(qt_ref.shape[1] // width)]:
        lanes = slice(part * width, (part + 1) * width)
        u = []
        for side in range(2):
            r0 = (2 * h + side) * PEER_HALF
            qc = qt_ref[r0:r0 + PEER_HALF, lanes].astype(BF16)
            sc = jnp.dot(keys_ref[h, side].astype(BF16), qc, preferred_element_type=F32)
            us = jnp.exp(sc - jnp.max(sc, axis=0, keepdims=True))
            vals = _top16_rows(us)
            for k in range(PEER_TOPK):
                top_ref[side, k:k + 1, :] = vals[k]
            u.append(us)
        cand_ref[...] = jnp.full(cand_ref.shape, -1.0, F32)
        off = 0
        for a, nb in enumerate(_CAND_ROWS):
            cand_ref[off:off + nb, :] = top_ref[0, a:a + 1, :] * top_ref[1, 0:nb, :]
            off += nb
        cand = cand_ref[...]
        selected = cand >= _top16_rows(cand)[-1]
        inv_z = 0.5 / jnp.sum(jnp.where(selected, cand, 0.0), axis=0, keepdims=True)
        off = 0
        for a, nb in enumerate(_CAND_ROWS):
            cand_ref[off:off + nb, :] = (top_ref[0, a:a + 1, :] * inv_z) * top_ref[1, 0:nb, :]
            off += nb
        theta = jnp.min(jnp.where(selected, cand_ref[...], jnp.inf), axis=0, keepdims=True)
        e1 = u[0] * inv_z
        for blk in range(PEER_N_KEYS // 8):
            e1_ref[h, blk, :, lanes] = e1[blk * 8:(blk + 1) * 8, :]
        e2_ref[h, :, lanes] = u[1]
        theta_ref[h:h + 1, lanes] = theta


PEER_QUERY_LANES = 128


def _peer_query(h2t, wq_t, sub_keys):
    d, s = h2t.shape
    tt = 512
    nq = wq_t.shape[0]
    side_blocked = jax.ShapeDtypeStruct((PEER_HEADS, PEER_N_KEYS // 8, 8, s), F32)
    side_flat = jax.ShapeDtypeStruct((PEER_HEADS, PEER_N_KEYS, s), F32)
    blocked_spec = pl.BlockSpec((PEER_HEADS, PEER_N_KEYS // 8, 8, tt), lambda t: (0, 0, 0, t))
    flat_spec = pl.BlockSpec((PEER_HEADS, PEER_N_KEYS, tt), lambda t: (0, 0, t))
    return pl.pallas_call(
        _peer_query_kernel,
        out_shape=(side_blocked, side_flat, jax.ShapeDtypeStruct((PEER_HEADS, s), F32)),
        grid=(s // tt,),
        in_specs=[pl.BlockSpec((d, tt), lambda t: (0, t)),
                  pl.BlockSpec((nq, d), lambda t: (0, 0)),
                  pl.BlockSpec(sub_keys.shape, lambda t: (0, 0, 0, 0))],
        out_specs=(blocked_spec, flat_spec, pl.BlockSpec((PEER_HEADS, tt), lambda t: (0, t))),
        scratch_shapes=[pltpu.VMEM((nq, tt), F32),
                        pltpu.VMEM((2, PEER_TOPK, PEER_QUERY_LANES), F32),
                        pltpu.VMEM((_CAND_PAD, PEER_QUERY_LANES), F32)],
        compiler_params=_params(("parallel",)),
        name="peer_query",
    )(h2t, wq_t, sub_keys)


PEER_LANE_CHUNK = 128
PEER_HALF_KEYS = 4
PEER_HALF_TILE = PEER_HALF_KEYS * PEER_N_KEYS
PEER_TILE = 2 * PEER_HALF_TILE
PEER_TOKENS = 512


def _gate_act(hid_ref, act_ref, e1_ref, e2_ref, theta_ref, half):
    tt = hid_ref.shape[2]
    for ii in range(PEER_HALF_KEYS):
        rows = slice(ii * PEER_N_KEYS, (ii + 1) * PEER_N_KEYS)
        key = half * PEER_HALF_KEYS + ii
        for c in range(tt // PEER_LANE_CHUNK):
            lanes = slice(c * PEER_LANE_CHUNK, (c + 1) * PEER_LANE_CHUNK)
            w = None
            for h in range(PEER_HEADS):
                p = e1_ref[h, 0, key:key + 1, lanes] * e2_ref[h, :, lanes]
                kept = jnp.where(p >= theta_ref[h:h + 1, lanes], p, 0.0)
                w = kept if w is None else w + kept
            act_ref[half, rows, lanes] = (hid_ref[half, rows, lanes] * w).astype(act_ref.dtype)


def _project(down, h2t):
    x = jnp.dot(down, h2t, preferred_element_type=F32)
    return (x * (1.0 + lax.erf(x * INV_SQRT2))).reshape(2, PEER_HALF_TILE, h2t.shape[1])


def _peer_project_kernel(down_ref, h2t_ref, hid_ref):
    hid_ref[...] = _project(down_ref[...], h2t_ref[...])


def _peer_project_first(h2t, down):
    d = h2t.shape[0]
    return pl.pallas_call(
        _peer_project_kernel,
        out_shape=jax.ShapeDtypeStruct((2, PEER_HALF_TILE, PEER_TOKENS), F32),
        grid=(1,),
        in_specs=[pl.BlockSpec((PEER_TILE, d), lambda i: (0, 0)),
                  pl.BlockSpec((d, PEER_TOKENS), lambda i: (0, 0))],
        out_specs=pl.BlockSpec((2, PEER_HALF_TILE, PEER_TOKENS), lambda i: (0, 0, 0)),
        compiler_params=_params(("arbitrary",)),
        name="peer_project_first",
    )(down, h2t)


def _peer_dense_kernel(h2t_ref, down_ref, up_ref, e1_ref, e2_ref, theta_ref, hid0_ref,
                       x1_ref, gate_ref, g_ref, out_ref, hid_ref, act_ref):
    t = pl.program_id(0)
    e = pl.program_id(1)

    @pl.when((t == 0) & (e == 0))
    def _():
        hid_ref[...] = hid0_ref[...]

    @pl.when(e == 0)
    def _():
        out_ref[...] = jnp.zeros(out_ref.shape, F32)

    for half in range(2):
        _gate_act(hid_ref, act_ref, e1_ref, e2_ref, theta_ref, half)
    act = act_ref[...].reshape(PEER_TILE, act_ref.shape[2])
    out_ref[...] += lax.dot_general(act, up_ref[...], (((0,), (0,)), ((), ())), preferred_element_type=F32)
    hid_ref[...] = _project(down_ref[...], h2t_ref[...])

    @pl.when(e == pl.num_programs(1) - 1)
    def _():
        y = out_ref[...]
        out_ref[...] = x1_ref[...] + gate_ref[...] * (_rms(y) * g_ref[...])


def _peer_dense(h2t, down, up, e1, e2, theta, x1, gate2, g3):
    d, s = h2t.shape
    tt = PEER_TOKENS
    n_e = down.shape[0] // PEER_TILE
    n_t = s // tt
    vec = pl.BlockSpec((1, d), lambda t, e: (0, 0))
    next_e = lambda e: (e + 1) % n_e
    next_t = lambda t, e: jnp.minimum(t + (e + 1) // n_e, n_t - 1)
    return pl.pallas_call(
        _peer_dense_kernel,
        out_shape=jax.ShapeDtypeStruct((s, d), F32),
        grid=(n_t, n_e),
        in_specs=[pl.BlockSpec((d, tt), lambda t, e: (0, next_t(t, e))),
                  pl.BlockSpec((PEER_TILE, d), lambda t, e: (next_e(e), 0)),
                  pl.BlockSpec((PEER_TILE, d), lambda t, e: (e, 0)),
                  pl.BlockSpec((PEER_HEADS, 1, 2 * PEER_HALF_KEYS, tt), lambda t, e: (0, e, 0, t)),
                  pl.BlockSpec((PEER_HEADS, PEER_N_KEYS, tt), lambda t, e: (0, 0, t)),
                  pl.BlockSpec((PEER_HEADS, tt), lambda t, e: (0, t)),
                  pl.BlockSpec((2, PEER_HALF_TILE, tt), lambda t, e: (0, 0, 0)),
                  pl.BlockSpec((tt, d), lambda t, e: (t, 0)),
                  vec, vec],
        out_specs=pl.BlockSpec((tt, d), lambda t, e: (t, 0)),
        scratch_shapes=[pltpu.VMEM((2, PEER_HALF_TILE, tt), F32),
                        pltpu.VMEM((2, PEER_HALF_TILE, tt), BF16)],
        compiler_params=_params(("arbitrary", "arbitrary")),
        name="peer_dense",
    )(h2t, down, up, e1, e2, theta, _peer_project_first(h2t, down), x1, gate2, g3)


def kernel(x, c, positions, ada_w, ada_b, norm_gains, w_in, w_attn_o, conv_w, conv_b, conv_ln_g,
           conv_ln_b, w_conv_o, w_out, peer_w_q, peer_sub_keys, peer_down, peer_up):
    batch, seq, d = x.shape
    depth = ada_w.shape[0]
    assert batch == 1, "kernels are written for a single sequence"
    assert all(window // dilation == ATTN_BLOCK for window, dilation in DILATED_GROUPS)
    cc = conv_w.shape[-1]
    xs = x[0]
    inv_freq = ROPE_THETA ** (-jnp.arange(0, HEAD_DIM, 2, dtype=F32) / HEAD_DIM)
    invf_row = jnp.concatenate([inv_freq, inv_freq])[None, :]
    cos, sin = _rope_tables(positions[0].astype(F32)[:, None], invf_row)
    row = lambda v: v[None, :]
    for l in range(depth):
        mod = _modulation(c[0][:, None], ada_w[l], ada_b[l][None, :])
        shift1, scale1, gate1, shift2, scale2, gate2 = [mod[:, k * d:(k + 1) * d] for k in range(6)]
        g = norm_gains[l]
        w_in_b = w_in[l]

        h = _norm_mod(xs, row(g[0]), scale1, shift1)
        u = _glu_proj(h, w_in_b, cc)
        gates, down_b = _gates_proj(h, w_in_b, 3 * ATTN_WIDTH + 2 * cc, 2 * d, peer_down[l])
        outs, lses = [], []
        for gi, (_, dilation) in enumerate(DILATED_GROUPS):
            o, lse = _dilated_attention(_qkv_proj(h, w_in_b, cos, sin, gi, dilation))
            outs.append(o)
            lses.append(lse)
        merged, up_b = _mix(outs, lses, u, gates, w_attn_o[l].astype(BF16), conv_w[l][:, 0, :],
                            row(conv_b[l]), row(conv_ln_g[l]), row(conv_ln_b[l]), w_conv_o[l].astype(BF16),
                            peer_up[l])
        x1, h2t = _out_proj(merged, w_out[l].astype(BF16), xs, gate1, row(g[1]), row(g[2]),
                            scale2, shift2)

        e1, e2, theta = _peer_query(h2t, peer_w_q[l].T.astype(BF16), peer_sub_keys[l])
        xs = _peer_dense(h2t, down_b, up_b, e1, e2, theta, x1, gate2, row(g[3]))
    return xs[None]
```
